```python
import math
import jax, jax.numpy as jnp
from jax import lax
import numpy as np

D_MODEL = 2048
BATCH = 4
SEQ = 2048
DEPTH = 1

HGRN_WIDTH = D_MODEL // 2
HGRN_HEAD_DIM = 128
HGRN_HEADS = HGRN_WIDTH // HGRN_HEAD_DIM
CHUNK = 64
ATTN_WIDTH = D_MODEL - HGRN_WIDTH
ATTN_HEAD_DIM = 128
ATTN_HEADS = ATTN_WIDTH // ATTN_HEAD_DIM
ATTN_KV_HEADS = 2
KV_WIDTH = ATTN_KV_HEADS * ATTN_HEAD_DIM
WINDOW = 128
ATTN_BLOCK = 128
KEY_SPAN = ATTN_BLOCK + 2 * WINDOW
REL_BUCKETS = 32
REL_MAX_DIST = 128
D_FF = 5632
EPS = 1e-6
NEG_INF = -1e30
IN_SPLITS = (HGRN_WIDTH, HGRN_WIDTH, HGRN_WIDTH, HGRN_WIDTH, HGRN_WIDTH, ATTN_WIDTH, KV_WIDTH, KV_WIDTH)
IN_COLS = sum(IN_SPLITS)

kernel_name = "hymba_hgrn2_swa_macaron_sandwich"


def rms_norm(x, gain):
    xf = x.astype(jnp.float32)
    y = xf * lax.rsqrt(jnp.mean(xf * xf, axis=-1, keepdims=True) + EPS)
    return (y * gain.astype(jnp.float32)).astype(x.dtype)


def swiglu(x, w_gate_up, w_down):
    gate, up = jnp.split(x @ w_gate_up, 2, axis=-1)
    return (jax.nn.silu(gate) * up) @ w_down


def hgrn_chunk_scan(q, k, v, log_f):
    b_, h_, l_, dk = q.shape
    dv = v.shape[-1]
    n = l_ // CHUNK
    q = q.reshape(b_, h_, n, CHUNK, dk)
    k = k.reshape(b_, h_, n, CHUNK, dk)
    log_f = log_f.reshape(b_, h_, n, CHUNK, dk)
    v = v.reshape(b_, h_, n, CHUNK, dv)
    cum = jnp.cumsum(log_f, axis=3)
    last = cum[:, :, :, -1:, :]
    q_dec = q * jnp.exp(cum)
    k_dec = k * jnp.exp(-cum)
    k_tail = k * jnp.exp(last - cum)
    lower = jnp.tril(jnp.ones((CHUNK, CHUNK), dtype=bool))
    scores = jnp.einsum('bhnck,bhnsk->bhncs', q_dec, k_dec)
    scores = jnp.where(lower, scores, 0.0)
    o_intra = jnp.einsum('bhncs,bhnsv->bhncv', scores, v)
    kv_chunk = jnp.einsum('bhnsk,bhnsv->bhnkv', k_tail, v)
    chunk_decay = jnp.exp(last[:, :, :, 0, :])

    def step(state, inp):
        kv_n, dec_n = inp
        return dec_n[..., None] * state + kv_n, state

    init = jnp.zeros((b_, h_, dk, dv), kv_chunk.dtype)
    _, prev = lax.scan(step, init, (jnp.moveaxis(kv_chunk, 2, 0), jnp.moveaxis(chunk_decay, 2, 0)))
    prev = jnp.moveaxis(prev, 0, 2)
    o_inter = jnp.einsum('bhnck,bhnkv->bhncv', q_dec, prev)
    return (o_intra + o_inter).reshape(b_, h_, l_, dv)


def hgrn2_mixer(q, i, f_fwd_logit, f_bwd_logit, g, lb_fwd, lb_bwd, out_gain):
    b_, l_, _ = q.shape

    def heads(t):
        return t.astype(jnp.float32).reshape(b_, l_, HGRN_HEADS, HGRN_HEAD_DIM).transpose(0, 2, 1, 3)

    qh, vh = heads(q), heads(i)

    def direction(f_logit, lb, flip):
        lb = lb.astype(jnp.float32).reshape(HGRN_HEADS, 1, HGRN_HEAD_DIM)
        f = lb + (1.0 - lb) * jax.nn.sigmoid(heads(f_logit))
        log_f, k = jnp.log(f), 1.0 - f
        qq, vv = qh, vh
        if flip:
            qq, vv, k, log_f = (jnp.flip(t, axis=2) for t in (qq, vv, k, log_f))
        o = hgrn_chunk_scan(qq, k, vv, log_f)
        return jnp.flip(o, axis=2) if flip else o

    o = direction(f_fwd_logit, lb_fwd, False) + direction(f_bwd_logit, lb_bwd, True)
    o = o.transpose(0, 2, 1, 3)
    o = o * lax.rsqrt(jnp.mean(o * o, axis=-1, keepdims=True) + EPS)
    o = o * out_gain.astype(jnp.float32).reshape(HGRN_HEADS, HGRN_HEAD_DIM)
    o = o.reshape(b_, l_, HGRN_WIDTH) * jax.nn.silu(g.astype(jnp.float32))
    return o.astype(q.dtype)


def t5_buckets(rel):
    nb = REL_BUCKETS // 2
    max_exact = nb // 2
    bucket = (rel > 0).astype(np.int32) * nb
    n = np.abs(rel)
    large = max_exact + (np.log(np.maximum(n, 1) / max_exact) / np.log(REL_MAX_DIST / max_exact)
                         * (nb - max_exact)).astype(np.int32)
    large = np.minimum(large, nb - 1)
    return bucket + np.where(n < max_exact, n, large).astype(np.int32)


def window_attention(q, k, v, sink, rel_table):
    b_, l_, _ = q.shape
    nb = l_ // ATTN_BLOCK
    grp = ATTN_HEADS // ATTN_KV_HEADS
    qb = q.reshape(b_, nb, ATTN_BLOCK, ATTN_KV_HEADS, grp, ATTN_HEAD_DIM)

    def band(t):
        tp = jnp.pad(t, ((0, 0), (WINDOW, WINDOW), (0, 0)))
        tp = tp.reshape(b_, nb + 2, ATTN_BLOCK, ATTN_KV_HEADS, ATTN_HEAD_DIM)
        return jnp.concatenate([tp[:, :-2], tp[:, 1:-1], tp[:, 2:]], axis=2)

    kb, vb = band(k), band(v)
    scores = jnp.einsum('bncxgd,bnsxd->bxgncs', qb, kb).astype(jnp.float32) / math.sqrt(ATTN_HEAD_DIM)
    c = np.arange(ATTN_BLOCK)[:, None]
    s = np.arange(KEY_SPAN)[None, :]
    rel = s - WINDOW - c
    bias = rel_table.astype(jnp.float32)[t5_buckets(rel)]
    bias = jnp.transpose(bias, (2, 0, 1)).reshape(ATTN_KV_HEADS, grp, 1, ATTN_BLOCK, KEY_SPAN)
    key_pos = np.arange(nb)[:, None, None] * ATTN_BLOCK - WINDOW + s[None]
    valid = (np.abs(rel)[None] <= WINDOW) & (key_pos >= 0) & (key_pos < l_)
    scores = jnp.where(valid, scores + bias, NEG_INF)
    sink_col = jnp.broadcast_to(sink.astype(jnp.float32).reshape(1, ATTN_KV_HEADS, grp, 1, 1, 1),
                                scores.shape[:-1] + (1,))
    probs = jax.nn.softmax(jnp.concatenate([scores, sink_col], axis=-1), axis=-1)[..., :KEY_SPAN]
    out = jnp.einsum('bxgncs,bnsxd->bncxgd', probs.astype(v.dtype), vb)
    return out.reshape(b_, l_, ATTN_WIDTH)


def setup_inputs(seed: int = 0) -> dict:
    key = jax.random.key(seed)
    ks = jax.random.split(key, 20)
    f32 = jnp.float32

    def w(k, shape, fan_in):
        return jax.random.normal(k, shape, f32) * fan_in ** -0.5

    def gain(k, shape):
        return 1.0 + 0.02 * jax.random.normal(k, shape, f32)

    return {
        "x": jax.random.normal(ks[0], (BATCH, SEQ, D_MODEL), f32),
        "pre_norm_ffn1": gain(ks[1], (DEPTH, D_MODEL)),
        "post_norm_ffn1": gain(ks[2], (DEPTH, D_MODEL)),
        "w_ffn1_gate_up": w(ks[3], (DEPTH, D_MODEL, 2 * D_FF), D_MODEL),
        "w_ffn1_down": w(ks[4], (DEPTH, D_FF, D_MODEL), D_FF),
        "pre_norm_mix": gain(ks[5], (DEPTH, D_MODEL)),
        "post_norm_mix": gain(ks[6], (DEPTH, D_MODEL)),
        "w_mix_in": w(ks[7], (DEPTH, D_MODEL, IN_COLS), D_MODEL),
        "hgrn_lower_bounds_fwd": 0.1 * jax.random.normal(ks[8], (DEPTH + 1, HGRN_WIDTH), f32),
        "hgrn_lower_bounds_bwd": 0.1 * jax.random.normal(ks[9], (DEPTH + 1, HGRN_WIDTH), f32),
        "hgrn_out_norm": gain(ks[10], (DEPTH, HGRN_WIDTH)),
        "attn_sink": 0.5 * jax.random.normal(ks[11], (DEPTH, ATTN_HEADS), f32),
        "w_mix_out": w(ks[12], (DEPTH, HGRN_WIDTH + ATTN_WIDTH, D_MODEL), HGRN_WIDTH + ATTN_WIDTH),
        "pre_norm_ffn2": gain(ks[13], (DEPTH, D_MODEL)),
        "post_norm_ffn2": gain(ks[14], (DEPTH, D_MODEL)),
        "w_ffn2_gate_up": w(ks[15], (DEPTH, D_MODEL, 2 * D_FF), D_MODEL),
        "w_ffn2_down": w(ks[16], (DEPTH, D_FF, D_MODEL), D_FF),
        "rel_bias_table": 0.5 * jax.random.normal(ks[17], (REL_BUCKETS, ATTN_HEADS), f32),
    }


def reference(x, pre_norm_ffn1, post_norm_ffn1, w_ffn1_gate_up, w_ffn1_down, pre_norm_mix,
              post_norm_mix, w_mix_in, hgrn_lower_bounds_fwd, hgrn_lower_bounds_bwd, hgrn_out_norm,
              attn_sink, w_mix_out, pre_norm_ffn2, post_norm_ffn2, w_ffn2_gate_up, w_ffn2_down,
              rel_bias_table):
    lb_fwd_all = jnp.cumsum(jax.nn.softmax(hgrn_lower_bounds_fwd.astype(jnp.float32), axis=0), axis=0)
    lb_bwd_all = jnp.cumsum(jax.nn.softmax(hgrn_lower_bounds_bwd.astype(jnp.float32), axis=0), axis=0)
    split_at = np.cumsum(IN_SPLITS)[:-1].tolist()
    for layer in range(DEPTH):
        ff = swiglu(rms_norm(x, pre_norm_ffn1[layer]), w_ffn1_gate_up[layer], w_ffn1_down[layer])
        x = x + 0.5 * rms_norm(ff, post_norm_ffn1[layer])
        h = rms_norm(x, pre_norm_mix[layer])
        q_h, i_h, f_fwd, f_bwd, g_h, q_a, k_a, v_a = jnp.split(h @ w_mix_in[layer], split_at, axis=-1)
        y_h = hgrn2_mixer(q_h, i_h, f_fwd, f_bwd, g_h, lb_fwd_all[layer], lb_bwd_all[layer],
                          hgrn_out_norm[layer])
        y_a = window_attention(q_a, k_a, v_a, attn_sink[layer], rel_bias_table)
        mixed = jnp.concatenate([y_h.astype(x.dtype), y_a.astype(x.dtype)], axis=-1) @ w_mix_out[layer]
        x = x + rms_norm(mixed, post_norm_mix[layer])
        ff = swiglu(rms_norm(x, pre_norm_ffn2[layer]), w_ffn2_gate_up[layer], w_ffn2_down[layer])
        x = x + 0.5 * rms_norm(ff, post_norm_ffn2[layer])
    return x
```

```python
import functools
import math

import jax
import jax.numpy as jnp
import numpy as np
from jax import lax
from jax.experimental import pallas as pl
from jax.experimental.pallas import tpu as pltpu

F32 = jnp.float32
BF16 = jnp.bfloat16

D_MODEL = 2048
HGRN_WIDTH = 1024
HEAD_DIM = 128
HGRN_HEADS = HGRN_WIDTH // HEAD_DIM
CHUNK = 64
ATTN_WIDTH = 1024
ATTN_HEADS = ATTN_WIDTH // HEAD_DIM
KV_HEADS = 2
GROUP = ATTN_HEADS // KV_HEADS
KV_WIDTH = KV_HEADS * HEAD_DIM
WINDOW = 128
ATTN_BLOCK = 128
KEY_SPAN = ATTN_BLOCK + 2 * WINDOW
REL_BUCKETS = 32
REL_MAX_DIST = 128
D_FF = 5632
EPS = 1e-6
NEG_INF = -1e30
IN_COLS = 5 * HGRN_WIDTH + ATTN_WIDTH + 2 * KV_WIDTH

MIB = 1024 * 1024
VMEM_LIMIT_BYTES = 56 * MIB


def _rms(x, gain):
    return x * lax.rsqrt(jnp.mean(x * x, axis=-1, keepdims=True) + EPS) * gain


def _sigmoid(x):
    return 1.0 / (1.0 + jnp.exp(-x))


def _ffn_kernel(x_ref, gpre_ref, gpost_ref, wg_ref, wu_ref, wd_ref, o_ref, h_ref, acc_ref):
    j = pl.program_id(1)

    @pl.when(j == 0)
    def _():
        h_ref[...] = _rms(x_ref[...], gpre_ref[...]).astype(BF16)

    h = h_ref[...]
    gate = jnp.dot(h, wg_ref[...], preferred_element_type=F32)
    up = jnp.dot(h, wu_ref[...], preferred_element_type=F32)
    act = (gate * _sigmoid(gate) * up).astype(BF16)
    part = jnp.dot(act, wd_ref[...], preferred_element_type=F32)

    @pl.when(j == 0)
    def _():
        acc_ref[...] = part

    @pl.when(j > 0)
    def _():
        acc_ref[...] += part

    @pl.when(j == pl.num_programs(1) - 1)
    def _():
        o_ref[...] = x_ref[...] + 0.5 * _rms(acc_ref[...], gpost_ref[...])


def _ffn(x, gpre, gpost, w_gate_up, w_down, *, tm=512, tf=512):
    m, d = x.shape
    nj = D_FF // tf
    return pl.pallas_call(
        _ffn_kernel,
        grid=(m // tm, nj),
        in_specs=[
            pl.BlockSpec((tm, d), lambda i, j: (i, 0)),
            pl.BlockSpec((1, d), lambda i, j: (0, 0)),
            pl.BlockSpec((1, d), lambda i, j: (0, 0)),
            pl.BlockSpec((d, tf), lambda i, j: (0, j)),
            pl.BlockSpec((d, tf), lambda i, j: (0, j + nj)),
            pl.BlockSpec((tf, d), lambda i, j: (j, 0)),
        ],
        out_specs=pl.BlockSpec((tm, d), lambda i, j: (i, 0)),
        out_shape=jax.ShapeDtypeStruct((m, d), F32),
        scratch_shapes=[pltpu.VMEM((tm, d), BF16), pltpu.VMEM((tm, d), F32)],
        compiler_params=pltpu.CompilerParams(
            dimension_semantics=("parallel", "arbitrary"), vmem_limit_bytes=VMEM_LIMIT_BYTES),
        name="ffn",
    )(x, gpre, gpost, w_gate_up, w_gate_up, w_down)


def _inproj_kernel(x_ref, g_ref, w_ref, o_ref, h_ref):
    @pl.when(pl.program_id(1) == 0)
    def _():
        h_ref[...] = _rms(x_ref[...], g_ref[...]).astype(BF16)

    o_ref[...] = jnp.dot(h_ref[...], w_ref[...], preferred_element_type=F32)


def _inproj(x, gain, w, *, tm=1024, tn=512):
    m, d = x.shape
    n = w.shape[1]
    return pl.pallas_call(
        _inproj_kernel,
        grid=(m // tm, n // tn),
        in_specs=[
            pl.BlockSpec((tm, d), lambda i, j: (i, 0)),
            pl.BlockSpec((1, d), lambda i, j: (0, 0)),
            pl.BlockSpec((d, tn), lambda i, j: (0, j)),
        ],
        out_specs=pl.BlockSpec((tm, tn), lambda i, j: (i, j)),
        out_shape=jax.ShapeDtypeStruct((m, n), F32),
        scratch_shapes=[pltpu.VMEM((tm, d), BF16)],
        compiler_params=pltpu.CompilerParams(
            dimension_semantics=("parallel", "arbitrary"), vmem_limit_bytes=VMEM_LIMIT_BYTES),
        name="inproj",
    )(x, gain, w)


def _chunk_cumsum(x, row_in_chunk, reverse):
    n_rows = x.shape[0]
    c = x
    s = 1
    while s < CHUNK:
        if reverse:
            shifted = pltpu.roll(c, n_rows - s, axis=0)
            keep = row_in_chunk < CHUNK - s
        else:
            shifted = pltpu.roll(c, s, axis=0)
            keep = row_in_chunk >= s
        c = c + jnp.where(keep, shifted, 0.0)
        s *= 2
    return c


def _hgrn_kernel(q_ref, v_ref, ff_ref, fb_ref, g_ref, lbf_ref, lbb_ref, gain_ref, o_ref,
                 qd_ref, kd_ref, kt_ref, dec_ref, vb_ref, of_ref, ob_ref):
    seq = q_ref.shape[0]
    n_chunks = seq // CHUNK
    q = q_ref[...]
    vb_ref[...] = v_ref[...].astype(BF16)
    row_in_chunk = lax.broadcasted_iota(jnp.int32, (seq, HEAD_DIM), 0) % CHUNK

    def prepare(d, f_logit_ref, lb_param_ref, reverse):
        a = lb_param_ref[...]
        e = jnp.exp(a - jnp.max(a, axis=0, keepdims=True))
        lb = e[0:1] / jnp.sum(e, axis=0, keepdims=True)
        f = lb + (1.0 - lb) * _sigmoid(f_logit_ref[...])
        log_f = jnp.log(f)
        k = 1.0 - f
        c = _chunk_cumsum(log_f, row_in_chunk, reverse)
        c3 = c.reshape(n_chunks, CHUNK, HEAD_DIM)
        tot = c3[:, 0:1, :] if reverse else c3[:, CHUNK - 1:CHUNK, :]
        tail = jnp.exp(tot - c3).reshape(seq, HEAD_DIM)
        qd_ref[d] = (q * jnp.exp(c)).astype(BF16)
        kd_ref[d] = (k * jnp.exp(-c)).astype(BF16)
        kt_ref[d] = (k * tail).astype(BF16)
        dec_ref[d] = jnp.exp(tot).reshape(n_chunks, HEAD_DIM)

    prepare(0, ff_ref, lbf_ref, False)
    prepare(1, fb_ref, lbb_ref, True)

    r = lax.broadcasted_iota(jnp.int32, (CHUNK, CHUNK), 0)
    s = lax.broadcasted_iota(jnp.int32, (CHUNK, CHUNK), 1)
    masks = (r >= s, s >= r)
    out_refs = (of_ref, ob_ref)

    def chunk_step(d, n, state_t):
        rows = pl.ds(pl.multiple_of(n * CHUNK, CHUNK), CHUNK)
        qd = qd_ref[d, rows, :]
        kd = kd_ref[d, rows, :]
        kt = kt_ref[d, rows, :]
        vc = vb_ref[rows, :]
        scores = lax.dot_general(qd, kd, (((1,), (1,)), ((), ())), preferred_element_type=F32)
        scores = jnp.where(masks[d], scores, 0.0).astype(BF16)
        o = jnp.dot(scores, vc, preferred_element_type=F32)
        o = o + lax.dot_general(qd, state_t.astype(BF16), (((1,), (1,)), ((), ())),
                                preferred_element_type=F32)
        out_refs[d][rows, :] = o
        kv_t = lax.dot_general(vc, kt, (((0,), (0,)), ((), ())), preferred_element_type=F32)
        return state_t * dec_ref[d, pl.ds(n, 1), :] + kv_t

    def body(t, carry):
        sf, sb = carry
        sf = chunk_step(0, t, sf)
        sb = chunk_step(1, n_chunks - 1 - t, sb)
        return sf, sb

    zero = jnp.zeros((HEAD_DIM, HEAD_DIM), F32)
    lax.fori_loop(0, n_chunks, body, (zero, zero))

    o = of_ref[...] + ob_ref[...]
    o = _rms(o, gain_ref[...])
    g = g_ref[...]
    o_ref[...] = (o * (g * _sigmoid(g))).astype(o_ref.dtype)


def _hgrn(p, lbf, lbb, out_gain, batch, seq):
    hw = HGRN_HEADS

    def col(group):
        return lambda b, h: (b, group * hw + h)

    blk = (seq, HEAD_DIM)
    return pl.pallas_call(
        _hgrn_kernel,
        grid=(batch, hw),
        in_specs=[
            pl.BlockSpec(blk, col(0)),
            pl.BlockSpec(blk, col(1)),
            pl.BlockSpec(blk, col(2)),
            pl.BlockSpec(blk, col(3)),
            pl.BlockSpec(blk, col(4)),
            pl.BlockSpec((lbf.shape[0], HEAD_DIM), lambda b, h: (0, h)),
            pl.BlockSpec((lbb.shape[0], HEAD_DIM), lambda b, h: (0, h)),
            pl.BlockSpec((1, HEAD_DIM), lambda b, h: (0, h)),
        ],
        out_specs=pl.BlockSpec(blk, lambda b, h: (b, h)),
        out_shape=jax.ShapeDtypeStruct((batch * seq, HGRN_WIDTH), BF16),
        scratch_shapes=[
            pltpu.VMEM((2, seq, HEAD_DIM), BF16),
            pltpu.VMEM((2, seq, HEAD_DIM), BF16),
            pltpu.VMEM((2, seq, HEAD_DIM), BF16),
            pltpu.VMEM((2, seq // CHUNK, HEAD_DIM), F32),
            pltpu.VMEM((seq, HEAD_DIM), BF16),
            pltpu.VMEM((seq, HEAD_DIM), F32),
            pltpu.VMEM((seq, HEAD_DIM), F32),
        ],
        compiler_params=pltpu.CompilerParams(
            dimension_semantics=("parallel", "parallel"), vmem_limit_bytes=VMEM_LIMIT_BYTES),
        name="hgrn",
    )(p, p, p, p, p, lbf, lbb, out_gain)


def _t5_bucket_table():
    nb = REL_BUCKETS // 2
    max_exact = nb // 2
    c = np.arange(ATTN_BLOCK)[:, None]
    s = np.arange(KEY_SPAN)[None, :]
    rel = s - WINDOW - c
    bucket = (rel > 0).astype(np.int32) * nb
    n = np.abs(rel)
    large = max_exact + (np.log(np.maximum(n, 1) / max_exact) / np.log(REL_MAX_DIST / max_exact)
                         * (nb - max_exact)).astype(np.int32)
    large = np.minimum(large, nb - 1)
    bucket = bucket + np.where(n < max_exact, n, large).astype(np.int32)
    return np.where(np.abs(rel) <= WINDOW, bucket, -1).astype(np.int32)


def _attn_kernel(table_ref, sink_ref, bucket_ref, q_ref, k_ref, v_ref, o_ref,
                 bias_ref, kb_ref, vb_ref):
    x = pl.program_id(1)
    seq = q_ref.shape[0]
    n_blocks = seq // ATTN_BLOCK
    scale = 1.0 / math.sqrt(HEAD_DIM)

    kb_ref[...] = k_ref[...].astype(BF16)
    vb_ref[...] = v_ref[...].astype(BF16)

    bucket = bucket_ref[...]
    for g in range(GROUP):
        head = x * GROUP + g
        bias = jnp.full((ATTN_BLOCK, KEY_SPAN), NEG_INF, F32)
        for b in range(REL_BUCKETS):
            bias = jnp.where(bucket == b, table_ref[b, head], bias)
        bias_ref[pl.ds(g * ATTN_BLOCK, ATTN_BLOCK), :] = bias

    rows = GROUP * ATTN_BLOCK
    row_head = lax.broadcasted_iota(jnp.int32, (rows, 1), 0) // ATTN_BLOCK
    sink = jnp.zeros((rows, 1), F32)
    for g in range(GROUP):
        sink = jnp.where(row_head == g, sink_ref[0, x * GROUP + g], sink)
    key_col = lax.broadcasted_iota(jnp.int32, (rows, KEY_SPAN), 1)

    def body(i, carry):
        prev = jnp.maximum(i - 1, 0)
        nxt = jnp.minimum(i + 1, n_blocks - 1)

        def blk(ref, idx):
            return ref[pl.ds(pl.multiple_of(idx * ATTN_BLOCK, ATTN_BLOCK), ATTN_BLOCK), :]

        qb = blk(q_ref, i)
        qs = jnp.concatenate([qb[:, g * HEAD_DIM:(g + 1) * HEAD_DIM] for g in range(GROUP)],
                             axis=0).astype(BF16)
        kcat = jnp.concatenate([blk(kb_ref, prev), blk(kb_ref, i), blk(kb_ref, nxt)], axis=0)
        vcat = jnp.concatenate([blk(vb_ref, prev), blk(vb_ref, i), blk(vb_ref, nxt)], axis=0)
        sc = lax.dot_general(qs, kcat, (((1,), (1,)), ((), ())), preferred_element_type=F32)
        sc = sc * scale + bias_ref[...]
        key_pos = (i - 1) * ATTN_BLOCK + key_col
        sc = jnp.where((key_pos >= 0) & (key_pos < seq), sc, NEG_INF)
        m = jnp.maximum(jnp.max(sc, axis=-1, keepdims=True), sink)
        pr = jnp.exp(sc - m)
        den = jnp.sum(pr, axis=-1, keepdims=True) + jnp.exp(sink - m)
        o = jnp.dot(pr.astype(BF16), vcat, preferred_element_type=F32) / den
        ob = jnp.concatenate([o[g * ATTN_BLOCK:(g + 1) * ATTN_BLOCK] for g in range(GROUP)], axis=1)
        o_ref[pl.ds(pl.multiple_of(i * ATTN_BLOCK, ATTN_BLOCK), ATTN_BLOCK), :] = ob.astype(o_ref.dtype)
        return carry

    lax.fori_loop(0, n_blocks, body, 0)


def _attention(p, sink, rel_table, batch, seq):
    q_col0 = 5 * HGRN_WIDTH // (GROUP * HEAD_DIM)
    k_col0 = (5 * HGRN_WIDTH + ATTN_WIDTH) // HEAD_DIM
    v_col0 = k_col0 + KV_HEADS
    bucket = jnp.asarray(_t5_bucket_table())
    smem = pl.BlockSpec(memory_space=pltpu.SMEM)
    return pl.pallas_call(
        _attn_kernel,
        grid=(batch, KV_HEADS),
        in_specs=[
            smem,
            smem,
            pl.BlockSpec((ATTN_BLOCK, KEY_SPAN), lambda b, x: (0, 0)),
            pl.BlockSpec((seq, GROUP * HEAD_DIM), lambda b, x: (b, q_col0 + x)),
            pl.BlockSpec((seq, HEAD_DIM), lambda b, x: (b, k_col0 + x)),
            pl.BlockSpec((seq, HEAD_DIM), lambda b, x: (b, v_col0 + x)),
        ],
        out_specs=pl.BlockSpec((seq, GROUP * HEAD_DIM), lambda b, x: (b, x)),
        out_shape=jax.ShapeDtypeStruct((batch * seq, ATTN_WIDTH), BF16),
        scratch_shapes=[
            pltpu.VMEM((GROUP * ATTN_BLOCK, KEY_SPAN), F32),
            pltpu.VMEM((seq, HEAD_DIM), BF16),
            pltpu.VMEM((seq, HEAD_DIM), BF16),
        ],
        compiler_params=pltpu.CompilerParams(
            dimension_semantics=("parallel", "parallel"), vmem_limit_bytes=VMEM_LIMIT_BYTES),
        name="attention",
    )(rel_table, sink, bucket, p, p, p)


def _outproj_kernel(yh_ref, ya_ref, x_ref, wh_ref, wa_ref, g_ref, o_ref):
    mixed = jnp.dot(yh_ref[...], wh_ref[...], preferred_element_type=F32)
    mixed = mixed + jnp.dot(ya_ref[...], wa_ref[...], preferred_element_type=F32)
    o_ref[...] = x_ref[...] + _rms(mixed, g_ref[...])


def _outproj(y_h, y_a, x, w_out, gain, *, tm=512):
    m, d = x.shape
    return pl.pallas_call(
        _outproj_kernel,
        grid=(m // tm,),
        in_specs=[
            pl.BlockSpec((tm, HGRN_WIDTH), lambda i: (i, 0)),
            pl.BlockSpec((tm, ATTN_WIDTH), lambda i: (i, 0)),
            pl.BlockSpec((tm, d), lambda i: (i, 0)),
            pl.BlockSpec((HGRN_WIDTH, d), lambda i: (0, 0)),
            pl.BlockSpec((ATTN_WIDTH, d), lambda i: (1, 0)),
            pl.BlockSpec((1, d), lambda i: (0, 0)),
        ],
        out_specs=pl.BlockSpec((tm, d), lambda i: (i, 0)),
        out_shape=jax.ShapeDtypeStruct((m, d), F32),
        compiler_params=pltpu.CompilerParams(
            dimension_semantics=("parallel",), vmem_limit_bytes=VMEM_LIMIT_BYTES),
        name="outproj",
    )(y_h, y_a, x, w_out, w_out, gain)


def kernel(x, pre_norm_ffn1, post_norm_ffn1, w_ffn1_gate_up, w_ffn1_down, pre_norm_mix, post_norm_mix,
           w_mix_in, hgrn_lower_bounds_fwd, hgrn_lower_bounds_bwd, hgrn_out_norm, attn_sink, w_mix_out,
           pre_norm_ffn2, post_norm_ffn2, w_ffn2_gate_up, w_ffn2_down, rel_bias_table):
    batch, seq, d = x.shape
    depth = pre_norm_ffn1.shape[0]
    assert depth == 1 and d == D_MODEL
    xf = x.reshape(batch * seq, d)
    layer = 0
    xf = _ffn(xf, pre_norm_ffn1[layer:layer + 1], post_norm_ffn1[layer:layer + 1],
              w_ffn1_gate_up[layer].astype(BF16), w_ffn1_down[layer].astype(BF16))
    p = _inproj(xf, pre_norm_mix[layer:layer + 1], w_mix_in[layer].astype(BF16))
    y_h = _hgrn(p, hgrn_lower_bounds_fwd, hgrn_lower_bounds_bwd, hgrn_out_norm[layer:layer + 1],
                batch, seq)
    y_a = _attention(p, attn_sink[layer:layer + 1], rel_bias_table, batch, seq)
    xf = _outproj(y_h, y_a, xf, w_mix_out[layer].astype(BF16), post_norm_mix[layer:layer + 1])
    xf = _ffn(xf, pre_norm_ffn2[layer:layer + 1], post_norm_ffn2[layer:layer + 1],
              w_ffn2_gate_up[layer].astype(BF16), w_ffn2_down[layer].astype(BF16))
    return xf.reshape(batch, seq, d)
```

```python
import functools
import math

import jax
import jax.numpy as jnp
import numpy as np
from jax import lax
from jax.experimental import pallas as pl
from jax.experimental.pallas import tpu as pltpu

F32 = jnp.float32
BF16 = jnp.bfloat16

D_MODEL = 2048
HGRN_WIDTH = 1024
HEAD_DIM = 128
HGRN_HEADS = HGRN_WIDTH // HEAD_DIM
CHUNK = 64
ATTN_WIDTH = 1024
ATTN_HEADS = ATTN_WIDTH // HEAD_DIM
KV_HEADS = 2
GROUP = ATTN_HEADS // KV_HEADS
KV_WIDTH = KV_HEADS * HEAD_DIM
WINDOW = 128
ATTN_BLOCK = 128
KEY_SPAN = ATTN_BLOCK + 2 * WINDOW
REL_BUCKETS = 32
REL_MAX_DIST = 128
D_FF = 5632
EPS = 1e-6
NEG_INF = -1e30
IN_COLS = 5 * HGRN_WIDTH + ATTN_WIDTH + 2 * KV_WIDTH

MIB = 1024 * 1024
VMEM_LIMIT_BYTES = 56 * MIB
ROW_BLOCK = 32


def _rms(x, gain):
    return x * lax.rsqrt(jnp.mean(x * x, axis=-1, keepdims=True) + EPS) * gain


def _sigmoid(x):
    return 1.0 / (1.0 + jnp.exp(-x))


def _for_row_blocks(n_rows, fn):
    def body(r, carry):
        fn(pl.ds(pl.multiple_of(r * ROW_BLOCK, ROW_BLOCK), ROW_BLOCK))
        return carry

    lax.fori_loop(0, n_rows // ROW_BLOCK, body, 0, unroll=4)


def _ffn_kernel(x_ref, gpre_ref, gpost_ref, wg_ref, wu_ref, wd_ref, o_ref, h_ref):
    j = pl.program_id(1)
    tm = x_ref.shape[0]

    @pl.when(j == 0)
    def _():
        def prologue(rows):
            h_ref[rows, :] = _rms(x_ref[rows, :], gpre_ref[...]).astype(BF16)
            o_ref[rows, :] = jnp.zeros((ROW_BLOCK, o_ref.shape[1]), F32)

        _for_row_blocks(tm, prologue)

    h = h_ref[...]
    gate = jnp.dot(h, wg_ref[...].astype(BF16), preferred_element_type=F32)
    up = jnp.dot(h, wu_ref[...].astype(BF16), preferred_element_type=F32)
    act = (gate * _sigmoid(gate) * up).astype(BF16)
    o_ref[...] += jnp.dot(act, wd_ref[...].astype(BF16), preferred_element_type=F32)

    @pl.when(j == pl.num_programs(1) - 1)
    def _():
        for r in range(0, tm, ROW_BLOCK):
            rows = pl.ds(r, ROW_BLOCK)
            o_ref[rows, :] = x_ref[rows, :] + 0.5 * _rms(o_ref[rows, :], gpost_ref[...])


def _ffn(x, gpre, gpost, w_gate_up, w_down, *, tm=1024, tf=256):
    m, d = x.shape
    nj = D_FF // tf
    return pl.pallas_call(
        _ffn_kernel,
        grid=(m // tm, nj),
        in_specs=[
            pl.BlockSpec((tm, d), lambda i, j: (i, 0)),
            pl.BlockSpec((1, d), lambda i, j: (0, 0)),
            pl.BlockSpec((1, d), lambda i, j: (0, 0)),
            pl.BlockSpec((d, tf), lambda i, j: (0, j)),
            pl.BlockSpec((d, tf), lambda i, j: (0, j + nj)),
            pl.BlockSpec((tf, d), lambda i, j: (j, 0)),
        ],
        out_specs=pl.BlockSpec((tm, d), lambda i, j: (i, 0)),
        out_shape=jax.ShapeDtypeStruct((m, d), F32),
        scratch_shapes=[pltpu.VMEM((tm, d), BF16)],
        compiler_params=pltpu.CompilerParams(
            dimension_semantics=("parallel", "arbitrary"), vmem_limit_bytes=VMEM_LIMIT_BYTES),
        name="ffn",
    )(x, gpre, gpost, w_gate_up, w_gate_up, w_down)


def _inproj_kernel(x_ref, g_ref, w_ref, o_ref, h_ref):
    @pl.when(pl.program_id(1) == 0)
    def _():
        def prologue(rows):
            h_ref[rows, :] = _rms(x_ref[rows, :], g_ref[...]).astype(BF16)

        _for_row_blocks(x_ref.shape[0], prologue)

    o_ref[...] = jnp.dot(h_ref[...], w_ref[...].astype(BF16), preferred_element_type=F32)


def _inproj(x, gain, w, *, tm=1024, tn=512):
    m, d = x.shape
    n = w.shape[1]
    return pl.pallas_call(
        _inproj_kernel,
        grid=(m // tm, n // tn),
        in_specs=[
            pl.BlockSpec((tm, d), lambda i, j: (i, 0)),
            pl.BlockSpec((1, d), lambda i, j: (0, 0)),
            pl.BlockSpec((d, tn), lambda i, j: (0, j)),
        ],
        out_specs=pl.BlockSpec((tm, tn), lambda i, j: (i, j)),
        out_shape=jax.ShapeDtypeStruct((m, n), F32),
        scratch_shapes=[pltpu.VMEM((tm, d), BF16)],
        compiler_params=pltpu.CompilerParams(
            dimension_semantics=("parallel", "arbitrary"), vmem_limit_bytes=VMEM_LIMIT_BYTES),
        name="inproj",
    )(x, gain, w)


def _chunk_cumsum(x, row_in_chunk, reverse):
    n_rows = x.shape[0]
    c = x
    s = 1
    while s < CHUNK:
        if reverse:
            shifted = pltpu.roll(c, n_rows - s, axis=0)
            keep = row_in_chunk < CHUNK - s
        else:
            shifted = pltpu.roll(c, s, axis=0)
            keep = row_in_chunk >= s
        c = c + jnp.where(keep, shifted, 0.0)
        s *= 2
    return c


def _hgrn_kernel(q_ref, v_ref, ff_ref, fb_ref, g_ref, lbf_ref, lbb_ref, gain_ref, o_ref,
                 qd_ref, kd_ref, kt_ref, dec_ref, vb_ref, of_ref, ob_ref):
    seq = q_ref.shape[0]
    n_chunks = seq // CHUNK
    q = q_ref[...]
    vb_ref[...] = v_ref[...].astype(BF16)
    row_in_chunk = lax.broadcasted_iota(jnp.int32, (seq, HEAD_DIM), 0) % CHUNK

    def prepare(d, f_logit_ref, lb_param_ref, reverse):
        a = lb_param_ref[...]
        e = jnp.exp(a - jnp.max(a, axis=0, keepdims=True))
        lb = e[0:1] / jnp.sum(e, axis=0, keepdims=True)
        f = lb + (1.0 - lb) * _sigmoid(f_logit_ref[...])
        log_f = jnp.log(f)
        k = 1.0 - f
        c = _chunk_cumsum(log_f, row_in_chunk, reverse)
        c3 = c.reshape(n_chunks, CHUNK, HEAD_DIM)
        tot = c3[:, 0:1, :] if reverse else c3[:, CHUNK - 1:CHUNK, :]
        tail = jnp.exp(tot - c3).reshape(seq, HEAD_DIM)
        qd_ref[d] = (q * jnp.exp(c)).astype(BF16)
        kd_ref[d] = (k * jnp.exp(-c)).astype(BF16)
        kt_ref[d] = (k * tail).astype(BF16)
        dec_ref[d] = jnp.exp(tot).reshape(n_chunks, HEAD_DIM)

    prepare(0, ff_ref, lbf_ref, False)
    prepare(1, fb_ref, lbb_ref, True)

    r = lax.broadcasted_iota(jnp.int32, (CHUNK, CHUNK), 0)
    s = lax.broadcasted_iota(jnp.int32, (CHUNK, CHUNK), 1)
    masks = (r >= s, s >= r)
    out_refs = (of_ref, ob_ref)

    def chunk_step(d, n, state_t):
        rows = pl.ds(pl.multiple_of(n * CHUNK, CHUNK), CHUNK)
        qd = qd_ref[d, rows, :]
        kd = kd_ref[d, rows, :]
        kt = kt_ref[d, rows, :]
        vc = vb_ref[rows, :]
        scores = lax.dot_general(qd, kd, (((1,), (1,)), ((), ())), preferred_element_type=F32)
        scores = jnp.where(masks[d], scores, 0.0).astype(BF16)
        o = jnp.dot(scores, vc, preferred_element_type=F32)
        o = o + lax.dot_general(qd, state_t.astype(BF16), (((1,), (1,)), ((), ())),
                                preferred_element_type=F32)
        out_refs[d][rows, :] = o
        kv_t = lax.dot_general(vc, kt, (((0,), (0,)), ((), ())), preferred_element_type=F32)
        return state_t * dec_ref[d, pl.ds(n, 1), :] + kv_t

    def body(t, carry):
        sf, sb = carry
        sf = chunk_step(0, t, sf)
        sb = chunk_step(1, n_chunks - 1 - t, sb)
        return sf, sb

    zero = jnp.zeros((HEAD_DIM, HEAD_DIM), F32)
    lax.fori_loop(0, n_chunks, body, (zero, zero))

    o = of_ref[...] + ob_ref[...]
    o = _rms(o, gain_ref[...])
    g = g_ref[...]
    o_ref[...] = (o * (g * _sigmoid(g))).astype(o_ref.dtype)


def _hgrn(p, lbf, lbb, out_gain, batch, seq):
    hw = HGRN_HEADS

    def col(group):
        return lambda b, h: (b, group * hw + h)

    blk = (seq, HEAD_DIM)
    return pl.pallas_call(
        _hgrn_kernel,
        grid=(batch, hw),
        in_specs=[
            pl.BlockSpec(blk, col(0)),
            pl.BlockSpec(blk, col(1)),
            pl.BlockSpec(blk, col(2)),
            pl.BlockSpec(blk, col(3)),
            pl.BlockSpec(blk, col(4)),
            pl.BlockSpec((lbf.shape[0], HEAD_DIM), lambda b, h: (0, h)),
            pl.BlockSpec((lbb.shape[0], HEAD_DIM), lambda b, h: (0, h)),
            pl.BlockSpec((1, HEAD_DIM), lambda b, h: (0, h)),
        ],
        out_specs=pl.BlockSpec(blk, lambda b, h: (b, h)),
        out_shape=jax.ShapeDtypeStruct((batch * seq, HGRN_WIDTH), BF16),
        scratch_shapes=[
            pltpu.VMEM((2, seq, HEAD_DIM), BF16),
            pltpu.VMEM((2, seq, HEAD_DIM), BF16),
            pltpu.VMEM((2, seq, HEAD_DIM), BF16),
            pltpu.VMEM((2, seq // CHUNK, HEAD_DIM), F32),
            pltpu.VMEM((seq, HEAD_DIM), BF16),
            pltpu.VMEM((seq, HEAD_DIM), F32),
            pltpu.VMEM((seq, HEAD_DIM), F32),
        ],
        compiler_params=pltpu.CompilerParams(
            dimension_semantics=("parallel", "parallel"), vmem_limit_bytes=VMEM_LIMIT_BYTES),
        name="hgrn",
    )(p, p, p, p, p, lbf, lbb, out_gain)


def _t5_bucket_table():
    nb = REL_BUCKETS // 2
    max_exact = nb // 2
    c = np.arange(ATTN_BLOCK)[:, None]
    s = np.arange(KEY_SPAN)[None, :]
    rel = s - WINDOW - c
    bucket = (rel > 0).astype(np.int32) * nb
    n = np.abs(rel)
    large = max_exact + (np.log(np.maximum(n, 1) / max_exact) / np.log(REL_MAX_DIST / max_exact)
                         * (nb - max_exact)).astype(np.int32)
    large = np.minimum(large, nb - 1)
    bucket = bucket + np.where(n < max_exact, n, large).astype(np.int32)
    return np.where(np.abs(rel) <= WINDOW, bucket, -1).astype(np.int32)


def _attn_kernel(table_ref, sink_ref, bucket_ref, q_ref, k_ref, v_ref, o_ref,
                 bias_ref, kb_ref, vb_ref):
    x = pl.program_id(1)
    seq = q_ref.shape[0]
    n_blocks = seq // ATTN_BLOCK
    scale = 1.0 / math.sqrt(HEAD_DIM)

    kb_ref[...] = k_ref[...].astype(BF16)
    vb_ref[...] = v_ref[...].astype(BF16)

    bucket = bucket_ref[...]
    for g in range(GROUP):
        head = x * GROUP + g
        bias = jnp.full((ATTN_BLOCK, KEY_SPAN), NEG_INF, F32)
        for b in range(REL_BUCKETS):
            bias = jnp.where(bucket == b, table_ref[b, head], bias)
        bias_ref[pl.ds(g * ATTN_BLOCK, ATTN_BLOCK), :] = bias

    rows = GROUP * ATTN_BLOCK
    row_head = lax.broadcasted_iota(jnp.int32, (rows, 1), 0) // ATTN_BLOCK
    sink = jnp.zeros((rows, 1), F32)
    for g in range(GROUP):
        sink = jnp.where(row_head == g, sink_ref[0, x * GROUP + g], sink)
    key_col = lax.broadcasted_iota(jnp.int32, (rows, KEY_SPAN), 1)

    def body(i, carry):
        prev = jnp.maximum(i - 1, 0)
        nxt = jnp.minimum(i + 1, n_blocks - 1)

        def blk(ref, idx):
            return ref[pl.ds(pl.multiple_of(idx * ATTN_BLOCK, ATTN_BLOCK), ATTN_BLOCK), :]

        qb = blk(q_ref, i)
        qs = jnp.concatenate([qb[:, g * HEAD_DIM:(g + 1) * HEAD_DIM] for g in range(GROUP)],
                             axis=0).astype(BF16)
        kcat = jnp.concatenate([blk(kb_ref, prev), blk(kb_ref, i), blk(kb_ref, nxt)], axis=0)
        vcat = jnp.concatenate([blk(vb_ref, prev), blk(vb_ref, i), blk(vb_ref, nxt)], axis=0)
        sc = lax.dot_general(qs, kcat, (((1,), (1,)), ((), ())), preferred_element_type=F32)
        sc = sc * scale + bias_ref[...]
        key_pos = (i - 1) * ATTN_BLOCK + key_col
        sc = jnp.where((key_pos >= 0) & (key_pos < seq), sc, NEG_INF)
        m = jnp.maximum(jnp.max(sc, axis=-1, keepdims=True), sink)
        pr = jnp.exp(sc - m)
        den = jnp.sum(pr, axis=-1, keepdims=True) + jnp.exp(sink - m)
        o = jnp.dot(pr.astype(BF16), vcat, preferred_element_type=F32) / den
        ob = jnp.concatenate([o[g * ATTN_BLOCK:(g + 1) * ATTN_BLOCK] for g in range(GROUP)], axis=1)
        o_ref[pl.ds(pl.multiple_of(i * ATTN_BLOCK, ATTN_BLOCK), ATTN_BLOCK), :] = ob.astype(o_ref.dtype)
        return carry

    lax.fori_loop(0, n_blocks, body, 0)


def _attention(p, sink, rel_table, batch, seq):
    q_col0 = 5 * HGRN_WIDTH // (GROUP * HEAD_DIM)
    k_col0 = (5 * HGRN_WIDTH + ATTN_WIDTH) // HEAD_DIM
    v_col0 = k_col0 + KV_HEADS
    bucket = jnp.asarray(_t5_bucket_table())
    smem = pl.BlockSpec(memory_space=pltpu.SMEM)
    return pl.pallas_call(
        _attn_kernel,
        grid=(batch, KV_HEADS),
        in_specs=[
            smem,
            smem,
            pl.BlockSpec((ATTN_BLOCK, KEY_SPAN), lambda b, x: (0, 0)),
            pl.BlockSpec((seq, GROUP * HEAD_DIM), lambda b, x: (b, q_col0 + x)),
            pl.BlockSpec((seq, HEAD_DIM), lambda b, x: (b, k_col0 + x)),
            pl.BlockSpec((seq, HEAD_DIM), lambda b, x: (b, v_col0 + x)),
        ],
        out_specs=pl.BlockSpec((seq, GROUP * HEAD_DIM), lambda b, x: (b, x)),
        out_shape=jax.ShapeDtypeStruct((batch * seq, ATTN_WIDTH), BF16),
        scratch_shapes=[
            pltpu.VMEM((GROUP * ATTN_BLOCK, KEY_SPAN), F32),
            pltpu.VMEM((seq, HEAD_DIM), BF16),
            pltpu.VMEM((seq, HEAD_DIM), BF16),
        ],
        compiler_params=pltpu.CompilerParams(
            dimension_semantics=("parallel", "parallel"), vmem_limit_bytes=VMEM_LIMIT_BYTES),
        name="attention",
    )(rel_table, sink, bucket, p, p, p)


def _outproj_kernel(yh_ref, ya_ref, x_ref, wh_ref, wa_ref, g_ref, o_ref):
    mixed = jnp.dot(yh_ref[...], wh_ref[...], preferred_element_type=F32)
    mixed = mixed + jnp.dot(ya_ref[...], wa_ref[...], preferred_element_type=F32)
    o_ref[...] = x_ref[...] + _rms(mixed, g_ref[...])


def _outproj(y_h, y_a, x, w_out, gain, *, tm=512):
    m, d = x.shape
    return pl.pallas_call(
        _outproj_kernel,
        grid=(m // tm,),
        in_specs=[
            pl.BlockSpec((tm, HGRN_WIDTH), lambda i: (i, 0)),
            pl.BlockSpec((tm, ATTN_WIDTH), lambda i: (i, 0)),
            pl.BlockSpec((tm, d), lambda i: (i, 0)),
            pl.BlockSpec((HGRN_WIDTH, d), lambda i: (0, 0)),
            pl.BlockSpec((ATTN_WIDTH, d), lambda i: (1, 0)),
            pl.BlockSpec((1, d), lambda i: (0, 0)),
        ],
        out_specs=pl.BlockSpec((tm, d), lambda i: (i, 0)),
        out_shape=jax.ShapeDtypeStruct((m, d), F32),
        compiler_params=pltpu.CompilerParams(
            dimension_semantics=("parallel",), vmem_limit_bytes=VMEM_LIMIT_BYTES),
        name="outproj",
    )(y_h, y_a, x, w_out, w_out, gain)


def kernel(x, pre_norm_ffn1, post_norm_ffn1, w_ffn1_gate_up, w_ffn1_down, pre_norm_mix, post_norm_mix,
           w_mix_in, hgrn_lower_bounds_fwd, hgrn_lower_bounds_bwd, hgrn_out_norm, attn_sink, w_mix_out,
           pre_norm_ffn2, post_norm_ffn2, w_ffn2_gate_up, w_ffn2_down, rel_bias_table):
    batch, seq, d = x.shape
    depth = pre_norm_ffn1.shape[0]
    assert depth == 1 and d == D_MODEL
    xf = x.reshape(batch * seq, d)
    layer = 0
    xf = _ffn(xf, pre_norm_ffn1[layer:layer + 1], post_norm_ffn1[layer:layer + 1],
              w_ffn1_gate_up[layer], w_ffn1_down[layer])
    p = _inproj(xf, pre_norm_mix[layer:layer + 1], w_mix_in[layer])
    y_h = _hgrn(p, hgrn_lower_bounds_fwd, hgrn_lower_bounds_bwd, hgrn_out_norm[layer:layer + 1],
                batch, seq)
    y_a = _attention(p, attn_sink[layer:layer + 1], rel_bias_table, batch, seq)
    xf = _outproj(y_h, y_a, xf, w_mix_out[layer].astype(BF16), post_norm_mix[layer:layer + 1])
    xf = _ffn(xf, pre_norm_ffn2[layer:layer + 1], post_norm_ffn2[layer:layer + 1],
              w_ffn2_gate_up[layer], w_ffn2_down[layer])
    return xf.reshape(batch, seq, d)
```

```python
import functools
import math

import jax
import jax.numpy as jnp
import numpy as np
from jax import lax
from jax.experimental import pallas as pl
from jax.experimental.pallas import tpu as pltpu

F32 = jnp.float32
BF16 = jnp.bfloat16

D_MODEL = 2048
HGRN_WIDTH = 1024
HEAD_DIM = 128
HGRN_HEADS = HGRN_WIDTH // HEAD_DIM
CHUNK = 64
ATTN_WIDTH = 1024
ATTN_HEADS = ATTN_WIDTH // HEAD_DIM
KV_HEADS = 2
GROUP = ATTN_HEADS // KV_HEADS
KV_WIDTH = KV_HEADS * HEAD_DIM
WINDOW = 128
ATTN_BLOCK = 128
KEY_SPAN = ATTN_BLOCK + 2 * WINDOW
REL_BUCKETS = 32
REL_MAX_DIST = 128
D_FF = 5632
EPS = 1e-6
NEG_INF = -1e30
LOG2_E = 1.0 / math.log(2.0)
IN_COLS = 5 * HGRN_WIDTH + ATTN_WIDTH + 2 * KV_WIDTH

MIB = 1024 * 1024
VMEM_LIMIT_BYTES = 56 * MIB
ROW_BLOCK = 32


def _rms(x, gain):
    return x * lax.rsqrt(jnp.mean(x * x, axis=-1, keepdims=True) + EPS) * gain


def _sigmoid(x):
    return 1.0 / (1.0 + jnp.exp(-x))


def _for_row_blocks(n_rows, fn):
    def body(r, carry):
        fn(pl.ds(pl.multiple_of(r * ROW_BLOCK, ROW_BLOCK), ROW_BLOCK))
        return carry

    lax.fori_loop(0, n_rows // ROW_BLOCK, body, 0, unroll=4)


def _ffn_kernel(x_ref, gpre_ref, gpost_ref, wg_ref, wu_ref, wd_ref, o_ref, h_ref):
    j = pl.program_id(1)
    tm = x_ref.shape[0]

    @pl.when(j == 0)
    def _():
        def prologue(rows):
            h_ref[rows, :] = _rms(x_ref[rows, :], gpre_ref[...]).astype(BF16)
            o_ref[rows, :] = jnp.zeros((ROW_BLOCK, o_ref.shape[1]), F32)

        _for_row_blocks(tm, prologue)

    h = h_ref[...]
    gate = jnp.dot(h, wg_ref[...].astype(BF16), preferred_element_type=F32)
    up = jnp.dot(h, wu_ref[...].astype(BF16), preferred_element_type=F32)
    act = (gate * _sigmoid(gate) * up).astype(BF16)
    o_ref[...] += jnp.dot(act, wd_ref[...].astype(BF16), preferred_element_type=F32)

    @pl.when(j == pl.num_programs(1) - 1)
    def _():
        for r in range(0, tm, ROW_BLOCK):
            rows = pl.ds(r, ROW_BLOCK)
            o_ref[rows, :] = x_ref[rows, :] + 0.5 * _rms(o_ref[rows, :], gpost_ref[...])


def _ffn(x, gpre, gpost, w_gate_up, w_down, *, tm=1024, tf=256):
    m, d = x.shape
    nj = D_FF // tf
    return pl.pallas_call(
        _ffn_kernel,
        grid=(m // tm, nj),
        in_specs=[
            pl.BlockSpec((tm, d), lambda i, j: (i, 0)),
            pl.BlockSpec((1, d), lambda i, j: (0, 0)),
            pl.BlockSpec((1, d), lambda i, j: (0, 0)),
            pl.BlockSpec((d, tf), lambda i, j: (0, j)),
            pl.BlockSpec((d, tf), lambda i, j: (0, j + nj)),
            pl.BlockSpec((tf, d), lambda i, j: (j, 0)),
        ],
        out_specs=pl.BlockSpec((tm, d), lambda i, j: (i, 0)),
        out_shape=jax.ShapeDtypeStruct((m, d), F32),
        scratch_shapes=[pltpu.VMEM((tm, d), BF16)],
        compiler_params=pltpu.CompilerParams(
            dimension_semantics=("parallel", "arbitrary"), vmem_limit_bytes=VMEM_LIMIT_BYTES),
        name="ffn",
    )(x, gpre, gpost, w_gate_up, w_gate_up, w_down)


def _inproj_kernel(x_ref, g_ref, w_ref, o_ref, h_ref):
    @pl.when(pl.program_id(1) == 0)
    def _():
        def prologue(rows):
            h_ref[rows, :] = _rms(x_ref[rows, :], g_ref[...]).astype(BF16)

        _for_row_blocks(x_ref.shape[0], prologue)

    o_ref[...] = jnp.dot(h_ref[...], w_ref[...].astype(BF16), preferred_element_type=F32)


def _inproj(x, gain, w, *, tm=1024, tn=512):
    m, d = x.shape
    n = w.shape[1]
    return pl.pallas_call(
        _inproj_kernel,
        grid=(m // tm, n // tn),
        in_specs=[
            pl.BlockSpec((tm, d), lambda i, j: (i, 0)),
            pl.BlockSpec((1, d), lambda i, j: (0, 0)),
            pl.BlockSpec((d, tn), lambda i, j: (0, j)),
        ],
        out_specs=pl.BlockSpec((tm, tn), lambda i, j: (i, j)),
        out_shape=jax.ShapeDtypeStruct((m, n), F32),
        scratch_shapes=[pltpu.VMEM((tm, d), BF16)],
        compiler_params=pltpu.CompilerParams(
            dimension_semantics=("parallel", "arbitrary"), vmem_limit_bytes=VMEM_LIMIT_BYTES),
        name="inproj",
    )(x, gain, w)


def _chunk_cumsum(x, row_in_chunk, reverse):
    n_rows = x.shape[0]
    c = x
    s = 1
    while s < CHUNK:
        if reverse:
            shifted = pltpu.roll(c, n_rows - s, axis=0)
            keep = row_in_chunk < CHUNK - s
        else:
            shifted = pltpu.roll(c, s, axis=0)
            keep = row_in_chunk >= s
        c = c + jnp.where(keep, shifted, 0.0)
        s *= 2
    return c


def _hgrn_kernel(q_ref, v_ref, ff_ref, fb_ref, g_ref, lbf_ref, lbb_ref, gain_ref, o_ref,
                 qd_ref, kv_ref, dec_ref, st_ref, oi_ref):
    seq = q_ref.shape[0]
    n_chunks = seq // CHUNK
    pair = 2 * CHUNK
    contract_last = (((1,), (1,)), ((), ()))

    def gate_consts(lb_param_ref):
        a = lb_param_ref[...]
        e = jnp.exp(a - jnp.max(a, axis=0, keepdims=True))
        lb = e[0:1] / jnp.sum(e, axis=0, keepdims=True)
        return 0.5 * (1.0 + lb), 0.5 * (1.0 - lb)

    consts = (gate_consts(lbf_ref), gate_consts(lbb_ref))
    logit_refs = (ff_ref, fb_ref)

    row_in_chunk = lax.broadcasted_iota(jnp.int32, (pair, HEAD_DIM), 0) % CHUNK
    r = lax.broadcasted_iota(jnp.int32, (pair, pair), 0)
    s = lax.broadcasted_iota(jnp.int32, (pair, pair), 1)
    same_chunk = (r // CHUNK) == (s // CHUNK)
    tri = (same_chunk & (r >= s), same_chunk & (s >= r))
    zero_blk = jnp.zeros((CHUNK, HEAD_DIM), BF16)

    def pair_body(pi, carry):
        rows = pl.ds(pl.multiple_of(pi * pair, pair), pair)
        q = q_ref[rows, :]
        v = v_ref[rows, :]
        vb = v.astype(BF16)
        v_t = v.T.astype(BF16)
        probs = None
        for d in range(2):
            mid, half = consts[d]
            t = jnp.tanh(0.5 * logit_refs[d][rows, :])
            f = mid + half * t
            k = half * (1.0 - t)
            c = _chunk_cumsum(jnp.log(f) * LOG2_E, row_in_chunk, reverse=(d == 1))
            c3 = c.reshape(2, CHUNK, HEAD_DIM)
            tot = c3[:, 0:1, :] if d == 1 else c3[:, CHUNK - 1:CHUNK, :]
            k_tail = (k * jnp.exp2(tot - c3).reshape(pair, HEAD_DIM)).astype(BF16)
            q_dec = (q * jnp.exp2(c)).astype(BF16)
            k_dec = (k * jnp.exp2(-c)).astype(BF16)
            qd_ref[rows, d * HEAD_DIM:(d + 1) * HEAD_DIM] = q_dec
            dec_ref[d, pl.ds(pi * 2, 2), :] = jnp.exp2(tot).reshape(2, HEAD_DIM)
            sc = lax.dot_general(q_dec, k_dec, contract_last, preferred_element_type=F32)
            sc = jnp.where(tri[d], sc, 0.0)
            probs = sc if probs is None else probs + sc
            rhs = jnp.concatenate(
                [jnp.concatenate([k_tail[:CHUNK], zero_blk], axis=1),
                 jnp.concatenate([zero_blk, k_tail[CHUNK:]], axis=1)], axis=0)
            kv = jnp.dot(v_t, rhs, preferred_element_type=F32)
            kv_ref[d, pi * 2] = kv[:, :HEAD_DIM]
            kv_ref[d, pi * 2 + 1] = kv[:, HEAD_DIM:]
        oi_ref[rows, :] = jnp.dot(probs.astype(BF16), vb, preferred_element_type=F32)
        return carry

    lax.fori_loop(0, n_chunks // 2, pair_body, 0, unroll=8)

    def scan_body(t, carry):
        sf, sb = carry
        nf, nb = t, n_chunks - 1 - t
        st_ref[nf, :, :HEAD_DIM] = sf.astype(BF16)
        st_ref[nb, :, HEAD_DIM:] = sb.astype(BF16)
        sf = sf * dec_ref[0, pl.ds(nf, 1), :] + kv_ref[0, nf]
        sb = sb * dec_ref[1, pl.ds(nb, 1), :] + kv_ref[1, nb]
        return sf, sb

    zero = jnp.zeros((HEAD_DIM, HEAD_DIM), F32)
    lax.fori_loop(0, n_chunks, scan_body, (zero, zero), unroll=2)

    def out_body(n, carry):
        rows = pl.ds(pl.multiple_of(n * CHUNK, CHUNK), CHUNK)
        o = oi_ref[rows, :] + lax.dot_general(qd_ref[rows, :], st_ref[n], contract_last,
                                              preferred_element_type=F32)
        o = _rms(o, gain_ref[...])
        g = g_ref[rows, :]
        o_ref[rows, :] = (o * (0.5 * g * (1.0 + jnp.tanh(0.5 * g)))).astype(o_ref.dtype)
        return carry

    lax.fori_loop(0, n_chunks, out_body, 0, unroll=16)


def _hgrn(p, lbf, lbb, out_gain, batch, seq):
    hw = HGRN_HEADS

    def col(group):
        return lambda b, h: (b, group * hw + h)

    blk = (seq, HEAD_DIM)
    return pl.pallas_call(
        _hgrn_kernel,
        grid=(batch, hw),
        in_specs=[
            pl.BlockSpec(blk, col(0)),
            pl.BlockSpec(blk, col(1)),
            pl.BlockSpec(blk, col(2)),
            pl.BlockSpec(blk, col(3)),
            pl.BlockSpec(blk, col(4)),
            pl.BlockSpec((lbf.shape[0], HEAD_DIM), lambda b, h: (0, h)),
            pl.BlockSpec((lbb.shape[0], HEAD_DIM), lambda b, h: (0, h)),
            pl.BlockSpec((1, HEAD_DIM), lambda b, h: (0, h)),
        ],
        out_specs=pl.BlockSpec(blk, lambda b, h: (b, h)),
        out_shape=jax.ShapeDtypeStruct((batch * seq, HGRN_WIDTH), BF16),
        scratch_shapes=[
            pltpu.VMEM((seq, 2 * HEAD_DIM), BF16),
            pltpu.VMEM((2, seq // CHUNK, HEAD_DIM, HEAD_DIM), F32),
            pltpu.VMEM((2, seq // CHUNK, HEAD_DIM), F32),
            pltpu.VMEM((seq // CHUNK, HEAD_DIM, 2 * HEAD_DIM), BF16),
            pltpu.VMEM((seq, HEAD_DIM), F32),
        ],
        compiler_params=pltpu.CompilerParams(
            dimension_semantics=("parallel", "parallel"), vmem_limit_bytes=VMEM_LIMIT_BYTES),
        name="hgrn",
    )(p, p, p, p, p, lbf, lbb, out_gain)


def _t5_bucket_table():
    nb = REL_BUCKETS // 2
    max_exact = nb // 2
    c = np.arange(ATTN_BLOCK)[:, None]
    s = np.arange(KEY_SPAN)[None, :]
    rel = s - WINDOW - c
    bucket = (rel > 0).astype(np.int32) * nb
    n = np.abs(rel)
    large = max_exact + (np.log(np.maximum(n, 1) / max_exact) / np.log(REL_MAX_DIST / max_exact)
                         * (nb - max_exact)).astype(np.int32)
    large = np.minimum(large, nb - 1)
    bucket = bucket + np.where(n < max_exact, n, large).astype(np.int32)
    return np.where(np.abs(rel) <= WINDOW, bucket, -1).astype(np.int32)


def _attn_kernel(table_ref, sink_ref, bucket_ref, q_ref, k_ref, v_ref, o_ref,
                 bias_ref, kb_ref, vb_ref):
    x = pl.program_id(1)
    seq = q_ref.shape[0]
    n_blocks = seq // ATTN_BLOCK
    scale = 1.0 / math.sqrt(HEAD_DIM)

    kb_ref[...] = k_ref[...].astype(BF16)
    vb_ref[...] = v_ref[...].astype(BF16)
    rows = GROUP * ATTN_BLOCK

    @pl.when((pl.program_id(0) == 0) & (x == 0))
    def _():
        bucket = bucket_ref[...]
        for head in range(ATTN_HEADS):
            bias = jnp.full((ATTN_BLOCK, KEY_SPAN), NEG_INF, F32)
            for b in range(REL_BUCKETS):
                bias = jnp.where(bucket == b, table_ref[b, head], bias)
            bias_ref[pl.ds(head * ATTN_BLOCK, ATTN_BLOCK), :] = bias

    row_head = lax.broadcasted_iota(jnp.int32, (rows, 1), 0) // ATTN_BLOCK
    sink = jnp.zeros((rows, 1), F32)
    for g in range(GROUP):
        sink = jnp.where(row_head == g, sink_ref[0, x * GROUP + g], sink)
    bias_rows = pl.ds(pl.multiple_of(x * rows, rows), rows)

    def block(i, edge):
        if edge:
            prev, nxt = max(i - 1, 0), min(i + 1, n_blocks - 1)
        else:
            prev, nxt = i - 1, i + 1

        def blk(ref, idx):
            start = idx * ATTN_BLOCK
            if not isinstance(idx, int):
                start = pl.multiple_of(start, ATTN_BLOCK)
            return ref[pl.ds(start, ATTN_BLOCK), :]

        qb = blk(q_ref, i)
        qs = jnp.concatenate([qb[:, g * HEAD_DIM:(g + 1) * HEAD_DIM] for g in range(GROUP)],
                             axis=0).astype(BF16)
        kcat = jnp.concatenate([blk(kb_ref, prev), blk(kb_ref, i), blk(kb_ref, nxt)], axis=0)
        vcat = jnp.concatenate([blk(vb_ref, prev), blk(vb_ref, i), blk(vb_ref, nxt)], axis=0)
        sc = lax.dot_general(qs, kcat, (((1,), (1,)), ((), ())), preferred_element_type=F32)
        sc = sc * scale + bias_ref[bias_rows, :]
        if edge:
            key_pos = (i - 1) * ATTN_BLOCK + lax.broadcasted_iota(jnp.int32, (rows, KEY_SPAN), 1)
            sc = jnp.where((key_pos >= 0) & (key_pos < seq), sc, NEG_INF)
        m = jnp.maximum(jnp.max(sc, axis=-1, keepdims=True), sink)
        pr = jnp.exp(sc - m)
        den = jnp.sum(pr, axis=-1, keepdims=True) + jnp.exp(sink - m)
        o = jnp.dot(pr.astype(BF16), vcat, preferred_element_type=F32) / den
        ob = jnp.concatenate([o[g * ATTN_BLOCK:(g + 1) * ATTN_BLOCK] for g in range(GROUP)], axis=1)
        out_start = i * ATTN_BLOCK if edge else pl.multiple_of(i * ATTN_BLOCK, ATTN_BLOCK)
        o_ref[pl.ds(out_start, ATTN_BLOCK), :] = ob.astype(o_ref.dtype)

    def interior(i, carry):
        block(i, edge=False)
        return carry

    block(0, edge=True)
    lax.fori_loop(1, n_blocks - 1, interior, 0, unroll=7)
    block(n_blocks - 1, edge=True)


def _attention(p, sink, rel_table, batch, seq):
    q_col0 = 5 * HGRN_WIDTH // (GROUP * HEAD_DIM)
    k_col0 = (5 * HGRN_WIDTH + ATTN_WIDTH) // HEAD_DIM
    v_col0 = k_col0 + KV_HEADS
    bucket = jnp.asarray(_t5_bucket_table())
    smem = pl.BlockSpec(memory_space=pltpu.SMEM)
    return pl.pallas_call(
        _attn_kernel,
        grid=(batch, KV_HEADS),
        in_specs=[
            smem,
            smem,
            pl.BlockSpec((ATTN_BLOCK, KEY_SPAN), lambda b, x: (0, 0)),
            pl.BlockSpec((seq, GROUP * HEAD_DIM), lambda b, x: (b, q_col0 + x)),
            pl.BlockSpec((seq, HEAD_DIM), lambda b, x: (b, k_col0 + x)),
            pl.BlockSpec((seq, HEAD_DIM), lambda b, x: (b, v_col0 + x)),
        ],
        out_specs=pl.BlockSpec((seq, GROUP * HEAD_DIM), lambda b, x: (b, x)),
        out_shape=jax.ShapeDtypeStruct((batch * seq, ATTN_WIDTH), BF16),
        scratch_shapes=[
            pltpu.VMEM((ATTN_HEADS * ATTN_BLOCK, KEY_SPAN), F32),
            pltpu.VMEM((seq, HEAD_DIM), BF16),
            pltpu.VMEM((seq, HEAD_DIM), BF16),
        ],
        compiler_params=pltpu.CompilerParams(
            dimension_semantics=("arbitrary", "arbitrary"), vmem_limit_bytes=VMEM_LIMIT_BYTES),
        name="attention",
    )(rel_table, sink, bucket, p, p, p)


def _outproj_kernel(yh_ref, ya_ref, x_ref, wh_ref, wa_ref, g_ref, o_ref):
    mixed = jnp.dot(yh_ref[...], wh_ref[...], preferred_element_type=F32)
    mixed = mixed + jnp.dot(ya_ref[...], wa_ref[...], preferred_element_type=F32)
    o_ref[...] = x_ref[...] + _rms(mixed, g_ref[...])


def _outproj(y_h, y_a, x, w_out, gain, *, tm=512):
    m, d = x.shape
    return pl.pallas_call(
        _outproj_kernel,
        grid=(m // tm,),
        in_specs=[
            pl.BlockSpec((tm, HGRN_WIDTH), lambda i: (i, 0)),
            pl.BlockSpec((tm, ATTN_WIDTH), lambda i: (i, 0)),
            pl.BlockSpec((tm, d), lambda i: (i, 0)),
            pl.BlockSpec((HGRN_WIDTH, d), lambda i: (0, 0)),
            pl.BlockSpec((ATTN_WIDTH, d), lambda i: (1, 0)),
            pl.BlockSpec((1, d), lambda i: (0, 0)),
        ],
        out_specs=pl.BlockSpec((tm, d), lambda i: (i, 0)),
        out_shape=jax.ShapeDtypeStruct((m, d), F32),
        compiler_params=pltpu.CompilerParams(
            dimension_semantics=("parallel",), vmem_limit_bytes=VMEM_LIMIT_BYTES),
        name="outproj",
    )(y_h, y_a, x, w_out, w_out, gain)


def kernel(x, pre_norm_ffn1, post_norm_ffn1, w_ffn1_gate_up, w_ffn1_down, pre_norm_mix, post_norm_mix,
           w_mix_in, hgrn_lower_bounds_fwd, hgrn_lower_bounds_bwd, hgrn_out_norm, attn_sink, w_mix_out,
           pre_norm_ffn2, post_norm_ffn2, w_ffn2_gate_up, w_ffn2_down, rel_bias_table):
    batch, seq, d = x.shape
    depth = pre_norm_ffn1.shape[0]
    assert depth == 1 and d == D_MODEL
    xf = x.reshape(batch * seq, d)
    layer = 0
    xf = _ffn(xf, pre_norm_ffn1[layer:layer + 1], post_norm_ffn1[layer:layer + 1],
              w_ffn1_gate_up[layer], w_ffn1_down[layer])
    p = _inproj(xf, pre_norm_mix[layer:layer + 1], w_mix_in[layer])
    y_h = _hgrn(p, hgrn_lower_bounds_fwd, hgrn_lower_bounds_bwd, hgrn_out_norm[layer:layer + 1],
                batch, seq)
    y_a = _attention(p, attn_sink[layer:layer + 1], rel_bias_table, batch, seq)
    xf = _outproj(y_h, y_a, xf, w_mix_out[layer].astype(BF16), post_norm_mix[layer:layer + 1])
    xf = _ffn(xf, pre_norm_ffn2[layer:layer + 1], post_norm_ffn2[layer:layer + 1],
              w_ffn2_gate_up[layer], w_ffn2_down[layer])
    return xf.reshape(batch, seq, d)
```

```python
import functools
import math

import jax
import jax.numpy as jnp
import numpy as np
from jax import lax
from jax.experimental import pallas as pl
from jax.experimental.pallas import tpu as pltpu

F32 = jnp.float32
BF16 = jnp.bfloat16

D_MODEL = 2048
HGRN_WIDTH = 1024
HEAD_DIM = 128
HGRN_HEADS = HGRN_WIDTH // HEAD_DIM
CHUNK = 64
ATTN_WIDTH = 1024
ATTN_HEADS = ATTN_WIDTH // HEAD_DIM
KV_HEADS = 2
GROUP = ATTN_HEADS // KV_HEADS
KV_WIDTH = KV_HEADS * HEAD_DIM
WINDOW = 128
ATTN_BLOCK = 128
KEY_SPAN = ATTN_BLOCK + 2 * WINDOW
REL_BUCKETS = 32
REL_MAX_DIST = 128
D_FF = 5632
EPS = 1e-6
NEG_INF = -1e30
LOG2_E = 1.0 / math.log(2.0)

MIB = 1024 * 1024
VMEM_LIMIT_BYTES = 56 * MIB
ROW_BLOCK = 32


def _rms(x, gain):
    return x * lax.rsqrt(jnp.mean(x * x, axis=-1, keepdims=True) + EPS) * gain


def _sigmoid(x):
    return 1.0 / (1.0 + jnp.exp(-x))


def _for_row_blocks(n_rows, fn):
    def body(r, carry):
        fn(pl.ds(pl.multiple_of(r * ROW_BLOCK, ROW_BLOCK), ROW_BLOCK))
        return carry

    lax.fori_loop(0, n_rows // ROW_BLOCK, body, 0, unroll=4)


def _ffn_kernel(*refs, emit_next_norm):
    if emit_next_norm:
        x_ref, gpre_ref, gpost_ref, gnext_ref, wg_ref, wu_ref, wd_ref, o_ref, h_ref = refs
    else:
        x_ref, gpre_ref, gpost_ref, wg_ref, wu_ref, wd_ref, o_ref, h_ref = refs
    j = pl.program_id(1)
    tm = x_ref.shape[0]

    @pl.when(j == 0)
    def _():
        def prologue(rows):
            h_ref[rows, :] = _rms(x_ref[rows, :], gpre_ref[...]).astype(BF16)
            o_ref[rows, :] = jnp.zeros((ROW_BLOCK, o_ref.shape[1]), F32)

        _for_row_blocks(tm, prologue)

    h = h_ref[...]
    gate = jnp.dot(h, wg_ref[...].astype(BF16), preferred_element_type=F32)
    up = jnp.dot(h, wu_ref[...].astype(BF16), preferred_element_type=F32)
    act = (gate * _sigmoid(gate) * up).astype(BF16)
    o_ref[...] += jnp.dot(act, wd_ref[...].astype(BF16), preferred_element_type=F32)

    @pl.when(j == pl.num_programs(1) - 1)
    def _():
        for r in range(0, tm, ROW_BLOCK):
            rows = pl.ds(r, ROW_BLOCK)
            new_x = x_ref[rows, :] + 0.5 * _rms(o_ref[rows, :], gpost_ref[...])
            o_ref[rows, :] = new_x
            if emit_next_norm:
                h_ref[rows, :] = _rms(new_x, gnext_ref[...]).astype(BF16)


def _ffn(x, gpre, gpost, w_gate_up, w_down, gnext=None, *, tm=1024, tf=256):
    m, d = x.shape
    nj = D_FF // tf
    emit = gnext is not None
    row_tile = pl.BlockSpec((tm, d), lambda i, j: (i, 0))
    gain_spec = pl.BlockSpec((1, d), lambda i, j: (0, 0))
    weight_specs = [
        pl.BlockSpec((d, tf), lambda i, j: (0, j)),
        pl.BlockSpec((d, tf), lambda i, j: (0, j + nj)),
        pl.BlockSpec((tf, d), lambda i, j: (j, 0)),
    ]
    gains = (gpre, gpost, gnext) if emit else (gpre, gpost)
    return pl.pallas_call(
        functools.partial(_ffn_kernel, emit_next_norm=emit),
        grid=(m // tm, nj),
        in_specs=[row_tile] + [gain_spec] * len(gains) + weight_specs,
        out_specs=(row_tile, row_tile) if emit else row_tile,
        out_shape=((jax.ShapeDtypeStruct((m, d), F32), jax.ShapeDtypeStruct((m, d), BF16)) if emit
                   else jax.ShapeDtypeStruct((m, d), F32)),
        scratch_shapes=[] if emit else [pltpu.VMEM((tm, d), BF16)],
        compiler_params=pltpu.CompilerParams(
            dimension_semantics=("parallel", "arbitrary"), vmem_limit_bytes=VMEM_LIMIT_BYTES),
        name="ffn",
    )(x, *gains, w_gate_up, w_gate_up, w_down)


def _proj_kernel(xn_ref, w_ref, o_ref):
    o_ref[...] = jnp.dot(xn_ref[...], w_ref[...].astype(BF16),
                         preferred_element_type=F32).astype(o_ref.dtype)


def _proj(xn, w_in, col0, n, out_dtype, *, tm=2048, tn=512):
    m, d = xn.shape
    blk0 = col0 // tn
    return pl.pallas_call(
        _proj_kernel,
        grid=(m // tm, n // tn),
        in_specs=[
            pl.BlockSpec((tm, d), lambda i, j: (i, 0)),
            pl.BlockSpec((d, tn), lambda i, j: (0, blk0 + j)),
        ],
        out_specs=pl.BlockSpec((tm, tn), lambda i, j: (i, j)),
        out_shape=jax.ShapeDtypeStruct((m, n), out_dtype),
        compiler_params=pltpu.CompilerParams(
            dimension_semantics=("parallel", "arbitrary"), vmem_limit_bytes=VMEM_LIMIT_BYTES),
        name="inproj",
    )(xn, w_in)


def _chunk_cumsum(x, row_in_chunk, reverse):
    n_rows = x.shape[0]
    c = x
    s = 1
    while s < CHUNK:
        if reverse:
            shifted = pltpu.roll(c, n_rows - s, axis=0)
            keep = row_in_chunk < CHUNK - s
        else:
            shifted = pltpu.roll(c, s, axis=0)
            keep = row_in_chunk >= s
        c = c + jnp.where(keep, shifted, 0.0)
        s *= 2
    return c


def _hgrn_kernel(q_ref, v_ref, ff_ref, fb_ref, g_ref, lbf_ref, lbb_ref, gain_ref, o_ref,
                 qd_ref, kv_ref, dec_ref, st_ref, oi_ref):
    seq = q_ref.shape[0]
    n_chunks = seq // CHUNK
    pair = 2 * CHUNK
    contract_last = (((1,), (1,)), ((), ()))

    def gate_consts(lb_param_ref):
        a = lb_param_ref[...]
        e = jnp.exp(a - jnp.max(a, axis=0, keepdims=True))
        lb = e[0:1] / jnp.sum(e, axis=0, keepdims=True)
        return 0.5 * (1.0 + lb), 0.5 * (1.0 - lb)

    consts = (gate_consts(lbf_ref), gate_consts(lbb_ref))
    logit_refs = (ff_ref, fb_ref)

    row_in_chunk = lax.broadcasted_iota(jnp.int32, (pair, HEAD_DIM), 0) % CHUNK
    r = lax.broadcasted_iota(jnp.int32, (pair, pair), 0)
    s = lax.broadcasted_iota(jnp.int32, (pair, pair), 1)
    same_chunk = (r // CHUNK) == (s // CHUNK)
    tri = (same_chunk & (r >= s), same_chunk & (s >= r))
    zero_blk = jnp.zeros((CHUNK, HEAD_DIM), BF16)

    def pair_body(pi, carry):
        rows = pl.ds(pl.multiple_of(pi * pair, pair), pair)
        q = q_ref[rows, :].astype(F32)
        vb = v_ref[rows, :]
        v_t = vb.astype(F32).T.astype(BF16)
        probs = None
        for d in range(2):
            mid, half = consts[d]
            t = jnp.tanh(0.5 * logit_refs[d][rows, :])
            f = mid + half * t
            k = half * (1.0 - t)
            c = _chunk_cumsum(jnp.log(f) * LOG2_E, row_in_chunk, reverse=(d == 1))
            c3 = c.reshape(2, CHUNK, HEAD_DIM)
            tot = c3[:, 0:1, :] if d == 1 else c3[:, CHUNK - 1:CHUNK, :]
            k_tail = (k * jnp.exp2(tot - c3).reshape(pair, HEAD_DIM)).astype(BF16)
            q_dec = (q * jnp.exp2(c)).astype(BF16)
            k_dec = (k * jnp.exp2(-c)).astype(BF16)
            qd_ref[rows, d * HEAD_DIM:(d + 1) * HEAD_DIM] = q_dec
            dec_ref[d, pl.ds(pi * 2, 2), :] = jnp.exp2(tot).reshape(2, HEAD_DIM)
            sc = lax.dot_general(q_dec, k_dec, contract_last, preferred_element_type=F32)
            sc = jnp.where(tri[d], sc, 0.0)
            probs = sc if probs is None else probs + sc
            rhs = jnp.concatenate(
                [jnp.concatenate([k_tail[:CHUNK], zero_blk], axis=1),
                 jnp.concatenate([zero_blk, k_tail[CHUNK:]], axis=1)], axis=0)
            kv = jnp.dot(v_t, rhs, preferred_element_type=F32)
            kv_ref[d, pi * 2] = kv[:, :HEAD_DIM]
            kv_ref[d, pi * 2 + 1] = kv[:, HEAD_DIM:]
        oi_ref[rows, :] = jnp.dot(probs.astype(BF16), vb, preferred_element_type=F32)
        return carry

    lax.fori_loop(0, n_chunks // 2, pair_body, 0, unroll=8)

    def scan_body(t, carry):
        sf, sb = carry
        nf, nb = t, n_chunks - 1 - t
        st_ref[nf, :, :HEAD_DIM] = sf.astype(BF16)
        st_ref[nb, :, HEAD_DIM:] = sb.astype(BF16)
        sf = sf * dec_ref[0, pl.ds(nf, 1), :] + kv_ref[0, nf]
        sb = sb * dec_ref[1, pl.ds(nb, 1), :] + kv_ref[1, nb]
        return sf, sb

    zero = jnp.zeros((HEAD_DIM, HEAD_DIM), F32)
    lax.fori_loop(0, n_chunks, scan_body, (zero, zero), unroll=2)

    def out_body(n, carry):
        rows = pl.ds(pl.multiple_of(n * CHUNK, CHUNK), CHUNK)
        o = oi_ref[rows, :] + lax.dot_general(qd_ref[rows, :], st_ref[n], contract_last,
                                              preferred_element_type=F32)
        o = _rms(o, gain_ref[...])
        g = g_ref[rows, :].astype(F32)
        o_ref[rows, :] = (o * (0.5 * g * (1.0 + jnp.tanh(0.5 * g)))).astype(o_ref.dtype)
        return carry

    lax.fori_loop(0, n_chunks, out_body, 0, unroll=16)


def _hgrn(p_qv, p_f, p_rest, lbf, lbb, out_gain, batch, seq):
    hw = HGRN_HEADS

    def col(group):
        return lambda b, h: (b, group * hw + h)

    blk = (seq, HEAD_DIM)
    return pl.pallas_call(
        _hgrn_kernel,
        grid=(batch, hw),
        in_specs=[
            pl.BlockSpec(blk, col(0)),
            pl.BlockSpec(blk, col(1)),
            pl.BlockSpec(blk, col(0)),
            pl.BlockSpec(blk, col(1)),
            pl.BlockSpec(blk, col(0)),
            pl.BlockSpec((lbf.shape[0], HEAD_DIM), lambda b, h: (0, h)),
            pl.BlockSpec((lbb.shape[0], HEAD_DIM), lambda b, h: (0, h)),
            pl.BlockSpec((1, HEAD_DIM), lambda b, h: (0, h)),
        ],
        out_specs=pl.BlockSpec(blk, lambda b, h: (b, h)),
        out_shape=jax.ShapeDtypeStruct((batch * seq, HGRN_WIDTH), BF16),
        scratch_shapes=[
            pltpu.VMEM((seq, 2 * HEAD_DIM), BF16),
            pltpu.VMEM((2, seq // CHUNK, HEAD_DIM, HEAD_DIM), F32),
            pltpu.VMEM((2, seq // CHUNK, HEAD_DIM), F32),
            pltpu.VMEM((seq // CHUNK, HEAD_DIM, 2 * HEAD_DIM), BF16),
            pltpu.VMEM((seq, HEAD_DIM), F32),
        ],
        compiler_params=pltpu.CompilerParams(
            dimension_semantics=("parallel", "parallel"), vmem_limit_bytes=VMEM_LIMIT_BYTES),
        name="hgrn",
    )(p_qv, p_qv, p_f, p_f, p_rest, lbf, lbb, out_gain)


def _t5_bucket_table():
    nb = REL_BUCKETS // 2
    max_exact = nb // 2
    c = np.arange(ATTN_BLOCK)[:, None]
    s = np.arange(KEY_SPAN)[None, :]
    rel = s - WINDOW - c
    bucket = (rel > 0).astype(np.int32) * nb
    n = np.abs(rel)
    large = max_exact + (np.log(np.maximum(n, 1) / max_exact) / np.log(REL_MAX_DIST / max_exact)
                         * (nb - max_exact)).astype(np.int32)
    large = np.minimum(large, nb - 1)
    bucket = bucket + np.where(n < max_exact, n, large).astype(np.int32)
    return np.where(np.abs(rel) <= WINDOW, bucket, -1).astype(np.int32)


def _attn_kernel(table_ref, sink_ref, bucket_ref, q_ref, k_ref, v_ref, o_ref, bias_ref):
    x = pl.program_id(1)
    seq = q_ref.shape[0]
    n_blocks = seq // ATTN_BLOCK
    scale = 1.0 / math.sqrt(HEAD_DIM)
    rows = GROUP * ATTN_BLOCK

    @pl.when((pl.program_id(0) == 0) & (x == 0))
    def _():
        bucket = bucket_ref[...]
        for head in range(ATTN_HEADS):
            bias = jnp.full((ATTN_BLOCK, KEY_SPAN), NEG_INF, F32)
            for b in range(REL_BUCKETS):
                bias = jnp.where(bucket == b, table_ref[b, head], bias)
            bias_ref[pl.ds(head * ATTN_BLOCK, ATTN_BLOCK), :] = bias

    row_head = lax.broadcasted_iota(jnp.int32, (rows, 1), 0) // ATTN_BLOCK
    sink = jnp.zeros((rows, 1), F32)
    for g in range(GROUP):
        sink = jnp.where(row_head == g, sink_ref[0, x * GROUP + g], sink)
    bias_rows = pl.ds(pl.multiple_of(x * rows, rows), rows)

    def block(i, edge):
        if edge:
            prev, nxt = max(i - 1, 0), min(i + 1, n_blocks - 1)
        else:
            prev, nxt = i - 1, i + 1

        def blk(ref, idx):
            start = idx * ATTN_BLOCK
            if not isinstance(idx, int):
                start = pl.multiple_of(start, ATTN_BLOCK)
            return ref[pl.ds(start, ATTN_BLOCK), :]

        qb = blk(q_ref, i)
        qs = jnp.concatenate([qb[:, g * HEAD_DIM:(g + 1) * HEAD_DIM] for g in range(GROUP)], axis=0)
        kcat = jnp.concatenate([blk(k_ref, prev), blk(k_ref, i), blk(k_ref, nxt)], axis=0)
        vcat = jnp.concatenate([blk(v_ref, prev), blk(v_ref, i), blk(v_ref, nxt)], axis=0)
        sc = lax.dot_general(qs, kcat, (((1,), (1,)), ((), ())), preferred_element_type=F32)
        sc = sc * scale + bias_ref[bias_rows, :]
        if edge:
            key_pos = (i - 1) * ATTN_BLOCK + lax.broadcasted_iota(jnp.int32, (rows, KEY_SPAN), 1)
            sc = jnp.where((key_pos >= 0) & (key_pos < seq), sc, NEG_INF)
        m = jnp.maximum(jnp.max(sc, axis=-1, keepdims=True), sink)
        pr = jnp.exp(sc - m)
        den = jnp.sum(pr, axis=-1, keepdims=True) + jnp.exp(sink - m)
        o = jnp.dot(pr.astype(BF16), vcat, preferred_element_type=F32) / den
        ob = jnp.concatenate([o[g * ATTN_BLOCK:(g + 1) * ATTN_BLOCK] for g in range(GROUP)], axis=1)
        out_start = i * ATTN_BLOCK if edge else pl.multiple_of(i * ATTN_BLOCK, ATTN_BLOCK)
        o_ref[pl.ds(out_start, ATTN_BLOCK), :] = ob.astype(o_ref.dtype)

    def interior(i, carry):
        block(i, edge=False)
        return carry

    block(0, edge=True)
    lax.fori_loop(1, n_blocks - 1, interior, 0, unroll=7)
    block(n_blocks - 1, edge=True)


def _attention(p_rest, sink, rel_table, batch, seq):
    q_col0 = HGRN_WIDTH // (GROUP * HEAD_DIM)
    k_col0 = (HGRN_WIDTH + ATTN_WIDTH) // HEAD_DIM
    v_col0 = k_col0 + KV_HEADS
    bucket = jnp.asarray(_t5_bucket_table())
    smem = pl.BlockSpec(memory_space=pltpu.SMEM)
    return pl.pallas_call(
        _attn_kernel,
        grid=(batch, KV_HEADS),
        in_specs=[
            smem,
            smem,
            pl.BlockSpec((ATTN_BLOCK, KEY_SPAN), lambda b, x: (0, 0)),
            pl.BlockSpec((seq, GROUP * HEAD_DIM), lambda b, x: (b, q_col0 + x)),
            pl.BlockSpec((seq, HEAD_DIM), lambda b, x: (b, k_col0 + x)),
            pl.BlockSpec((seq, HEAD_DIM), lambda b, x: (b, v_col0 + x)),
        ],
        out_specs=pl.BlockSpec((seq, GROUP * HEAD_DIM), lambda b, x: (b, x)),
        out_shape=jax.ShapeDtypeStruct((batch * seq, ATTN_WIDTH), BF16),
        scratch_shapes=[
            pltpu.VMEM((ATTN_HEADS * ATTN_BLOCK, KEY_SPAN), F32),
        ],
        compiler_params=pltpu.CompilerParams(
            dimension_semantics=("arbitrary", "arbitrary"), vmem_limit_bytes=VMEM_LIMIT_BYTES),
        name="attention",
    )(rel_table, sink, bucket, p_rest, p_rest, p_rest)


def _outproj_kernel(yh_ref, ya_ref, x_ref, wh_ref, wa_ref, g_ref, o_ref):
    mixed = jnp.dot(yh_ref[...], wh_ref[...], preferred_element_type=F32)
    mixed = mixed + jnp.dot(ya_ref[...], wa_ref[...], preferred_element_type=F32)
    o_ref[...] = x_ref[...] + _rms(mixed, g_ref[...])


def _outproj(y_h, y_a, x, w_out, gain, *, tm=512):
    m, d = x.shape
    return pl.pallas_call(
        _outproj_kernel,
        grid=(m // tm,),
        in_specs=[
            pl.BlockSpec((tm, HGRN_WIDTH), lambda i: (i, 0)),
            pl.BlockSpec((tm, ATTN_WIDTH), lambda i: (i, 0)),
            pl.BlockSpec((tm, d), lambda i: (i, 0)),
            pl.BlockSpec((HGRN_WIDTH, d), lambda i: (0, 0)),
            pl.BlockSpec((ATTN_WIDTH, d), lambda i: (1, 0)),
            pl.BlockSpec((1, d), lambda i: (0, 0)),
        ],
        out_specs=pl.BlockSpec((tm, d), lambda i: (i, 0)),
        out_shape=jax.ShapeDtypeStruct((m, d), F32),
        compiler_params=pltpu.CompilerParams(
            dimension_semantics=("parallel",), vmem_limit_bytes=VMEM_LIMIT_BYTES),
        name="outproj",
    )(y_h, y_a, x, w_out, w_out, gain)


def kernel(x, pre_norm_ffn1, post_norm_ffn1, w_ffn1_gate_up, w_ffn1_down, pre_norm_mix, post_norm_mix,
           w_mix_in, hgrn_lower_bounds_fwd, hgrn_lower_bounds_bwd, hgrn_out_norm, attn_sink, w_mix_out,
           pre_norm_ffn2, post_norm_ffn2, w_ffn2_gate_up, w_ffn2_down, rel_bias_table):
    batch, seq, d = x.shape
    depth = pre_norm_ffn1.shape[0]
    assert depth == 1 and d == D_MODEL
    xf = x.reshape(batch * seq, d)
    layer = 0
    xf, xn = _ffn(xf, pre_norm_ffn1[layer:layer + 1], post_norm_ffn1[layer:layer + 1],
                  w_ffn1_gate_up[layer], w_ffn1_down[layer], gnext=pre_norm_mix[layer:layer + 1])
    w_in = w_mix_in[layer]
    p_qv = _proj(xn, w_in, 0, 2 * HGRN_WIDTH, BF16)
    p_f = _proj(xn, w_in, 2 * HGRN_WIDTH, 2 * HGRN_WIDTH, F32)
    p_rest = _proj(xn, w_in, 4 * HGRN_WIDTH, HGRN_WIDTH + ATTN_WIDTH + 2 * KV_WIDTH, BF16)
    y_h = _hgrn(p_qv, p_f, p_rest, hgrn_lower_bounds_fwd, hgrn_lower_bounds_bwd,
                hgrn_out_norm[layer:layer + 1], batch, seq)
    y_a = _attention(p_rest, attn_sink[layer:layer + 1], rel_bias_table, batch, seq)
    xf = _outproj(y_h, y_a, xf, w_mix_out[layer].astype(BF16), post_norm_mix[layer:layer + 1])
    xf = _ffn(xf, pre_norm_ffn2[layer:layer + 1], post_norm_ffn2[layer:layer + 1],
              w_ffn2_gate_up[layer], w_ffn2_down[layer])
    return xf.reshape(batch, seq, d)
```

```python
import functools
import math

import jax
import jax.numpy as jnp
import numpy as np
from jax import lax
from jax.experimental import pallas as pl
from jax.experimental.pallas import tpu as pltpu

F32 = jnp.float32
BF16 = jnp.bfloat16

D_MODEL = 2048
HGRN_WIDTH = 1024
HEAD_DIM = 128
HGRN_HEADS = HGRN_WIDTH // HEAD_DIM
CHUNK = 64
ATTN_WIDTH = 1024
ATTN_HEADS = ATTN_WIDTH // HEAD_DIM
KV_HEADS = 2
GROUP = ATTN_HEADS // KV_HEADS
KV_WIDTH = KV_HEADS * HEAD_DIM
WINDOW = 128
ATTN_BLOCK = 128
KEY_SPAN = ATTN_BLOCK + 2 * WINDOW
REL_BUCKETS = 32
REL_MAX_DIST = 128
D_FF = 5632
EPS = 1e-6
NEG_INF = -1e30
LOG2_E = 1.0 / math.log(2.0)

MIB = 1024 * 1024
VMEM_LIMIT_BYTES = 56 * MIB
ROW_BLOCK = 32


def _rms(x, gain):
    return x * lax.rsqrt(jnp.mean(x * x, axis=-1, keepdims=True) + EPS) * gain


def _sigmoid(x):
    return 1.0 / (1.0 + jnp.exp(-x))


def _for_row_blocks(n_rows, fn):
    def body(r, carry):
        fn(pl.ds(pl.multiple_of(r * ROW_BLOCK, ROW_BLOCK), ROW_BLOCK))
        return carry

    lax.fori_loop(0, n_rows // ROW_BLOCK, body, 0, unroll=4)


def _ffn_kernel(*refs, emit_next_norm):
    if emit_next_norm:
        x_ref, gpre_ref, gpost_ref, gnext_ref, wg_ref, wu_ref, wd_ref, o_ref, h_ref = refs
    else:
        x_ref, gpre_ref, gpost_ref, wg_ref, wu_ref, wd_ref, o_ref, h_ref = refs
    j = pl.program_id(1)
    tm = x_ref.shape[0]

    @pl.when(j == 0)
    def _():
        def prologue(rows):
            h_ref[rows, :] = _rms(x_ref[rows, :], gpre_ref[...]).astype(BF16)
            o_ref[rows, :] = jnp.zeros((ROW_BLOCK, o_ref.shape[1]), F32)

        _for_row_blocks(tm, prologue)

    h = h_ref[...]
    gate = jnp.dot(h, wg_ref[...].astype(BF16), preferred_element_type=F32)
    up = jnp.dot(h, wu_ref[...].astype(BF16), preferred_element_type=F32)
    act = (gate * _sigmoid(gate) * up).astype(BF16)
    o_ref[...] += jnp.dot(act, wd_ref[...].astype(BF16), preferred_element_type=F32)

    @pl.when(j == pl.num_programs(1) - 1)
    def _():
        for r in range(0, tm, ROW_BLOCK):
            rows = pl.ds(r, ROW_BLOCK)
            new_x = x_ref[rows, :] + 0.5 * _rms(o_ref[rows, :], gpost_ref[...])
            o_ref[rows, :] = new_x
            if emit_next_norm:
                h_ref[rows, :] = _rms(new_x, gnext_ref[...]).astype(BF16)


def _ffn(x, gpre, gpost, w_gate_up, w_down, gnext=None, *, tm=1024, tf=256):
    m, d = x.shape
    nj = D_FF // tf
    emit = gnext is not None
    row_tile = pl.BlockSpec((tm, d), lambda i, j: (i, 0))
    gain_spec = pl.BlockSpec((1, d), lambda i, j: (0, 0))
    weight_specs = [
        pl.BlockSpec((d, tf), lambda i, j: (0, j)),
        pl.BlockSpec((d, tf), lambda i, j: (0, j + nj)),
        pl.BlockSpec((tf, d), lambda i, j: (j, 0)),
    ]
    gains = (gpre, gpost, gnext) if emit else (gpre, gpost)
    return pl.pallas_call(
        functools.partial(_ffn_kernel, emit_next_norm=emit),
        grid=(m // tm, nj),
        in_specs=[row_tile] + [gain_spec] * len(gains) + weight_specs,
        out_specs=(row_tile, row_tile) if emit else row_tile,
        out_shape=((jax.ShapeDtypeStruct((m, d), F32), jax.ShapeDtypeStruct((m, d), BF16)) if emit
                   else jax.ShapeDtypeStruct((m, d), F32)),
        scratch_shapes=[] if emit else [pltpu.VMEM((tm, d), BF16)],
        compiler_params=pltpu.CompilerParams(
            dimension_semantics=("parallel", "arbitrary"), vmem_limit_bytes=VMEM_LIMIT_BYTES),
        name="ffn",
    )(x, *gains, w_gate_up, w_gate_up, w_down)


def _proj_kernel(xn_ref, w_ref, o_ref):
    res = jnp.dot(xn_ref[...], w_ref[...].astype(BF16), preferred_element_type=F32).astype(o_ref.dtype)
    for c in range(o_ref.shape[0]):
        o_ref[c] = res[:, c * HEAD_DIM:(c + 1) * HEAD_DIM]


def _proj(xn, w_in, col_blocks, out_dtype, *, tm=2048, tn=512):
    m, d = xn.shape
    per = tn // HEAD_DIM
    jumps = [(pos, col_blocks[pos] - col_blocks[pos - 1] - 1) for pos in range(1, len(col_blocks))
             if col_blocks[pos] != col_blocks[pos - 1] + 1]

    def w_block(i, j):
        blk = j + col_blocks[0]
        for pos, gap in jumps:
            blk = blk + jnp.where(j >= pos, gap, 0)
        return (0, blk)

    return pl.pallas_call(
        _proj_kernel,
        grid=(m // tm, len(col_blocks)),
        in_specs=[
            pl.BlockSpec((tm, d), lambda i, j: (i, 0)),
            pl.BlockSpec((d, tn), w_block),
        ],
        out_specs=pl.BlockSpec((per, tm, HEAD_DIM), lambda i, j: (j, i, 0)),
        out_shape=jax.ShapeDtypeStruct((len(col_blocks) * per, m, HEAD_DIM), out_dtype),
        compiler_params=pltpu.CompilerParams(
            dimension_semantics=("parallel", "arbitrary"), vmem_limit_bytes=VMEM_LIMIT_BYTES),
        name="inproj",
    )(xn, w_in)


def _chunk_cumsum(x, row_in_chunk, reverse):
    n_rows = x.shape[0]
    c = x
    s = 1
    while s < CHUNK:
        if reverse:
            shifted = pltpu.roll(c, n_rows - s, axis=0)
            keep = row_in_chunk < CHUNK - s
        else:
            shifted = pltpu.roll(c, s, axis=0)
            keep = row_in_chunk >= s
        c = c + jnp.where(keep, shifted, 0.0)
        s *= 2
    return c


def _hgrn_kernel(q_ref, v_ref, ff_ref, fb_ref, g_ref, lbf_ref, lbb_ref, gain_ref, o_ref,
                 qd_ref, kv_ref, dec_ref, st_ref, oi_ref):
    seq = q_ref.shape[0]
    n_chunks = seq // CHUNK
    pair = 2 * CHUNK
    contract_last = (((1,), (1,)), ((), ()))

    def gate_consts(lb_param_ref):
        a = lb_param_ref[...]
        e = jnp.exp(a - jnp.max(a, axis=0, keepdims=True))
        lb = e[0:1] / jnp.sum(e, axis=0, keepdims=True)
        return 0.5 * (1.0 + lb), 0.5 * (1.0 - lb)

    consts = (gate_consts(lbf_ref), gate_consts(lbb_ref))
    logit_refs = (ff_ref, fb_ref)

    row_in_chunk = lax.broadcasted_iota(jnp.int32, (pair, HEAD_DIM), 0) % CHUNK
    r = lax.broadcasted_iota(jnp.int32, (pair, pair), 0)
    s = lax.broadcasted_iota(jnp.int32, (pair, pair), 1)
    same_chunk = (r // CHUNK) == (s // CHUNK)
    tri = (same_chunk & (r >= s), same_chunk & (s >= r))
    zero_blk = jnp.zeros((CHUNK, HEAD_DIM), BF16)

    def pair_body(pi, carry):
        rows = pl.ds(pl.multiple_of(pi * pair, pair), pair)
        q = q_ref[rows, :].astype(F32)
        vb = v_ref[rows, :]
        v_t = vb.astype(F32).T.astype(BF16)
        probs = None
        for d in range(2):
            mid, half = consts[d]
            t = jnp.tanh(0.5 * logit_refs[d][rows, :])
            f = mid + half * t
            k = half * (1.0 - t)
            c = _chunk_cumsum(jnp.log(f) * LOG2_E, row_in_chunk, reverse=(d == 1))
            c3 = c.reshape(2, CHUNK, HEAD_DIM)
            tot = c3[:, 0:1, :] if d == 1 else c3[:, CHUNK - 1:CHUNK, :]
            k_tail = (k * jnp.exp2(tot - c3).reshape(pair, HEAD_DIM)).astype(BF16)
            q_dec = (q * jnp.exp2(c)).astype(BF16)
            k_dec = (k * jnp.exp2(-c)).astype(BF16)
            qd_ref[rows, d * HEAD_DIM:(d + 1) * HEAD_DIM] = q_dec
            dec_ref[d, pl.ds(pi * 2, 2), :] = jnp.exp2(tot).reshape(2, HEAD_DIM)
            sc = lax.dot_general(q_dec, k_dec, contract_last, preferred_element_type=F32)
            sc = jnp.where(tri[d], sc, 0.0)
            probs = sc if probs is None else probs + sc
            rhs = jnp.concatenate(
                [jnp.concatenate([k_tail[:CHUNK], zero_blk], axis=1),
                 jnp.concatenate([zero_blk, k_tail[CHUNK:]], axis=1)], axis=0)
            kv = jnp.dot(v_t, rhs, preferred_element_type=F32)
            kv_ref[d, pi * 2] = kv[:, :HEAD_DIM]
            kv_ref[d, pi * 2 + 1] = kv[:, HEAD_DIM:]
        oi_ref[rows, :] = jnp.dot(probs.astype(BF16), vb, preferred_element_type=F32)
        return carry

    lax.fori_loop(0, n_chunks // 2, pair_body, 0, unroll=8)

    def scan_body(t, carry):
        sf, sb = carry
        nf, nb = t, n_chunks - 1 - t
        st_ref[nf, :, :HEAD_DIM] = sf.astype(BF16)
        st_ref[nb, :, HEAD_DIM:] = sb.astype(BF16)
        sf = sf * dec_ref[0, pl.ds(nf, 1), :] + kv_ref[0, nf]
        sb = sb * dec_ref[1, pl.ds(nb, 1), :] + kv_ref[1, nb]
        return sf, sb

    zero = jnp.zeros((HEAD_DIM, HEAD_DIM), F32)
    lax.fori_loop(0, n_chunks, scan_body, (zero, zero), unroll=2)

    def out_body(n, carry):
        rows = pl.ds(pl.multiple_of(n * CHUNK, CHUNK), CHUNK)
        o = oi_ref[rows, :] + lax.dot_general(qd_ref[rows, :], st_ref[n], contract_last,
                                              preferred_element_type=F32)
        o = _rms(o, gain_ref[...])
        g = g_ref[rows, :].astype(F32)
        o_ref[rows, :] = (o * (0.5 * g * (1.0 + jnp.tanh(0.5 * g)))).astype(o_ref.dtype)
        return carry

    lax.fori_loop(0, n_chunks, out_body, 0, unroll=16)


def _hgrn(p16, p_f, lbf, lbb, out_gain, batch, seq):
    hw = HGRN_HEADS

    def slab(group):
        return lambda b, h: (group * hw + h, b, 0)

    blk = (None, seq, HEAD_DIM)
    return pl.pallas_call(
        _hgrn_kernel,
        grid=(batch, hw),
        in_specs=[
            pl.BlockSpec(blk, slab(0)),
            pl.BlockSpec(blk, slab(1)),
            pl.BlockSpec(blk, slab(0)),
            pl.BlockSpec(blk, slab(1)),
            pl.BlockSpec(blk, slab(2)),
            pl.BlockSpec((lbf.shape[0], HEAD_DIM), lambda b, h: (0, h)),
            pl.BlockSpec((lbb.shape[0], HEAD_DIM), lambda b, h: (0, h)),
            pl.BlockSpec((1, HEAD_DIM), lambda b, h: (0, h)),
        ],
        out_specs=pl.BlockSpec(blk, lambda b, h: (h, b, 0)),
        out_shape=jax.ShapeDtypeStruct((hw, batch * seq, HEAD_DIM), BF16),
        scratch_shapes=[
            pltpu.VMEM((seq, 2 * HEAD_DIM), BF16),
            pltpu.VMEM((2, seq // CHUNK, HEAD_DIM, HEAD_DIM), F32),
            pltpu.VMEM((2, seq // CHUNK, HEAD_DIM), F32),
            pltpu.VMEM((seq // CHUNK, HEAD_DIM, 2 * HEAD_DIM), BF16),
            pltpu.VMEM((seq, HEAD_DIM), F32),
        ],
        compiler_params=pltpu.CompilerParams(
            dimension_semantics=("parallel", "parallel"), vmem_limit_bytes=VMEM_LIMIT_BYTES),
        name="hgrn",
    )(p16, p16, p_f, p_f, p16, lbf, lbb, out_gain)


def _t5_bucket_table():
    nb = REL_BUCKETS // 2
    max_exact = nb // 2
    c = np.arange(ATTN_BLOCK)[:, None]
    s = np.arange(KEY_SPAN)[None, :]
    rel = s - WINDOW - c
    bucket = (rel > 0).astype(np.int32) * nb
    n = np.abs(rel)
    large = max_exact + (np.log(np.maximum(n, 1) / max_exact) / np.log(REL_MAX_DIST / max_exact)
                         * (nb - max_exact)).astype(np.int32)
    large = np.minimum(large, nb - 1)
    bucket = bucket + np.where(n < max_exact, n, large).astype(np.int32)
    return np.where(np.abs(rel) <= WINDOW, bucket, -1).astype(np.int32)


def _attn_kernel(table_ref, sink_ref, bucket_ref, q_ref, k_ref, v_ref, o_ref, bias_ref):
    x = pl.program_id(1)
    seq = k_ref.shape[0]
    n_blocks = seq // ATTN_BLOCK
    scale = 1.0 / math.sqrt(HEAD_DIM)
    rows = GROUP * ATTN_BLOCK

    @pl.when((pl.program_id(0) == 0) & (x == 0))
    def _():
        bucket = bucket_ref[...]
        for head in range(ATTN_HEADS):
            bias = jnp.full((ATTN_BLOCK, KEY_SPAN), NEG_INF, F32)
            for b in range(REL_BUCKETS):
                bias = jnp.where(bucket == b, table_ref[b, head], bias)
            bias_ref[pl.ds(head * ATTN_BLOCK, ATTN_BLOCK), :] = bias

    row_head = lax.broadcasted_iota(jnp.int32, (rows, 1), 0) // ATTN_BLOCK
    sink = jnp.zeros((rows, 1), F32)
    for g in range(GROUP):
        sink = jnp.where(row_head == g, sink_ref[0, x * GROUP + g], sink)
    bias_rows = pl.ds(pl.multiple_of(x * rows, rows), rows)

    def block(i, edge):
        if edge:
            prev, nxt = max(i - 1, 0), min(i + 1, n_blocks - 1)
        else:
            prev, nxt = i - 1, i + 1

        def block_rows(idx):
            start = idx * ATTN_BLOCK
            if not isinstance(idx, int):
                start = pl.multiple_of(start, ATTN_BLOCK)
            return pl.ds(start, ATTN_BLOCK)

        def blk(ref, idx):
            return ref[block_rows(idx), :]

        qs = jnp.concatenate([q_ref[g, block_rows(i), :] for g in range(GROUP)], axis=0)
        kcat = jnp.concatenate([blk(k_ref, prev), blk(k_ref, i), blk(k_ref, nxt)], axis=0)
        vcat = jnp.concatenate([blk(v_ref, prev), blk(v_ref, i), blk(v_ref, nxt)], axis=0)
        sc = lax.dot_general(qs, kcat, (((1,), (1,)), ((), ())), preferred_element_type=F32)
        sc = sc * scale + bias_ref[bias_rows, :]
        if edge:
            key_pos = (i - 1) * ATTN_BLOCK + lax.broadcasted_iota(jnp.int32, (rows, KEY_SPAN), 1)
            sc = jnp.where((key_pos >= 0) & (key_pos < seq), sc, NEG_INF)
        m = jnp.maximum(jnp.max(sc, axis=-1, keepdims=True), sink)
        pr = jnp.exp(sc - m)
        den = jnp.sum(pr, axis=-1, keepdims=True) + jnp.exp(sink - m)
        o = jnp.dot(pr.astype(BF16), vcat, preferred_element_type=F32) / den
        for g in range(GROUP):
            o_ref[g, block_rows(i), :] = o[g * ATTN_BLOCK:(g + 1) * ATTN_BLOCK].astype(o_ref.dtype)

    def interior(i, carry):
        block(i, edge=False)
        return carry

    block(0, edge=True)
    lax.fori_loop(1, n_blocks - 1, interior, 0, unroll=7)
    block(n_blocks - 1, edge=True)


def _attention(p16, sink, rel_table, batch, seq):
    q_blk0 = 3 * HGRN_HEADS // GROUP
    k_slab0 = 3 * HGRN_HEADS + ATTN_HEADS
    v_slab0 = k_slab0 + KV_HEADS
    bucket = jnp.asarray(_t5_bucket_table())
    smem = pl.BlockSpec(memory_space=pltpu.SMEM)
    return pl.pallas_call(
        _attn_kernel,
        grid=(batch, KV_HEADS),
        in_specs=[
            smem,
            smem,
            pl.BlockSpec((ATTN_BLOCK, KEY_SPAN), lambda b, x: (0, 0)),
            pl.BlockSpec((GROUP, seq, HEAD_DIM), lambda b, x: (q_blk0 + x, b, 0)),
            pl.BlockSpec((None, seq, HEAD_DIM), lambda b, x: (k_slab0 + x, b, 0)),
            pl.BlockSpec((None, seq, HEAD_DIM), lambda b, x: (v_slab0 + x, b, 0)),
        ],
        out_specs=pl.BlockSpec((GROUP, seq, HEAD_DIM), lambda b, x: (x, b, 0)),
        out_shape=jax.ShapeDtypeStruct((ATTN_HEADS, batch * seq, HEAD_DIM), BF16),
        scratch_shapes=[
            pltpu.VMEM((ATTN_HEADS * ATTN_BLOCK, KEY_SPAN), F32),
        ],
        compiler_params=pltpu.CompilerParams(
            dimension_semantics=("arbitrary", "arbitrary"), vmem_limit_bytes=VMEM_LIMIT_BYTES),
        name="attention",
    )(rel_table, sink, bucket, p16, p16, p16)


def _outproj_kernel(yh_ref, ya_ref, x_ref, wh_ref, wa_ref, g_ref, o_ref):
    y_h = jnp.concatenate([yh_ref[c] for c in range(yh_ref.shape[0])], axis=1)
    y_a = jnp.concatenate([ya_ref[c] for c in range(ya_ref.shape[0])], axis=1)
    mixed = jnp.dot(y_h, wh_ref[...], preferred_element_type=F32)
    mixed = mixed + jnp.dot(y_a, wa_ref[...], preferred_element_type=F32)
    o_ref[...] = x_ref[...] + _rms(mixed, g_ref[...])


def _outproj(y_h, y_a, x, w_out, gain, *, tm=512):
    m, d = x.shape
    return pl.pallas_call(
        _outproj_kernel,
        grid=(m // tm,),
        in_specs=[
            pl.BlockSpec((HGRN_HEADS, tm, HEAD_DIM), lambda i: (0, i, 0)),
            pl.BlockSpec((ATTN_HEADS, tm, HEAD_DIM), lambda i: (0, i, 0)),
            pl.BlockSpec((tm, d), lambda i: (i, 0)),
            pl.BlockSpec((HGRN_WIDTH, d), lambda i: (0, 0)),
            pl.BlockSpec((ATTN_WIDTH, d), lambda i: (1, 0)),
            pl.BlockSpec((1, d), lambda i: (0, 0)),
        ],
        out_specs=pl.BlockSpec((tm, d), lambda i: (i, 0)),
        out_shape=jax.ShapeDtypeStruct((m, d), F32),
        compiler_params=pltpu.CompilerParams(
            dimension_semantics=("parallel",), vmem_limit_bytes=VMEM_LIMIT_BYTES),
        name="outproj",
    )(y_h, y_a, x, w_out, w_out, gain)


def kernel(x, pre_norm_ffn1, post_norm_ffn1, w_ffn1_gate_up, w_ffn1_down, pre_norm_mix, post_norm_mix,
           w_mix_in, hgrn_lower_bounds_fwd, hgrn_lower_bounds_bwd, hgrn_out_norm, attn_sink, w_mix_out,
           pre_norm_ffn2, post_norm_ffn2, w_ffn2_gate_up, w_ffn2_down, rel_bias_table):
    batch, seq, d = x.shape
    depth = pre_norm_ffn1.shape[0]
    assert depth == 1 and d == D_MODEL
    xf = x.reshape(batch * seq, d)
    layer = 0
    xf, xn = _ffn(xf, pre_norm_ffn1[layer:layer + 1], post_norm_ffn1[layer:layer + 1],
                  w_ffn1_gate_up[layer], w_ffn1_down[layer], gnext=pre_norm_mix[layer:layer + 1])
    w_in = w_mix_in[layer]
    p16 = _proj(xn, w_in, [0, 1, 2, 3, 8, 9, 10, 11, 12], BF16)
    p_f = _proj(xn, w_in, [4, 5, 6, 7], F32)
    y_h = _hgrn(p16, p_f, hgrn_lower_bounds_fwd, hgrn_lower_bounds_bwd,
                hgrn_out_norm[layer:layer + 1], batch, seq)
    y_a = _attention(p16, attn_sink[layer:layer + 1], rel_bias_table, batch, seq)
    xf = _outproj(y_h, y_a, xf, w_mix_out[layer].astype(BF16), post_norm_mix[layer:layer + 1])
    xf = _ffn(xf, pre_norm_ffn2[layer:layer + 1], post_norm_ffn2[layer:layer + 1],
              w_ffn2_gate_up[layer], w_ffn2_down[layer])
    return xf.reshape(batch, seq, d)
```

```python
import functools
import math

import jax
import jax.numpy as jnp
import numpy as np
from jax import lax
from jax.experimental import pallas as pl
from jax.experimental.pallas import tpu as pltpu

F32 = jnp.float32
BF16 = jnp.bfloat16

D_MODEL = 2048
HGRN_WIDTH = 1024
HEAD_DIM = 128
HGRN_HEADS = HGRN_WIDTH // HEAD_DIM
CHUNK = 64
ATTN_WIDTH = 1024
ATTN_HEADS = ATTN_WIDTH // HEAD_DIM
KV_HEADS = 2
GROUP = ATTN_HEADS // KV_HEADS
KV_WIDTH = KV_HEADS * HEAD_DIM
WINDOW = 128
ATTN_BLOCK = 128
KEY_SPAN = ATTN_BLOCK + 2 * WINDOW
REL_BUCKETS = 32
REL_MAX_DIST = 128
D_FF = 5632
EPS = 1e-6
NEG_INF = -1e30
LOG2_E = 1.0 / math.log(2.0)

MIB = 1024 * 1024
VMEM_LIMIT_BYTES = 56 * MIB
ROW_BLOCK = 32
PAIRS_PER_TRIP = 16
CHUNKS_PER_OUT_TRIP = 16
ATTN_BLOCKS_PER_GROUP = 2


def _rms(x, gain):
    return x * lax.rsqrt(jnp.mean(x * x, axis=-1, keepdims=True) + EPS) * gain


def _sigmoid(x):
    return 1.0 / (1.0 + jnp.exp(-x))


def _for_row_blocks(n_rows, fn):
    def body(r, carry):
        fn(pl.ds(pl.multiple_of(r * ROW_BLOCK, ROW_BLOCK), ROW_BLOCK))
        return carry

    lax.fori_loop(0, n_rows // ROW_BLOCK, body, 0, unroll=4)


def _ffn_kernel(*refs, emit_next_norm):
    if emit_next_norm:
        x_ref, gpre_ref, gpost_ref, gnext_ref, wg_ref, wu_ref, wd_ref, o_ref, h_ref = refs
    else:
        x_ref, gpre_ref, gpost_ref, wg_ref, wu_ref, wd_ref, o_ref, h_ref = refs
    j = pl.program_id(1)
    tm = x_ref.shape[0]

    @pl.when(j == 0)
    def _():
        def prologue(rows):
            h_ref[rows, :] = _rms(x_ref[rows, :], gpre_ref[...]).astype(BF16)
            o_ref[rows, :] = jnp.zeros((ROW_BLOCK, o_ref.shape[1]), F32)

        _for_row_blocks(tm, prologue)

    h = h_ref[...]
    gate = jnp.dot(h, wg_ref[...].astype(BF16), preferred_element_type=F32)
    up = jnp.dot(h, wu_ref[...].astype(BF16), preferred_element_type=F32)
    act = (gate * _sigmoid(gate) * up).astype(BF16)
    o_ref[...] += jnp.dot(act, wd_ref[...].astype(BF16), preferred_element_type=F32)

    @pl.when(j == pl.num_programs(1) - 1)
    def _():
        for r in range(0, tm, ROW_BLOCK):
            rows = pl.ds(r, ROW_BLOCK)
            new_x = x_ref[rows, :] + 0.5 * _rms(o_ref[rows, :], gpost_ref[...])
            o_ref[rows, :] = new_x
            if emit_next_norm:
                h_ref[rows, :] = _rms(new_x, gnext_ref[...]).astype(BF16)


def _ffn(x, gpre, gpost, w_gate_up, w_down, gnext=None, *, tm=1024, tf=256):
    m, d = x.shape
    nj = D_FF // tf
    emit = gnext is not None
    row_tile = pl.BlockSpec((tm, d), lambda i, j: (i, 0))
    gain_spec = pl.BlockSpec((1, d), lambda i, j: (0, 0))
    weight_specs = [
        pl.BlockSpec((d, tf), lambda i, j: (0, j)),
        pl.BlockSpec((d, tf), lambda i, j: (0, j + nj)),
        pl.BlockSpec((tf, d), lambda i, j: (j, 0)),
    ]
    gains = (gpre, gpost, gnext) if emit else (gpre, gpost)
    return pl.pallas_call(
        functools.partial(_ffn_kernel, emit_next_norm=emit),
        grid=(m // tm, nj),
        in_specs=[row_tile] + [gain_spec] * len(gains) + weight_specs,
        out_specs=(row_tile, row_tile) if emit else row_tile,
        out_shape=((jax.ShapeDtypeStruct((m, d), F32), jax.ShapeDtypeStruct((m, d), BF16)) if emit
                   else jax.ShapeDtypeStruct((m, d), F32)),
        scratch_shapes=[] if emit else [pltpu.VMEM((tm, d), BF16)],
        compiler_params=pltpu.CompilerParams(
            dimension_semantics=("parallel", "arbitrary"), vmem_limit_bytes=VMEM_LIMIT_BYTES),
        name="ffn",
    )(x, *gains, w_gate_up, w_gate_up, w_down)


def _proj_kernel(xn_ref, w_ref, o_ref):
    res = jnp.dot(xn_ref[...], w_ref[...].astype(BF16), preferred_element_type=F32).astype(o_ref.dtype)
    for c in range(o_ref.shape[0]):
        o_ref[c] = res[:, c * HEAD_DIM:(c + 1) * HEAD_DIM]


def _proj(xn, w_in, col_blocks, out_dtype, *, tm=2048, tn=512):
    m, d = xn.shape
    per = tn // HEAD_DIM
    jumps = [(pos, col_blocks[pos] - col_blocks[pos - 1] - 1) for pos in range(1, len(col_blocks))
             if col_blocks[pos] != col_blocks[pos - 1] + 1]

    def w_block(i, j):
        blk = j + col_blocks[0]
        for pos, gap in jumps:
            blk = blk + jnp.where(j >= pos, gap, 0)
        return (0, blk)

    return pl.pallas_call(
        _proj_kernel,
        grid=(m // tm, len(col_blocks)),
        in_specs=[
            pl.BlockSpec((tm, d), lambda i, j: (i, 0)),
            pl.BlockSpec((d, tn), w_block),
        ],
        out_specs=pl.BlockSpec((per, tm, HEAD_DIM), lambda i, j: (j, i, 0)),
        out_shape=jax.ShapeDtypeStruct((len(col_blocks) * per, m, HEAD_DIM), out_dtype),
        compiler_params=pltpu.CompilerParams(
            dimension_semantics=("parallel", "arbitrary"), vmem_limit_bytes=VMEM_LIMIT_BYTES),
        name="inproj",
    )(xn, w_in)


def _chunk_cumsum(x, ones_tri):
    n = x.shape[1]
    hi = x.astype(BF16)
    rest = x - hi.astype(F32)
    mid = rest.astype(BF16)
    lo = (rest - mid.astype(F32)).astype(BF16)
    sums = jnp.dot(ones_tri, jnp.concatenate([hi, mid, lo], axis=1), preferred_element_type=F32)
    return sums[:, :n] + sums[:, n:2 * n] + sums[:, 2 * n:]


def _hgrn_kernel(q_ref, v_ref, ff_ref, fb_ref, g_ref, lbf_ref, lbb_ref, gain_ref, o_ref,
                 qd_ref, kv_ref, dec_ref, st_ref, oi_ref):
    seq = q_ref.shape[0]
    n_chunks = seq // CHUNK
    pair = 2 * CHUNK
    contract_last = (((1,), (1,)), ((), ()))

    def gate_consts(lb_param_ref):
        a = lb_param_ref[...]
        e = jnp.exp(a - jnp.max(a, axis=0, keepdims=True))
        lb = e[0:1] / jnp.sum(e, axis=0, keepdims=True)
        return 0.5 * (1.0 + lb), 0.5 * (1.0 - lb)

    consts = (gate_consts(lbf_ref), gate_consts(lbb_ref))
    logit_refs = (ff_ref, fb_ref)

    r = lax.broadcasted_iota(jnp.int32, (pair, pair), 0)
    s = lax.broadcasted_iota(jnp.int32, (pair, pair), 1)
    same_chunk = (r // CHUNK) == (s // CHUNK)
    tri = (same_chunk & (r >= s), same_chunk & (s >= r))
    ones_tri = tuple(jnp.where(m, 1.0, 0.0).astype(BF16) for m in tri)
    zero_blk = jnp.zeros((CHUNK, HEAD_DIM), BF16)

    def trip_body(trip, carry):
        pairs = [trip * PAIRS_PER_TRIP + i for i in range(PAIRS_PER_TRIP)]
        rows_of = [pl.ds(pl.multiple_of(pi * pair, pair), pair) for pi in pairs]

        gates = []
        for rows in rows_of:
            per_dir = []
            for d in range(2):
                mid, half = consts[d]
                t = jnp.tanh(0.5 * logit_refs[d][rows, :])
                f = mid + half * t
                k = half * (1.0 - t)
                per_dir.append((k, _chunk_cumsum(jnp.log(f) * LOG2_E, ones_tri[d])))
            gates.append(per_dir)

        mixes = []
        for pi, rows, per_dir in zip(pairs, rows_of, gates):
            q = q_ref[rows, :].astype(F32)
            vb = v_ref[rows, :]
            v_t = vb.astype(F32).T.astype(BF16)
            probs = None
            for d, (k, c) in enumerate(per_dir):
                c3 = c.reshape(2, CHUNK, HEAD_DIM)
                tot = c3[:, 0:1, :] if d == 1 else c3[:, CHUNK - 1:CHUNK, :]
                k_tail = (k * jnp.exp2(tot - c3).reshape(pair, HEAD_DIM)).astype(BF16)
                q_dec = (q * jnp.exp2(c)).astype(BF16)
                k_dec = (k * jnp.exp2(-c)).astype(BF16)
                qd_ref[rows, d * HEAD_DIM:(d + 1) * HEAD_DIM] = q_dec
                dec_ref[d, pl.ds(pi * 2, 2), :] = jnp.exp2(tot).reshape(2, HEAD_DIM)
                sc = lax.dot_general(q_dec, k_dec, contract_last, preferred_element_type=F32)
                sc = jnp.where(tri[d], sc, 0.0)
                probs = sc if probs is None else probs + sc
                rhs = jnp.concatenate(
                    [jnp.concatenate([k_tail[:CHUNK], zero_blk], axis=1),
                     jnp.concatenate([zero_blk, k_tail[CHUNK:]], axis=1)], axis=0)
                kv = jnp.dot(v_t, rhs, preferred_element_type=F32)
                kv_ref[d, pi * 2] = kv[:, :HEAD_DIM]
                kv_ref[d, pi * 2 + 1] = kv[:, HEAD_DIM:]
            mixes.append((probs.astype(BF16), vb))

        for rows, (probs, vb) in zip(rows_of, mixes):
            oi_ref[rows, :] = jnp.dot(probs, vb, preferred_element_type=F32)
        return carry

    lax.fori_loop(0, n_chunks // (2 * PAIRS_PER_TRIP), trip_body, 0)

    def scan_body(t, carry):
        sf, sb = carry
        nf, nb = t, n_chunks - 1 - t
        st_ref[nf, :, :HEAD_DIM] = sf.astype(BF16)
        st_ref[nb, :, HEAD_DIM:] = sb.astype(BF16)
        sf = sf * dec_ref[0, pl.ds(nf, 1), :] + kv_ref[0, nf]
        sb = sb * dec_ref[1, pl.ds(nb, 1), :] + kv_ref[1, nb]
        return sf, sb

    zero = jnp.zeros((HEAD_DIM, HEAD_DIM), F32)
    lax.fori_loop(0, n_chunks, scan_body, (zero, zero), unroll=2)

    def out_body(trip, carry):
        chunks = [trip * CHUNKS_PER_OUT_TRIP + i for i in range(CHUNKS_PER_OUT_TRIP)]
        rows_of = [pl.ds(pl.multiple_of(n * CHUNK, CHUNK), CHUNK) for n in chunks]
        inter = [lax.dot_general(qd_ref[rows, :], st_ref[n], contract_last, preferred_element_type=F32)
                 for n, rows in zip(chunks, rows_of)]
        for rows, o_inter in zip(rows_of, inter):
            o = _rms(oi_ref[rows, :] + o_inter, gain_ref[...])
            g = g_ref[rows, :].astype(F32)
            o_ref[rows, :] = (o * (0.5 * g * (1.0 + jnp.tanh(0.5 * g)))).astype(o_ref.dtype)
        return carry

    lax.fori_loop(0, n_chunks // CHUNKS_PER_OUT_TRIP, out_body, 0)


def _hgrn(p16, p_f, lbf, lbb, out_gain, batch, seq):
    hw = HGRN_HEADS

    def slab(group):
        return lambda b, h: (group * hw + h, b, 0)

    blk = (None, seq, HEAD_DIM)
    return pl.pallas_call(
        _hgrn_kernel,
        grid=(batch, hw),
        in_specs=[
            pl.BlockSpec(blk, slab(0)),
            pl.BlockSpec(blk, slab(1)),
            pl.BlockSpec(blk, slab(0)),
            pl.BlockSpec(blk, slab(1)),
            pl.BlockSpec(blk, slab(2)),
            pl.BlockSpec((lbf.shape[0], HEAD_DIM), lambda b, h: (0, h)),
            pl.BlockSpec((lbb.shape[0], HEAD_DIM), lambda b, h: (0, h)),
            pl.BlockSpec((1, HEAD_DIM), lambda b, h: (0, h)),
        ],
        out_specs=pl.BlockSpec(blk, lambda b, h: (h, b, 0)),
        out_shape=jax.ShapeDtypeStruct((hw, batch * seq, HEAD_DIM), BF16),
        scratch_shapes=[
            pltpu.VMEM((seq, 2 * HEAD_DIM), BF16),
            pltpu.VMEM((2, seq // CHUNK, HEAD_DIM, HEAD_DIM), F32),
            pltpu.VMEM((2, seq // CHUNK, HEAD_DIM), F32),
            pltpu.VMEM((seq // CHUNK, HEAD_DIM, 2 * HEAD_DIM), BF16),
            pltpu.VMEM((seq, HEAD_DIM), F32),
        ],
        compiler_params=pltpu.CompilerParams(
            dimension_semantics=("parallel", "parallel"), vmem_limit_bytes=VMEM_LIMIT_BYTES),
        name="hgrn",
    )(p16, p16, p_f, p_f, p16, lbf, lbb, out_gain)


def _t5_bucket_table():
    nb = REL_BUCKETS // 2
    max_exact = nb // 2
    c = np.arange(ATTN_BLOCK)[:, None]
    s = np.arange(KEY_SPAN)[None, :]
    rel = s - WINDOW - c
    bucket = (rel > 0).astype(np.int32) * nb
    n = np.abs(rel)
    large = max_exact + (np.log(np.maximum(n, 1) / max_exact) / np.log(REL_MAX_DIST / max_exact)
                         * (nb - max_exact)).astype(np.int32)
    large = np.minimum(large, nb - 1)
    bucket = bucket + np.where(n < max_exact, n, large).astype(np.int32)
    return np.where(np.abs(rel) <= WINDOW, bucket, -1).astype(np.int32)


def _attn_kernel(table_ref, sink_ref, bucket_ref, q_ref, k_ref, v_ref, o_ref, bias_ref):
    x = pl.program_id(1)
    seq = k_ref.shape[0]
    n_blocks = seq // ATTN_BLOCK
    scale = 1.0 / math.sqrt(HEAD_DIM)
    rows = GROUP * ATTN_BLOCK

    @pl.when((pl.program_id(0) == 0) & (x == 0))
    def _():
        bucket = bucket_ref[...]
        for head in range(ATTN_HEADS):
            bias = jnp.full((ATTN_BLOCK, KEY_SPAN), NEG_INF, F32)
            for b in range(REL_BUCKETS):
                bias = jnp.where(bucket == b, table_ref[b, head], bias)
            bias_ref[pl.ds(head * ATTN_BLOCK, ATTN_BLOCK), :] = bias

    row_head = lax.broadcasted_iota(jnp.int32, (rows, 1), 0) // ATTN_BLOCK
    sink = jnp.zeros((rows, 1), F32)
    for g in range(GROUP):
        sink = jnp.where(row_head == g, sink_ref[0, x * GROUP + g], sink)
    bias_rows = pl.ds(pl.multiple_of(x * rows, rows), rows)

    def block_rows(i):
        return pl.ds(i * ATTN_BLOCK, ATTN_BLOCK)

    def band(ref, i):
        return jnp.concatenate([ref[block_rows(max(i - 1, 0)), :], ref[block_rows(i), :],
                                ref[block_rows(min(i + 1, n_blocks - 1)), :]], axis=0)

    def scores(i):
        qs = jnp.concatenate([q_ref[g, block_rows(i), :] for g in range(GROUP)], axis=0)
        sc = lax.dot_general(qs, band(k_ref, i), (((1,), (1,)), ((), ())), preferred_element_type=F32)
        sc = sc * scale + bias_ref[bias_rows, :]
        if i in (0, n_blocks - 1):
            key_pos = (i - 1) * ATTN_BLOCK + lax.broadcasted_iota(jnp.int32, (rows, KEY_SPAN), 1)
            sc = jnp.where((key_pos >= 0) & (key_pos < seq), sc, NEG_INF)
        return sc

    def finish(i, sc):
        m = jnp.maximum(jnp.max(sc, axis=-1, keepdims=True), sink)
        pr = jnp.exp(sc - m)
        den = jnp.sum(pr, axis=-1, keepdims=True) + jnp.exp(sink - m)
        o = jnp.dot(pr.astype(BF16), band(v_ref, i), preferred_element_type=F32) / den
        for g in range(GROUP):
            o_ref[g, block_rows(i), :] = o[g * ATTN_BLOCK:(g + 1) * ATTN_BLOCK].astype(o_ref.dtype)

    for first in range(0, n_blocks, ATTN_BLOCKS_PER_GROUP):
        group = range(first, first + ATTN_BLOCKS_PER_GROUP)
        logits = [scores(i) for i in group]
        for i, sc in zip(group, logits):
            finish(i, sc)


def _attention(p16, sink, rel_table, batch, seq):
    q_blk0 = 3 * HGRN_HEADS // GROUP
    k_slab0 = 3 * HGRN_HEADS + ATTN_HEADS
    v_slab0 = k_slab0 + KV_HEADS
    bucket = jnp.asarray(_t5_bucket_table())
    smem = pl.BlockSpec(memory_space=pltpu.SMEM)
    return pl.pallas_call(
        _attn_kernel,
        grid=(batch, KV_HEADS),
        in_specs=[
            smem,
            smem,
            pl.BlockSpec((ATTN_BLOCK, KEY_SPAN), lambda b, x: (0, 0)),
            pl.BlockSpec((GROUP, seq, HEAD_DIM), lambda b, x: (q_blk0 + x, b, 0)),
            pl.BlockSpec((None, seq, HEAD_DIM), lambda b, x: (k_slab0 + x, b, 0)),
            pl.BlockSpec((None, seq, HEAD_DIM), lambda b, x: (v_slab0 + x, b, 0)),
        ],
        out_specs=pl.BlockSpec((GROUP, seq, HEAD_DIM), lambda b, x: (x, b, 0)),
        out_shape=jax.ShapeDtypeStruct((ATTN_HEADS, batch * seq, HEAD_DIM), BF16),
        scratch_shapes=[
            pltpu.VMEM((ATTN_HEADS * ATTN_BLOCK, KEY_SPAN), F32),
        ],
        compiler_params=pltpu.CompilerParams(
            dimension_semantics=("arbitrary", "arbitrary"), vmem_limit_bytes=VMEM_LIMIT_BYTES),
        name="attention",
    )(rel_table, sink, bucket, p16, p16, p16)


def _outproj_kernel(yh_ref, ya_ref, x_ref, wh_ref, wa_ref, g_ref, o_ref):
    y_h = jnp.concatenate([yh_ref[c] for c in range(yh_ref.shape[0])], axis=1)
    y_a = jnp.concatenate([ya_ref[c] for c in range(ya_ref.shape[0])], axis=1)
    mixed = jnp.dot(y_h, wh_ref[...], preferred_element_type=F32)
    mixed = mixed + jnp.dot(y_a, wa_ref[...], preferred_element_type=F32)
    o_ref[...] = x_ref[...] + _rms(mixed, g_ref[...])


def _outproj(y_h, y_a, x, w_out, gain, *, tm=512):
    m, d = x.shape
    return pl.pallas_call(
        _outproj_kernel,
        grid=(m // tm,),
        in_specs=[
            pl.BlockSpec((HGRN_HEADS, tm, HEAD_DIM), lambda i: (0, i, 0)),
            pl.BlockSpec((ATTN_HEADS, tm, HEAD_DIM), lambda i: (0, i, 0)),
            pl.BlockSpec((tm, d), lambda i: (i, 0)),
            pl.BlockSpec((HGRN_WIDTH, d), lambda i: (0, 0)),
            pl.BlockSpec((ATTN_WIDTH, d), lambda i: (1, 0)),
            pl.BlockSpec((1, d), lambda i: (0, 0)),
        ],
        out_specs=pl.BlockSpec((tm, d), lambda i: (i, 0)),
        out_shape=jax.ShapeDtypeStruct((m, d), F32),
        compiler_params=pltpu.CompilerParams(
            dimension_semantics=("parallel",), vmem_limit_bytes=VMEM_LIMIT_BYTES),
        name="outproj",
    )(y_h, y_a, x, w_out, w_out, gain)


def kernel(x, pre_norm_ffn1, post_norm_ffn1, w_ffn1_gate_up, w_ffn1_down, pre_norm_mix, post_norm_mix,
           w_mix_in, hgrn_lower_bounds_fwd, hgrn_lower_bounds_bwd, hgrn_out_norm, attn_sink, w_mix_out,
           pre_norm_ffn2, post_norm_ffn2, w_ffn2_gate_up, w_ffn2_down, rel_bias_table):
    batch, seq, d = x.shape
    depth = pre_norm_ffn1.shape[0]
    assert depth == 1 and d == D_MODEL
    xf = x.reshape(batch * seq, d)
    layer = 0
    xf, xn = _ffn(xf, pre_norm_ffn1[layer:layer + 1], post_norm_ffn1[layer:layer + 1],
                  w_ffn1_gate_up[layer], w_ffn1_down[layer], gnext=pre_norm_mix[layer:layer + 1])
    w_in = w_mix_in[layer]
    p16 = _proj(xn, w_in, [0, 1, 2, 3, 8, 9, 10, 11, 12], BF16)
    p_f = _proj(xn, w_in, [4, 5, 6, 7], F32)
    y_h = _hgrn(p16, p_f, hgrn_lower_bounds_fwd, hgrn_lower_bounds_bwd,
                hgrn_out_norm[layer:layer + 1], batch, seq)
    y_a = _attention(p16, attn_sink[layer:layer + 1], rel_bias_table, batch, seq)
    xf = _outproj(y_h, y_a, xf, w_mix_out[layer].astype(BF16), post_norm_mix[layer:layer + 1])
    xf = _ffn(xf, pre_norm_ffn2[layer:layer + 1], post_norm_ffn2[layer:layer + 1],
              w_ffn2_gate_up[layer], w_ffn2_down[layer])
    return xf.reshape(batch, seq, d)
```

```python
import functools
import math

import jax
import jax.numpy as jnp
import numpy as np
from jax import lax
from jax.experimental import pallas as pl
from jax.experimental.pallas import tpu as pltpu

F32 = jnp.float32
BF16 = jnp.bfloat16

D_MODEL = 2048
HGRN_WIDTH = 1024
HEAD_DIM = 128
HGRN_HEADS = HGRN_WIDTH // HEAD_DIM
CHUNK = 64
ATTN_WIDTH = 1024
ATTN_HEADS = ATTN_WIDTH // HEAD_DIM
KV_HEADS = 2
GROUP = ATTN_HEADS // KV_HEADS
KV_WIDTH = KV_HEADS * HEAD_DIM
WINDOW = 128
ATTN_BLOCK = 128
KEY_SPAN = ATTN_BLOCK + 2 * WINDOW
REL_BUCKETS = 32
REL_MAX_DIST = 128
D_FF = 5632
EPS = 1e-6
NEG_INF = -1e30
LOG2_E = 1.0 / math.log(2.0)

MIB = 1024 * 1024
VMEM_LIMIT_BYTES = 56 * MIB
ROW_BLOCK = 32
PAIRS_PER_TRIP = 16
CHUNKS_PER_OUT_TRIP = 16
ATTN_BLOCKS_PER_GROUP = 2


def _rms(x, gain):
    return x * lax.rsqrt(jnp.mean(x * x, axis=-1, keepdims=True) + EPS) * gain


def _sigmoid(x):
    return 1.0 / (1.0 + jnp.exp(-x))


def _for_row_blocks(n_rows, fn):
    def body(r, carry):
        fn(pl.ds(pl.multiple_of(r * ROW_BLOCK, ROW_BLOCK), ROW_BLOCK))
        return carry

    lax.fori_loop(0, n_rows // ROW_BLOCK, body, 0, unroll=4)


def _ffn_kernel(*refs, emit_next_norm):
    if emit_next_norm:
        x_ref, gpre_ref, gpost_ref, gnext_ref, wg_ref, wu_ref, wd_ref, o_ref, h_ref = refs
    else:
        x_ref, gpre_ref, gpost_ref, wg_ref, wu_ref, wd_ref, o_ref, h_ref = refs
    j = pl.program_id(1)
    tm = x_ref.shape[0]

    @pl.when(j == 0)
    def _():
        def prologue(rows):
            h_ref[rows, :] = _rms(x_ref[rows, :], gpre_ref[...]).astype(BF16)
            o_ref[rows, :] = jnp.zeros((ROW_BLOCK, o_ref.shape[1]), F32)

        _for_row_blocks(tm, prologue)

    h = h_ref[...]
    gate = jnp.dot(h, wg_ref[...].astype(BF16), preferred_element_type=F32)
    up = jnp.dot(h, wu_ref[...].astype(BF16), preferred_element_type=F32)
    act = (gate * _sigmoid(gate) * up).astype(BF16)
    o_ref[...] += jnp.dot(act, wd_ref[...].astype(BF16), preferred_element_type=F32)

    @pl.when(j == pl.num_programs(1) - 1)
    def _():
        for r in range(0, tm, ROW_BLOCK):
            rows = pl.ds(r, ROW_BLOCK)
            new_x = x_ref[rows, :] + 0.5 * _rms(o_ref[rows, :], gpost_ref[...])
            o_ref[rows, :] = new_x
            if emit_next_norm:
                h_ref[rows, :] = _rms(new_x, gnext_ref[...]).astype(BF16)


def _ffn(x, gpre, gpost, w_gate_up, w_down, gnext=None, *, tm=1024, tf=256):
    m, d = x.shape
    nj = D_FF // tf
    emit = gnext is not None
    row_tile = pl.BlockSpec((tm, d), lambda i, j: (i, 0))
    gain_spec = pl.BlockSpec((1, d), lambda i, j: (0, 0))
    weight_specs = [
        pl.BlockSpec((d, tf), lambda i, j: (0, j)),
        pl.BlockSpec((d, tf), lambda i, j: (0, j + nj)),
        pl.BlockSpec((tf, d), lambda i, j: (j, 0)),
    ]
    gains = (gpre, gpost, gnext) if emit else (gpre, gpost)
    return pl.pallas_call(
        functools.partial(_ffn_kernel, emit_next_norm=emit),
        grid=(m // tm, nj),
        in_specs=[row_tile] + [gain_spec] * len(gains) + weight_specs,
        out_specs=(row_tile, row_tile) if emit else row_tile,
        out_shape=((jax.ShapeDtypeStruct((m, d), F32), jax.ShapeDtypeStruct((m, d), BF16)) if emit
                   else jax.ShapeDtypeStruct((m, d), F32)),
        scratch_shapes=[] if emit else [pltpu.VMEM((tm, d), BF16)],
        compiler_params=pltpu.CompilerParams(
            dimension_semantics=("parallel", "arbitrary"), vmem_limit_bytes=VMEM_LIMIT_BYTES),
        name="ffn",
    )(x, *gains, w_gate_up, w_gate_up, w_down)


def _proj_kernel(xn_ref, w_ref, o_ref):
    res = jnp.dot(xn_ref[...], w_ref[...].astype(BF16), preferred_element_type=F32).astype(o_ref.dtype)
    for c in range(o_ref.shape[0]):
        o_ref[c] = res[:, c * HEAD_DIM:(c + 1) * HEAD_DIM]


def _proj(xn, w_in, col_blocks, out_dtype, *, tm=2048, tn=512):
    m, d = xn.shape
    per = tn // HEAD_DIM
    jumps = [(pos, col_blocks[pos] - col_blocks[pos - 1] - 1) for pos in range(1, len(col_blocks))
             if col_blocks[pos] != col_blocks[pos - 1] + 1]

    def w_block(i, j):
        blk = j + col_blocks[0]
        for pos, gap in jumps:
            blk = blk + jnp.where(j >= pos, gap, 0)
        return (0, blk)

    return pl.pallas_call(
        _proj_kernel,
        grid=(m // tm, len(col_blocks)),
        in_specs=[
            pl.BlockSpec((tm, d), lambda i, j: (i, 0)),
            pl.BlockSpec((d, tn), w_block),
        ],
        out_specs=pl.BlockSpec((per, tm, HEAD_DIM), lambda i, j: (j, i, 0)),
        out_shape=jax.ShapeDtypeStruct((len(col_blocks) * per, m, HEAD_DIM), out_dtype),
        compiler_params=pltpu.CompilerParams(
            dimension_semantics=("parallel", "arbitrary"), vmem_limit_bytes=VMEM_LIMIT_BYTES),
        name="inproj",
    )(xn, w_in)


def _chunk_cumsum(x, ones_tri):
    n = x.shape[1]
    hi = x.astype(BF16)
    rest = x - hi.astype(F32)
    mid = rest.astype(BF16)
    lo = (rest - mid.astype(F32)).astype(BF16)
    sums = jnp.dot(ones_tri, jnp.concatenate([hi, mid, lo], axis=1), preferred_element_type=F32)
    return sums[:, :n] + sums[:, n:2 * n] + sums[:, 2 * n:]


def _hgrn_kernel(q_ref, v_ref, ff_ref, fb_ref, g_ref, lbf_ref, lbb_ref, gain_ref, wa_ref, wb_ref,
                 o_ref, wa16_ref, wb16_ref, qd_ref, kv_ref, dec_ref, st_ref, oi_ref):
    wa16_ref[...] = wa_ref[...].astype(BF16)
    wb16_ref[...] = wb_ref[...].astype(BF16)

    seq = q_ref.shape[0]
    n_chunks = seq // CHUNK
    pair = 2 * CHUNK
    contract_last = (((1,), (1,)), ((), ()))

    def gate_consts(lb_param_ref):
        a = lb_param_ref[...]
        e = jnp.exp(a - jnp.max(a, axis=0, keepdims=True))
        lb = e[0:1] / jnp.sum(e, axis=0, keepdims=True)
        return 0.5 * (1.0 + lb), 0.5 * (1.0 - lb)

    consts = (gate_consts(lbf_ref), gate_consts(lbb_ref))
    logit_refs = (ff_ref, fb_ref)

    r = lax.broadcasted_iota(jnp.int32, (pair, pair), 0)
    s = lax.broadcasted_iota(jnp.int32, (pair, pair), 1)
    same_chunk = (r // CHUNK) == (s // CHUNK)
    tri = (same_chunk & (r >= s), same_chunk & (s >= r))
    ones_tri = tuple(jnp.where(m, 1.0, 0.0).astype(BF16) for m in tri)
    zero_blk = jnp.zeros((CHUNK, HEAD_DIM), BF16)

    def trip_body(trip, carry):
        pairs = [trip * PAIRS_PER_TRIP + i for i in range(PAIRS_PER_TRIP)]
        rows_of = [pl.ds(pl.multiple_of(pi * pair, pair), pair) for pi in pairs]

        gates = []
        for rows in rows_of:
            per_dir = []
            for d in range(2):
                mid, half = consts[d]
                t = jnp.tanh(0.5 * logit_refs[d][rows, :])
                f = mid + half * t
                k = half * (1.0 - t)
                per_dir.append((k, _chunk_cumsum(jnp.log(f) * LOG2_E, ones_tri[d])))
            gates.append(per_dir)

        mixes = []
        for pi, rows, per_dir in zip(pairs, rows_of, gates):
            q = q_ref[rows, :].astype(F32)
            vb = v_ref[rows, :]
            v_t = vb.astype(F32).T.astype(BF16)
            probs = None
            for d, (k, c) in enumerate(per_dir):
                c3 = c.reshape(2, CHUNK, HEAD_DIM)
                tot = c3[:, 0:1, :] if d == 1 else c3[:, CHUNK - 1:CHUNK, :]
                k_tail = (k * jnp.exp2(tot - c3).reshape(pair, HEAD_DIM)).astype(BF16)
                q_dec = (q * jnp.exp2(c)).astype(BF16)
                k_dec = (k * jnp.exp2(-c)).astype(BF16)
                qd_ref[rows, d * HEAD_DIM:(d + 1) * HEAD_DIM] = q_dec
                dec_ref[d, pl.ds(pi * 2, 2), :] = jnp.exp2(tot).reshape(2, HEAD_DIM)
                sc = lax.dot_general(q_dec, k_dec, contract_last, preferred_element_type=F32)
                sc = jnp.where(tri[d], sc, 0.0)
                probs = sc if probs is None else probs + sc
                rhs = jnp.concatenate(
                    [jnp.concatenate([k_tail[:CHUNK], zero_blk], axis=1),
                     jnp.concatenate([zero_blk, k_tail[CHUNK:]], axis=1)], axis=0)
                kv = jnp.dot(v_t, rhs, preferred_element_type=F32)
                kv_ref[d, pi * 2] = kv[:, :HEAD_DIM]
                kv_ref[d, pi * 2 + 1] = kv[:, HEAD_DIM:]
            mixes.append((probs.astype(BF16), vb))

        for rows, (probs, vb) in zip(rows_of, mixes):
            oi_ref[rows, :] = jnp.dot(probs, vb, preferred_element_type=F32)
        return carry

    lax.fori_loop(0, n_chunks // (2 * PAIRS_PER_TRIP), trip_body, 0)

    def scan_body(t, carry):
        sf, sb = carry
        nf, nb = t, n_chunks - 1 - t
        st_ref[nf, :, :HEAD_DIM] = sf.astype(BF16)
        st_ref[nb, :, HEAD_DIM:] = sb.astype(BF16)
        sf = sf * dec_ref[0, pl.ds(nf, 1), :] + kv_ref[0, nf]
        sb = sb * dec_ref[1, pl.ds(nb, 1), :] + kv_ref[1, nb]
        return sf, sb

    zero = jnp.zeros((HEAD_DIM, HEAD_DIM), F32)
    lax.fori_loop(0, n_chunks, scan_body, (zero, zero), unroll=2)

    def out_body(trip, carry):
        chunks = [trip * CHUNKS_PER_OUT_TRIP + i for i in range(CHUNKS_PER_OUT_TRIP)]
        rows_of = [pl.ds(pl.multiple_of(n * CHUNK, CHUNK), CHUNK) for n in chunks]
        inter = [lax.dot_general(qd_ref[rows, :], st_ref[n], contract_last, preferred_element_type=F32)
                 for n, rows in zip(chunks, rows_of)]
        for rows, o_inter in zip(rows_of, inter):
            o = _rms(oi_ref[rows, :] + o_inter, gain_ref[...])
            g = g_ref[rows, :].astype(F32)
            o_ref[rows, :] = (o * (0.5 * g * (1.0 + jnp.tanh(0.5 * g)))).astype(o_ref.dtype)
        return carry

    lax.fori_loop(0, n_chunks // CHUNKS_PER_OUT_TRIP, out_body, 0)


def _hgrn(p16, p_f, lbf, lbb, out_gain, w_a, w_b, batch, seq):
    hw = HGRN_HEADS
    steps = batch * hw

    def slab(group):
        return lambda b, h: (group * hw + h, b, 0)

    def row_slab(w):
        return pl.BlockSpec((w.shape[0] // steps, w.shape[1]), lambda b, h: (b * hw + h, 0))

    blk = (None, seq, HEAD_DIM)
    return pl.pallas_call(
        _hgrn_kernel,
        grid=(batch, hw),
        in_specs=[
            pl.BlockSpec(blk, slab(0)),
            pl.BlockSpec(blk, slab(1)),
            pl.BlockSpec(blk, slab(0)),
            pl.BlockSpec(blk, slab(1)),
            pl.BlockSpec(blk, slab(2)),
            pl.BlockSpec((lbf.shape[0], HEAD_DIM), lambda b, h: (0, h)),
            pl.BlockSpec((lbb.shape[0], HEAD_DIM), lambda b, h: (0, h)),
            pl.BlockSpec((1, HEAD_DIM), lambda b, h: (0, h)),
            row_slab(w_a),
            row_slab(w_b),
        ],
        out_specs=(pl.BlockSpec(blk, lambda b, h: (h, b, 0)), row_slab(w_a), row_slab(w_b)),
        out_shape=(jax.ShapeDtypeStruct((hw, batch * seq, HEAD_DIM), BF16),
                   jax.ShapeDtypeStruct(w_a.shape, BF16), jax.ShapeDtypeStruct(w_b.shape, BF16)),
        scratch_shapes=[
            pltpu.VMEM((seq, 2 * HEAD_DIM), BF16),
            pltpu.VMEM((2, seq // CHUNK, HEAD_DIM, HEAD_DIM), F32),
            pltpu.VMEM((2, seq // CHUNK, HEAD_DIM), F32),
            pltpu.VMEM((seq // CHUNK, HEAD_DIM, 2 * HEAD_DIM), BF16),
            pltpu.VMEM((seq, HEAD_DIM), F32),
        ],
        compiler_params=pltpu.CompilerParams(
            dimension_semantics=("parallel", "parallel"), vmem_limit_bytes=VMEM_LIMIT_BYTES),
        name="hgrn",
    )(p16, p16, p_f, p_f, p16, lbf, lbb, out_gain, w_a, w_b)


def _t5_bucket_table():
    nb = REL_BUCKETS // 2
    max_exact = nb // 2
    c = np.arange(ATTN_BLOCK)[:, None]
    s = np.arange(KEY_SPAN)[None, :]
    rel = s - WINDOW - c
    bucket = (rel > 0).astype(np.int32) * nb
    n = np.abs(rel)
    large = max_exact + (np.log(np.maximum(n, 1) / max_exact) / np.log(REL_MAX_DIST / max_exact)
                         * (nb - max_exact)).astype(np.int32)
    large = np.minimum(large, nb - 1)
    bucket = bucket + np.where(n < max_exact, n, large).astype(np.int32)
    return np.where(np.abs(rel) <= WINDOW, bucket, -1).astype(np.int32)


def _attn_kernel(table_ref, sink_ref, bucket_ref, q_ref, k_ref, v_ref, o_ref, bias_ref):
    x = pl.program_id(1)
    seq = k_ref.shape[0]
    n_blocks = seq // ATTN_BLOCK
    scale = 1.0 / math.sqrt(HEAD_DIM)
    rows = GROUP * ATTN_BLOCK

    @pl.when((pl.program_id(0) == 0) & (x == 0))
    def _():
        bucket = bucket_ref[...]
        for head in range(ATTN_HEADS):
            bias = jnp.full((ATTN_BLOCK, KEY_SPAN), NEG_INF, F32)
            for b in range(REL_BUCKETS):
                bias = jnp.where(bucket == b, table_ref[b, head], bias)
            bias_ref[pl.ds(head * ATTN_BLOCK, ATTN_BLOCK), :] = bias

    row_head = lax.broadcasted_iota(jnp.int32, (rows, 1), 0) // ATTN_BLOCK
    sink = jnp.zeros((rows, 1), F32)
    for g in range(GROUP):
        sink = jnp.where(row_head == g, sink_ref[0, x * GROUP + g], sink)
    bias_rows = pl.ds(pl.multiple_of(x * rows, rows), rows)

    def block_rows(i):
        return pl.ds(i * ATTN_BLOCK, ATTN_BLOCK)

    def band(ref, i):
        return jnp.concatenate([ref[block_rows(max(i - 1, 0)), :], ref[block_rows(i), :],
                                ref[block_rows(min(i + 1, n_blocks - 1)), :]], axis=0)

    def scores(i):
        qs = jnp.concatenate([q_ref[g, block_rows(i), :] for g in range(GROUP)], axis=0)
        sc = lax.dot_general(qs, band(k_ref, i), (((1,), (1,)), ((), ())), preferred_element_type=F32)
        sc = sc * scale + bias_ref[bias_rows, :]
        if i in (0, n_blocks - 1):
            key_pos = (i - 1) * ATTN_BLOCK + lax.broadcasted_iota(jnp.int32, (rows, KEY_SPAN), 1)
            sc = jnp.where((key_pos >= 0) & (key_pos < seq), sc, NEG_INF)
        return sc

    def finish(i, sc):
        m = jnp.maximum(jnp.max(sc, axis=-1, keepdims=True), sink)
        pr = jnp.exp(sc - m)
        den = jnp.sum(pr, axis=-1, keepdims=True) + jnp.exp(sink - m)
        o = jnp.dot(pr.astype(BF16), band(v_ref, i), preferred_element_type=F32) / den
        for g in range(GROUP):
            o_ref[g, block_rows(i), :] = o[g * ATTN_BLOCK:(g + 1) * ATTN_BLOCK].astype(o_ref.dtype)

    for first in range(0, n_blocks, ATTN_BLOCKS_PER_GROUP):
        group = range(first, first + ATTN_BLOCKS_PER_GROUP)
        logits = [scores(i) for i in group]
        for i, sc in zip(group, logits):
            finish(i, sc)


def _attention(p16, sink, rel_table, batch, seq):
    q_blk0 = 3 * HGRN_HEADS // GROUP
    k_slab0 = 3 * HGRN_HEADS + ATTN_HEADS
    v_slab0 = k_slab0 + KV_HEADS
    bucket = jnp.asarray(_t5_bucket_table())
    smem = pl.BlockSpec(memory_space=pltpu.SMEM)
    return pl.pallas_call(
        _attn_kernel,
        grid=(batch, KV_HEADS),
        in_specs=[
            smem,
            smem,
            pl.BlockSpec((ATTN_BLOCK, KEY_SPAN), lambda b, x: (0, 0)),
            pl.BlockSpec((GROUP, seq, HEAD_DIM), lambda b, x: (q_blk0 + x, b, 0)),
            pl.BlockSpec((None, seq, HEAD_DIM), lambda b, x: (k_slab0 + x, b, 0)),
            pl.BlockSpec((None, seq, HEAD_DIM), lambda b, x: (v_slab0 + x, b, 0)),
        ],
        out_specs=pl.BlockSpec((GROUP, seq, HEAD_DIM), lambda b, x: (x, b, 0)),
        out_shape=jax.ShapeDtypeStruct((ATTN_HEADS, batch * seq, HEAD_DIM), BF16),
        scratch_shapes=[
            pltpu.VMEM((ATTN_HEADS * ATTN_BLOCK, KEY_SPAN), F32),
        ],
        compiler_params=pltpu.CompilerParams(
            dimension_semantics=("arbitrary", "arbitrary"), vmem_limit_bytes=VMEM_LIMIT_BYTES),
        name="attention",
    )(rel_table, sink, bucket, p16, p16, p16)


def _outproj_kernel(yh_ref, ya_ref, x_ref, wh_ref, wa_ref, g_ref, o_ref):
    y_h = jnp.concatenate([yh_ref[c] for c in range(yh_ref.shape[0])], axis=1)
    y_a = jnp.concatenate([ya_ref[c] for c in range(ya_ref.shape[0])], axis=1)
    mixed = jnp.dot(y_h, wh_ref[...], preferred_element_type=F32)
    mixed = mixed + jnp.dot(y_a, wa_ref[...], preferred_element_type=F32)
    o_ref[...] = x_ref[...] + _rms(mixed, g_ref[...])


def _outproj(y_h, y_a, x, w_out, gain, *, tm=512):
    m, d = x.shape
    return pl.pallas_call(
        _outproj_kernel,
        grid=(m // tm,),
        in_specs=[
            pl.BlockSpec((HGRN_HEADS, tm, HEAD_DIM), lambda i: (0, i, 0)),
            pl.BlockSpec((ATTN_HEADS, tm, HEAD_DIM), lambda i: (0, i, 0)),
            pl.BlockSpec((tm, d), lambda i: (i, 0)),
            pl.BlockSpec((HGRN_WIDTH, d), lambda i: (0, 0)),
            pl.BlockSpec((ATTN_WIDTH, d), lambda i: (1, 0)),
            pl.BlockSpec((1, d), lambda i: (0, 0)),
        ],
        out_specs=pl.BlockSpec((tm, d), lambda i: (i, 0)),
        out_shape=jax.ShapeDtypeStruct((m, d), F32),
        compiler_params=pltpu.CompilerParams(
            dimension_semantics=("parallel",), vmem_limit_bytes=VMEM_LIMIT_BYTES),
        name="outproj",
    )(y_h, y_a, x, w_out, w_out, gain)


def kernel(x, pre_norm_ffn1, post_norm_ffn1, w_ffn1_gate_up, w_ffn1_down, pre_norm_mix, post_norm_mix,
           w_mix_in, hgrn_lower_bounds_fwd, hgrn_lower_bounds_bwd, hgrn_out_norm, attn_sink, w_mix_out,
           pre_norm_ffn2, post_norm_ffn2, w_ffn2_gate_up, w_ffn2_down, rel_bias_table):
    batch, seq, d = x.shape
    depth = pre_norm_ffn1.shape[0]
    assert depth == 1 and d == D_MODEL
    xf = x.reshape(batch * seq, d)
    layer = 0
    xf, xn = _ffn(xf, pre_norm_ffn1[layer:layer + 1], post_norm_ffn1[layer:layer + 1],
                  w_ffn1_gate_up[layer], w_ffn1_down[layer], gnext=pre_norm_mix[layer:layer + 1])
    w_in = w_mix_in[layer]
    p16 = _proj(xn, w_in, [0, 1, 2, 3, 8, 9, 10, 11, 12], BF16)
    p_f = _proj(xn, w_in, [4, 5, 6, 7], F32)
    y_h, w2_gate_up, w2_down = _hgrn(p16, p_f, hgrn_lower_bounds_fwd, hgrn_lower_bounds_bwd,
                                     hgrn_out_norm[layer:layer + 1], w_ffn2_gate_up[layer], w_ffn2_down[layer],
                                     batch, seq)
    y_a = _attention(p16, attn_sink[layer:layer + 1], rel_bias_table, batch, seq)
    xf = _outproj(y_h, y_a, xf, w_mix_out[layer].astype(BF16), post_norm_mix[layer:layer + 1])
    xf = _ffn(xf, pre_norm_ffn2[layer:layer + 1], post_norm_ffn2[layer:layer + 1], w2_gate_up, w2_down, tf=512)
    return xf.reshape(batch, seq, d)
```

```python
import functools
import math

import jax
import jax.numpy as jnp
import numpy as np
from jax import lax
from jax.experimental import pallas as pl
from jax.experimental.pallas import tpu as pltpu

F32 = jnp.float32
BF16 = jnp.bfloat16

D_MODEL = 2048
HGRN_WIDTH = 1024
HEAD_DIM = 128
HGRN_HEADS = HGRN_WIDTH // HEAD_DIM
CHUNK = 64
ATTN_WIDTH = 1024
ATTN_HEADS = ATTN_WIDTH // HEAD_DIM
KV_HEADS = 2
GROUP = ATTN_HEADS // KV_HEADS
KV_WIDTH = KV_HEADS * HEAD_DIM
WINDOW = 128
ATTN_BLOCK = 128
KEY_SPAN = ATTN_BLOCK + 2 * WINDOW
REL_BUCKETS = 32
REL_MAX_DIST = 128
D_FF = 5632
EPS = 1e-6
NEG_INF = -1e30
LOG2_E = 1.0 / math.log(2.0)

MIB = 1024 * 1024
VMEM_LIMIT_BYTES = 56 * MIB
ROW_BLOCK = 32
PAIRS_PER_TRIP = 16
CHUNKS_PER_OUT_TRIP = 16
ATTN_BLOCKS_PER_GROUP = 2


def _rms(x, gain):
    return x * lax.rsqrt(jnp.mean(x * x, axis=-1, keepdims=True) + EPS) * gain


def _sigmoid(x):
    return 1.0 / (1.0 + jnp.exp(-x))


def _for_row_blocks(n_rows, fn):
    def body(r, carry):
        fn(pl.ds(pl.multiple_of(r * ROW_BLOCK, ROW_BLOCK), ROW_BLOCK))
        return carry

    lax.fori_loop(0, n_rows // ROW_BLOCK, body, 0, unroll=4)


def _ffn_kernel(*refs, emit_next_norm):
    if emit_next_norm:
        x_ref, gpre_ref, gpost_ref, gnext_ref, wg_ref, wu_ref, wd_ref, o_ref, h_ref = refs
    else:
        x_ref, gpre_ref, gpost_ref, wg_ref, wu_ref, wd_ref, o_ref, h_ref = refs
    j = pl.program_id(1)
    tm = x_ref.shape[0]

    @pl.when(j == 0)
    def _():
        def prologue(rows):
            h_ref[rows, :] = _rms(x_ref[rows, :], gpre_ref[...]).astype(BF16)
            o_ref[rows, :] = jnp.zeros((ROW_BLOCK, o_ref.shape[1]), F32)

        _for_row_blocks(tm, prologue)

    h = h_ref[...]
    gate = jnp.dot(h, wg_ref[...].astype(BF16), preferred_element_type=F32)
    up = jnp.dot(h, wu_ref[...].astype(BF16), preferred_element_type=F32)
    act = (gate * _sigmoid(gate) * up).astype(BF16)
    o_ref[...] += jnp.dot(act, wd_ref[...].astype(BF16), preferred_element_type=F32)

    @pl.when(j == pl.num_programs(1) - 1)
    def _():
        for r in range(0, tm, ROW_BLOCK):
            rows = pl.ds(r, ROW_BLOCK)
            new_x = x_ref[rows, :] + 0.5 * _rms(o_ref[rows, :], gpost_ref[...])
            o_ref[rows, :] = new_x
            if emit_next_norm:
                h_ref[rows, :] = _rms(new_x, gnext_ref[...]).astype(BF16)


def _ffn(x, gpre, gpost, w_gate_up, w_down, gnext=None, *, tm=1024, tf=256):
    m, d = x.shape
    nj = D_FF // tf
    emit = gnext is not None
    row_tile = pl.BlockSpec((tm, d), lambda i, j: (i, 0))
    gain_spec = pl.BlockSpec((1, d), lambda i, j: (0, 0))
    weight_specs = [
        pl.BlockSpec((d, tf), lambda i, j: (0, j)),
        pl.BlockSpec((d, tf), lambda i, j: (0, j + nj)),
        pl.BlockSpec((tf, d), lambda i, j: (j, 0)),
    ]
    gains = (gpre, gpost, gnext) if emit else (gpre, gpost)
    return pl.pallas_call(
        functools.partial(_ffn_kernel, emit_next_norm=emit),
        grid=(m // tm, nj),
        in_specs=[row_tile] + [gain_spec] * len(gains) + weight_specs,
        out_specs=(row_tile, row_tile) if emit else row_tile,
        out_shape=((jax.ShapeDtypeStruct((m, d), F32), jax.ShapeDtypeStruct((m, d), BF16)) if emit
                   else jax.ShapeDtypeStruct((m, d), F32)),
        scratch_shapes=[] if emit else [pltpu.VMEM((tm, d), BF16)],
        compiler_params=pltpu.CompilerParams(
            dimension_semantics=("parallel", "arbitrary"), vmem_limit_bytes=VMEM_LIMIT_BYTES),
        name="ffn",
    )(x, *gains, w_gate_up, w_gate_up, w_down)


def _proj_kernel(xn_ref, w_ref, o_ref):
    res = jnp.dot(xn_ref[...], w_ref[...].astype(BF16), preferred_element_type=F32).astype(o_ref.dtype)
    for c in range(o_ref.shape[0]):
        o_ref[c] = res[:, c * HEAD_DIM:(c + 1) * HEAD_DIM]


def _proj(xn, w_in, col_blocks, out_dtype, *, tm=2048, tn=512):
    m, d = xn.shape
    per = tn // HEAD_DIM
    jumps = [(pos, col_blocks[pos] - col_blocks[pos - 1] - 1) for pos in range(1, len(col_blocks))
             if col_blocks[pos] != col_blocks[pos - 1] + 1]

    def w_block(i, j):
        blk = j + col_blocks[0]
        for pos, gap in jumps:
            blk = blk + jnp.where(j >= pos, gap, 0)
        return (0, blk)

    return pl.pallas_call(
        _proj_kernel,
        grid=(m // tm, len(col_blocks)),
        in_specs=[
            pl.BlockSpec((tm, d), lambda i, j: (i, 0)),
            pl.BlockSpec((d, tn), w_block),
        ],
        out_specs=pl.BlockSpec((per, tm, HEAD_DIM), lambda i, j: (j, i, 0)),
        out_shape=jax.ShapeDtypeStruct((len(col_blocks) * per, m, HEAD_DIM), out_dtype),
        compiler_params=pltpu.CompilerParams(
            dimension_semantics=("parallel", "arbitrary"), vmem_limit_bytes=VMEM_LIMIT_BYTES),
        name="inproj",
    )(xn, w_in)


def _chunk_cumsum(x, ones_tri):
    n = x.shape[1]
    hi = x.astype(BF16)
    rest = x - hi.astype(F32)
    mid = rest.astype(BF16)
    lo = (rest - mid.astype(F32)).astype(BF16)
    sums = jnp.dot(ones_tri, jnp.concatenate([hi, mid, lo], axis=1), preferred_element_type=F32)
    return sums[:, :n] + sums[:, n:2 * n] + sums[:, 2 * n:]


def _hgrn_kernel(q_ref, v_ref, ff_ref, fb_ref, g_ref, lbf_ref, lbb_ref, gain_ref, wa_ref, wb_ref,
                 o_ref, wa16_ref, wb16_ref, qd_ref, kv_ref, dec_ref, st_ref, oi_ref):
    wa16_ref[...] = wa_ref[...].astype(BF16)
    wb16_ref[...] = wb_ref[...].astype(BF16)

    seq = q_ref.shape[0]
    n_chunks = seq // CHUNK
    pair = 2 * CHUNK
    contract_last = (((1,), (1,)), ((), ()))

    def gate_consts(lb_param_ref):
        a = lb_param_ref[...]
        e = jnp.exp(a - jnp.max(a, axis=0, keepdims=True))
        lb = e[0:1] / jnp.sum(e, axis=0, keepdims=True)
        return 0.5 * (1.0 + lb), 0.5 * (1.0 - lb)

    consts = (gate_consts(lbf_ref), gate_consts(lbb_ref))
    logit_refs = (ff_ref, fb_ref)

    r = lax.broadcasted_iota(jnp.int32, (pair, pair), 0)
    s = lax.broadcasted_iota(jnp.int32, (pair, pair), 1)
    same_chunk = (r // CHUNK) == (s // CHUNK)
    tri = (same_chunk & (r >= s), same_chunk & (s >= r))
    ones_tri = tuple(jnp.where(m, 1.0, 0.0).astype(BF16) for m in tri)
    zero_blk = jnp.zeros((CHUNK, HEAD_DIM), BF16)

    def trip_body(trip, carry):
        pairs = [trip * PAIRS_PER_TRIP + i for i in range(PAIRS_PER_TRIP)]
        rows_of = [pl.ds(pl.multiple_of(pi * pair, pair), pair) for pi in pairs]

        gates = []
        for rows in rows_of:
            per_dir = []
            for d in range(2):
                mid, half = consts[d]
                t = jnp.tanh(0.5 * logit_refs[d][rows, :])
                f = mid + half * t
                k = half * (1.0 - t)
                per_dir.append((k, _chunk_cumsum(jnp.log(f) * LOG2_E, ones_tri[d])))
            gates.append(per_dir)

        mixes = []
        for pi, rows, per_dir in zip(pairs, rows_of, gates):
            q = q_ref[rows, :].astype(F32)
            vb = v_ref[rows, :]
            v_t = vb.astype(F32).T.astype(BF16)
            probs = None
            for d, (k, c) in enumerate(per_dir):
                c3 = c.reshape(2, CHUNK, HEAD_DIM)
                tot = c3[:, 0:1, :] if d == 1 else c3[:, CHUNK - 1:CHUNK, :]
                k_tail = (k * jnp.exp2(tot - c3).reshape(pair, HEAD_DIM)).astype(BF16)
                q_dec = (q * jnp.exp2(c)).astype(BF16)
                k_dec = (k * jnp.exp2(-c)).astype(BF16)
                qd_ref[rows, d * HEAD_DIM:(d + 1) * HEAD_DIM] = q_dec
                dec_ref[d, pl.ds(pi * 2, 2), :] = jnp.exp2(tot).reshape(2, HEAD_DIM)
                sc = lax.dot_general(q_dec, k_dec, contract_last, preferred_element_type=F32)
                sc = jnp.where(tri[d], sc, 0.0)
                probs = sc if probs is None else probs + sc
                rhs = jnp.concatenate(
                    [jnp.concatenate([k_tail[:CHUNK], zero_blk], axis=1),
                     jnp.concatenate([zero_blk, k_tail[CHUNK:]], axis=1)], axis=0)
                kv = jnp.dot(v_t, rhs, preferred_element_type=F32)
                kv_ref[d, pi * 2] = kv[:, :HEAD_DIM]
                kv_ref[d, pi * 2 + 1] = kv[:, HEAD_DIM:]
            mixes.append((probs.astype(BF16), vb))

        for rows, (probs, vb) in zip(rows_of, mixes):
            oi_ref[rows, :] = jnp.dot(probs, vb, preferred_element_type=F32)
        return carry

    lax.fori_loop(0, n_chunks // (2 * PAIRS_PER_TRIP), trip_body, 0)

    def scan_body(t, carry):
        sf, sb = carry
        nf, nb = t, n_chunks - 1 - t
        st_ref[nf, :, :HEAD_DIM] = sf.astype(BF16)
        st_ref[nb, :, HEAD_DIM:] = sb.astype(BF16)
        sf = sf * dec_ref[0, pl.ds(nf, 1), :] + kv_ref[0, nf]
        sb = sb * dec_ref[1, pl.ds(nb, 1), :] + kv_ref[1, nb]
        return sf, sb

    zero = jnp.zeros((HEAD_DIM, HEAD_DIM), F32)
    lax.fori_loop(0, n_chunks, scan_body, (zero, zero), unroll=2)

    def out_body(trip, carry):
        chunks = [trip * CHUNKS_PER_OUT_TRIP + i for i in range(CHUNKS_PER_OUT_TRIP)]
        rows_of = [pl.ds(pl.multiple_of(n * CHUNK, CHUNK), CHUNK) for n in chunks]
        inter = [lax.dot_general(qd_ref[rows, :], st_ref[n], contract_last, preferred_element_type=F32)
                 for n, rows in zip(chunks, rows_of)]
        for rows, o_inter in zip(rows_of, inter):
            o = _rms(oi_ref[rows, :] + o_inter, gain_ref[...])
            g = g_ref[rows, :].astype(F32)
            o_ref[rows, :] = (o * (0.5 * g * (1.0 + jnp.tanh(0.5 * g)))).astype(o_ref.dtype)
        return carry

    lax.fori_loop(0, n_chunks // CHUNKS_PER_OUT_TRIP, out_body, 0)


def _hgrn(p16, p_f, lbf, lbb, out_gain, w_a, w_b, batch, seq):
    hw = HGRN_HEADS
    steps = batch * hw

    def slab(group):
        return lambda b, h: (group * hw + h, b, 0)

    def row_slab(w):
        return pl.BlockSpec((w.shape[0] // steps, w.shape[1]), lambda b, h: (b * hw + h, 0))

    blk = (None, seq, HEAD_DIM)
    return pl.pallas_call(
        _hgrn_kernel,
        grid=(batch, hw),
        in_specs=[
            pl.BlockSpec(blk, slab(0)),
            pl.BlockSpec(blk, slab(1)),
            pl.BlockSpec(blk, slab(0)),
            pl.BlockSpec(blk, slab(1)),
            pl.BlockSpec(blk, slab(2)),
            pl.BlockSpec((lbf.shape[0], HEAD_DIM), lambda b, h: (0, h)),
            pl.BlockSpec((lbb.shape[0], HEAD_DIM), lambda b, h: (0, h)),
            pl.BlockSpec((1, HEAD_DIM), lambda b, h: (0, h)),
            row_slab(w_a),
            row_slab(w_b),
        ],
        out_specs=(pl.BlockSpec(blk, lambda b, h: (h, b, 0)), row_slab(w_a), row_slab(w_b)),
        out_shape=(jax.ShapeDtypeStruct((hw, batch * seq, HEAD_DIM), BF16),
                   jax.ShapeDtypeStruct(w_a.shape, BF16), jax.ShapeDtypeStruct(w_b.shape, BF16)),
        scratch_shapes=[
            pltpu.VMEM((seq, 2 * HEAD_DIM), BF16),
            pltpu.VMEM((2, seq // CHUNK, HEAD_DIM, HEAD_DIM), F32),
            pltpu.VMEM((2, seq // CHUNK, HEAD_DIM), F32),
            pltpu.VMEM((seq // CHUNK, HEAD_DIM, 2 * HEAD_DIM), BF16),
            pltpu.VMEM((seq, HEAD_DIM), F32),
        ],
        compiler_params=pltpu.CompilerParams(
            dimension_semantics=("parallel", "parallel"), vmem_limit_bytes=VMEM_LIMIT_BYTES),
        name="hgrn",
    )(p16, p16, p_f, p_f, p16, lbf, lbb, out_gain, w_a, w_b)


def _t5_bucket_table():
    nb = REL_BUCKETS // 2
    max_exact = nb // 2
    c = np.arange(ATTN_BLOCK)[:, None]
    s = np.arange(KEY_SPAN)[None, :]
    rel = s - WINDOW - c
    bucket = (rel > 0).astype(np.int32) * nb
    n = np.abs(rel)
    large = max_exact + (np.log(np.maximum(n, 1) / max_exact) / np.log(REL_MAX_DIST / max_exact)
                         * (nb - max_exact)).astype(np.int32)
    large = np.minimum(large, nb - 1)
    bucket = bucket + np.where(n < max_exact, n, large).astype(np.int32)
    return np.where(np.abs(rel) <= WINDOW, bucket, -1).astype(np.int32)


def _attn_kernel(table_ref, sink_ref, bucket_ref, q_ref, k_ref, v_ref, w_ref, o_ref, w16_ref, bias_ref):
    w16_ref[...] = w_ref[...].astype(BF16)

    x = pl.program_id(1)
    seq = k_ref.shape[0]
    n_blocks = seq // ATTN_BLOCK
    scale = 1.0 / math.sqrt(HEAD_DIM)
    rows = GROUP * ATTN_BLOCK

    @pl.when((pl.program_id(0) == 0) & (x == 0))
    def _():
        bucket = bucket_ref[...]
        for head in range(ATTN_HEADS):
            bias = jnp.full((ATTN_BLOCK, KEY_SPAN), NEG_INF, F32)
            for b in range(REL_BUCKETS):
                bias = jnp.where(bucket == b, table_ref[b, head], bias)
            bias_ref[pl.ds(head * ATTN_BLOCK, ATTN_BLOCK), :] = bias

    row_head = lax.broadcasted_iota(jnp.int32, (rows, 1), 0) // ATTN_BLOCK
    sink = jnp.zeros((rows, 1), F32)
    for g in range(GROUP):
        sink = jnp.where(row_head == g, sink_ref[0, x * GROUP + g], sink)
    bias_rows = pl.ds(pl.multiple_of(x * rows, rows), rows)

    def block_rows(i):
        return pl.ds(i * ATTN_BLOCK, ATTN_BLOCK)

    def band(ref, i):
        return jnp.concatenate([ref[block_rows(max(i - 1, 0)), :], ref[block_rows(i), :],
                                ref[block_rows(min(i + 1, n_blocks - 1)), :]], axis=0)

    def scores(i):
        qs = jnp.concatenate([q_ref[g, block_rows(i), :] for g in range(GROUP)], axis=0)
        sc = lax.dot_general(qs, band(k_ref, i), (((1,), (1,)), ((), ())), preferred_element_type=F32)
        sc = sc * scale + bias_ref[bias_rows, :]
        if i in (0, n_blocks - 1):
            key_pos = (i - 1) * ATTN_BLOCK + lax.broadcasted_iota(jnp.int32, (rows, KEY_SPAN), 1)
            sc = jnp.where((key_pos >= 0) & (key_pos < seq), sc, NEG_INF)
        return sc

    def finish(i, sc):
        m = jnp.maximum(jnp.max(sc, axis=-1, keepdims=True), sink)
        pr = jnp.exp(sc - m)
        den = jnp.sum(pr, axis=-1, keepdims=True) + jnp.exp(sink - m)
        o = jnp.dot(pr.astype(BF16), band(v_ref, i), preferred_element_type=F32) / den
        for g in range(GROUP):
            o_ref[g, block_rows(i), :] = o[g * ATTN_BLOCK:(g + 1) * ATTN_BLOCK].astype(o_ref.dtype)

    for first in range(0, n_blocks, ATTN_BLOCKS_PER_GROUP):
        group = range(first, first + ATTN_BLOCKS_PER_GROUP)
        logits = [scores(i) for i in group]
        for i, sc in zip(group, logits):
            finish(i, sc)


def _attention(p16, sink, rel_table, w, batch, seq):
    q_blk0 = 3 * HGRN_HEADS // GROUP
    k_slab0 = 3 * HGRN_HEADS + ATTN_HEADS
    v_slab0 = k_slab0 + KV_HEADS
    bucket = jnp.asarray(_t5_bucket_table())
    smem = pl.BlockSpec(memory_space=pltpu.SMEM)
    w_slab = pl.BlockSpec((w.shape[0] // (batch * KV_HEADS), w.shape[1]), lambda b, x: (b * KV_HEADS + x, 0))
    return pl.pallas_call(
        _attn_kernel,
        grid=(batch, KV_HEADS),
        in_specs=[
            smem,
            smem,
            pl.BlockSpec((ATTN_BLOCK, KEY_SPAN), lambda b, x: (0, 0)),
            pl.BlockSpec((GROUP, seq, HEAD_DIM), lambda b, x: (q_blk0 + x, b, 0)),
            pl.BlockSpec((None, seq, HEAD_DIM), lambda b, x: (k_slab0 + x, b, 0)),
            pl.BlockSpec((None, seq, HEAD_DIM), lambda b, x: (v_slab0 + x, b, 0)),
            w_slab,
        ],
        out_specs=(pl.BlockSpec((GROUP, seq, HEAD_DIM), lambda b, x: (x, b, 0)), w_slab),
        out_shape=(jax.ShapeDtypeStruct((ATTN_HEADS, batch * seq, HEAD_DIM), BF16),
                   jax.ShapeDtypeStruct(w.shape, BF16)),
        scratch_shapes=[
            pltpu.VMEM((ATTN_HEADS * ATTN_BLOCK, KEY_SPAN), F32),
        ],
        compiler_params=pltpu.CompilerParams(
            dimension_semantics=("arbitrary", "arbitrary"), vmem_limit_bytes=VMEM_LIMIT_BYTES),
        name="attention",
    )(rel_table, sink, bucket, p16, p16, p16, w)


def _outproj_kernel(yh_ref, ya_ref, x_ref, wh_ref, wa_ref, g_ref, o_ref):
    y_h = jnp.concatenate([yh_ref[c] for c in range(yh_ref.shape[0])], axis=1)
    y_a = jnp.concatenate([ya_ref[c] for c in range(ya_ref.shape[0])], axis=1)
    mixed = jnp.dot(y_h, wh_ref[...], preferred_element_type=F32)
    mixed = mixed + jnp.dot(y_a, wa_ref[...], preferred_element_type=F32)
    o_ref[...] = x_ref[...] + _rms(mixed, g_ref[...])


def _outproj(y_h, y_a, x, w_out, gain, *, tm=512):
    m, d = x.shape
    return pl.pallas_call(
        _outproj_kernel,
        grid=(m // tm,),
        in_specs=[
            pl.BlockSpec((HGRN_HEADS, tm, HEAD_DIM), lambda i: (0, i, 0)),
            pl.BlockSpec((ATTN_HEADS, tm, HEAD_DIM), lambda i: (0, i, 0)),
            pl.BlockSpec((tm, d), lambda i: (i, 0)),
            pl.BlockSpec((HGRN_WIDTH, d), lambda i: (0, 0)),
            pl.BlockSpec((ATTN_WIDTH, d), lambda i: (1, 0)),
            pl.BlockSpec((1, d), lambda i: (0, 0)),
        ],
        out_specs=pl.BlockSpec((tm, d), lambda i: (i, 0)),
        out_shape=jax.ShapeDtypeStruct((m, d), F32),
        compiler_params=pltpu.CompilerParams(
            dimension_semantics=("parallel",), vmem_limit_bytes=VMEM_LIMIT_BYTES),
        name="outproj",
    )(y_h, y_a, x, w_out, w_out, gain)


def kernel(x, pre_norm_ffn1, post_norm_ffn1, w_ffn1_gate_up, w_ffn1_down, pre_norm_mix, post_norm_mix,
           w_mix_in, hgrn_lower_bounds_fwd, hgrn_lower_bounds_bwd, hgrn_out_norm, attn_sink, w_mix_out,
           pre_norm_ffn2, post_norm_ffn2, w_ffn2_gate_up, w_ffn2_down, rel_bias_table):
    batch, seq, d = x.shape
    depth = pre_norm_ffn1.shape[0]
    assert depth == 1 and d == D_MODEL
    xf = x.reshape(batch * seq, d)
    layer = 0
    xf, xn = _ffn(xf, pre_norm_ffn1[layer:layer + 1], post_norm_ffn1[layer:layer + 1],
                  w_ffn1_gate_up[layer], w_ffn1_down[layer], gnext=pre_norm_mix[layer:layer + 1])
    w_in = w_mix_in[layer]
    p16 = _proj(xn, w_in, [0, 1, 2, 3, 8, 9, 10, 11, 12], BF16, tm=4096)
    p_f = _proj(xn, w_in, [4, 5, 6, 7], F32)
    y_h, w2_gate_up, w2_down = _hgrn(p16, p_f, hgrn_lower_bounds_fwd, hgrn_lower_bounds_bwd,
                                     hgrn_out_norm[layer:layer + 1], w_ffn2_gate_up[layer], w_ffn2_down[layer],
                                     batch, seq)
    y_a, w_out = _attention(p16, attn_sink[layer:layer + 1], rel_bias_table, w_mix_out[layer], batch, seq)
    xf = _outproj(y_h, y_a, xf, w_out, post_norm_mix[layer:layer + 1])
    xf = _ffn(xf, pre_norm_ffn2[layer:layer + 1], post_norm_ffn2[layer:layer + 1], w2_gate_up, w2_down, tf=512)
    return xf.reshape(batch, seq, d)
```

```python
import functools
import math

import jax
import jax.numpy as jnp
import numpy as np
from jax import lax
from jax.experimental import pallas as pl
from jax.experimental.pallas import tpu as pltpu

F32 = jnp.float32
BF16 = jnp.bfloat16

D_MODEL = 2048
HGRN_WIDTH = 1024
HEAD_DIM = 128
HGRN_HEADS = HGRN_WIDTH // HEAD_DIM
CHUNK = 64
ATTN_WIDTH = 1024
ATTN_HEADS = ATTN_WIDTH // HEAD_DIM
KV_HEADS = 2
GROUP = ATTN_HEADS // KV_HEADS
KV_WIDTH = KV_HEADS * HEAD_DIM
WINDOW = 128
ATTN_BLOCK = 128
KEY_SPAN = ATTN_BLOCK + 2 * WINDOW
REL_BUCKETS = 32
REL_MAX_DIST = 128
D_FF = 5632
EPS = 1e-6
NEG_INF = -1e30
LOG2_E = 1.0 / math.log(2.0)

MIB = 1024 * 1024
VMEM_LIMIT_BYTES = 56 * MIB
ROW_BLOCK = 32
PAIRS_PER_TRIP = 16
CHUNKS_PER_OUT_TRIP = 16
LOGIT_SLAB = 256
ATTN_BLOCKS_PER_GROUP = 2


def _rms(x, gain):
    return x * lax.rsqrt(jnp.mean(x * x, axis=-1, keepdims=True) + EPS) * gain


def _sigmoid(x):
    return 1.0 / (1.0 + jnp.exp(-x))


def _for_row_blocks(n_rows, fn):
    def body(r, carry):
        fn(pl.ds(pl.multiple_of(r * ROW_BLOCK, ROW_BLOCK), ROW_BLOCK))
        return carry

    lax.fori_loop(0, n_rows // ROW_BLOCK, body, 0, unroll=4)


def _ffn_kernel(*refs, emit_next_norm):
    if emit_next_norm:
        x_ref, gpre_ref, gpost_ref, gnext_ref, wg_ref, wu_ref, wd_ref, o_ref, h_ref = refs
    else:
        x_ref, gpre_ref, gpost_ref, wg_ref, wu_ref, wd_ref, o_ref, h_ref = refs
    j = pl.program_id(1)
    tm = x_ref.shape[0]

    @pl.when(j == 0)
    def _():
        def prologue(rows):
            h_ref[rows, :] = _rms(x_ref[rows, :], gpre_ref[...]).astype(BF16)
            o_ref[rows, :] = jnp.zeros((ROW_BLOCK, o_ref.shape[1]), F32)

        _for_row_blocks(tm, prologue)

    h = h_ref[...]
    gate = jnp.dot(h, wg_ref[...].astype(BF16), preferred_element_type=F32)
    up = jnp.dot(h, wu_ref[...].astype(BF16), preferred_element_type=F32)
    act = (gate * _sigmoid(gate) * up).astype(BF16)
    o_ref[...] += jnp.dot(act, wd_ref[...].astype(BF16), preferred_element_type=F32)

    @pl.when(j == pl.num_programs(1) - 1)
    def _():
        for r in range(0, tm, ROW_BLOCK):
            rows = pl.ds(r, ROW_BLOCK)
            new_x = x_ref[rows, :] + 0.5 * _rms(o_ref[rows, :], gpost_ref[...])
            o_ref[rows, :] = new_x
            if emit_next_norm:
                h_ref[rows, :] = _rms(new_x, gnext_ref[...]).astype(BF16)


def _ffn(x, gpre, gpost, w_gate_up, w_down, gnext=None, *, tm=1024, tf=256):
    m, d = x.shape
    nj = D_FF // tf
    emit = gnext is not None
    row_tile = pl.BlockSpec((tm, d), lambda i, j: (i, 0))
    gain_spec = pl.BlockSpec((1, d), lambda i, j: (0, 0))
    weight_specs = [
        pl.BlockSpec((d, tf), lambda i, j: (0, j)),
        pl.BlockSpec((d, tf), lambda i, j: (0, j + nj)),
        pl.BlockSpec((tf, d), lambda i, j: (j, 0)),
    ]
    gains = (gpre, gpost, gnext) if emit else (gpre, gpost)
    return pl.pallas_call(
        functools.partial(_ffn_kernel, emit_next_norm=emit),
        grid=(m // tm, nj),
        in_specs=[row_tile] + [gain_spec] * len(gains) + weight_specs,
        out_specs=(row_tile, row_tile) if emit else row_tile,
        out_shape=((jax.ShapeDtypeStruct((m, d), F32), jax.ShapeDtypeStruct((m, d), BF16)) if emit
                   else jax.ShapeDtypeStruct((m, d), F32)),
        scratch_shapes=[] if emit else [pltpu.VMEM((tm, d), BF16)],
        compiler_params=pltpu.CompilerParams(
            dimension_semantics=("parallel", "arbitrary"), vmem_limit_bytes=VMEM_LIMIT_BYTES),
        name="ffn",
    )(x, *gains, w_gate_up, w_gate_up, w_down)


def _proj_kernel(xn_ref, w_ref, o_ref):
    res = jnp.dot(xn_ref[...], w_ref[...].astype(BF16), preferred_element_type=F32).astype(o_ref.dtype)
    for c in range(o_ref.shape[0]):
        o_ref[c] = res[:, c * HEAD_DIM:(c + 1) * HEAD_DIM]


def _proj(xn, w_in, col_blocks, out_dtype, *, tm=2048, tn=512):
    m, d = xn.shape
    per = tn // HEAD_DIM
    jumps = [(pos, col_blocks[pos] - col_blocks[pos - 1] - 1) for pos in range(1, len(col_blocks))
             if col_blocks[pos] != col_blocks[pos - 1] + 1]

    def w_block(i, j):
        blk = j + col_blocks[0]
        for pos, gap in jumps:
            blk = blk + jnp.where(j >= pos, gap, 0)
        return (0, blk)

    return pl.pallas_call(
        _proj_kernel,
        grid=(m // tm, len(col_blocks)),
        in_specs=[
            pl.BlockSpec((tm, d), lambda i, j: (i, 0)),
            pl.BlockSpec((d, tn), w_block),
        ],
        out_specs=pl.BlockSpec((per, tm, HEAD_DIM), lambda i, j: (j, i, 0)),
        out_shape=jax.ShapeDtypeStruct((len(col_blocks) * per, m, HEAD_DIM), out_dtype),
        compiler_params=pltpu.CompilerParams(
            dimension_semantics=("parallel", "arbitrary"), vmem_limit_bytes=VMEM_LIMIT_BYTES),
        name="inproj",
    )(xn, w_in)


def _chunk_cumsum(x, row_in_chunk, reverse):
    n_rows = x.shape[0]
    c = x
    s = 1
    while s < CHUNK:
        if reverse:
            shifted = pltpu.roll(c, n_rows - s, axis=0)
            keep = row_in_chunk < CHUNK - s
        else:
            shifted = pltpu.roll(c, s, axis=0)
            keep = row_in_chunk >= s
        c = c + jnp.where(keep, shifted, 0.0)
        s *= 2
    return c


def _hgrn_kernel(q_ref, v_ref, xn_ref, wff_ref, wfb_ref, g_ref, lbf_ref, lbb_ref, gain_ref, wa_ref, wb_ref,
                 o_ref, wa16_ref, wb16_ref, qd_ref, kv_ref, dec_ref, st_ref, oi_ref):
    wa16_ref[...] = wa_ref[...].astype(BF16)
    wb16_ref[...] = wb_ref[...].astype(BF16)

    seq = q_ref.shape[0]
    n_chunks = seq // CHUNK
    pair = 2 * CHUNK
    contract_last = (((1,), (1,)), ((), ()))

    def gate_consts(lb_param_ref):
        a = lb_param_ref[...]
        e = jnp.exp(a - jnp.max(a, axis=0, keepdims=True))
        lb = e[0:1] / jnp.sum(e, axis=0, keepdims=True)
        return 0.5 * (1.0 + lb), 0.5 * (1.0 - lb)

    consts = (gate_consts(lbf_ref), gate_consts(lbb_ref))
    w_logits = jnp.concatenate([wff_ref[...].astype(BF16), wfb_ref[...].astype(BF16)], axis=1)

    r = lax.broadcasted_iota(jnp.int32, (pair, pair), 0)
    s = lax.broadcasted_iota(jnp.int32, (pair, pair), 1)
    same_chunk = (r // CHUNK) == (s // CHUNK)
    tri = (same_chunk & (r >= s), same_chunk & (s >= r))
    row_in_chunk = lax.broadcasted_iota(jnp.int32, (pair, HEAD_DIM), 0) % CHUNK
    zero_blk = jnp.zeros((CHUNK, HEAD_DIM), BF16)

    def trip_body(trip, carry):
        pairs = [trip * PAIRS_PER_TRIP + i for i in range(PAIRS_PER_TRIP)]
        rows_of = [pl.ds(pl.multiple_of(pi * pair, pair), pair) for pi in pairs]

        pairs_per_slab = LOGIT_SLAB // pair
        n_slabs = PAIRS_PER_TRIP // pairs_per_slab

        def project_logits(slab):
            start = pl.multiple_of((trip * n_slabs + slab) * LOGIT_SLAB, LOGIT_SLAB)
            return jnp.dot(xn_ref[pl.ds(start, LOGIT_SLAB), :], w_logits, preferred_element_type=F32)

        logits = [project_logits(s) for s in range(min(2, n_slabs))]
        gates = []
        for i in range(PAIRS_PER_TRIP):
            slab, offset = divmod(i, pairs_per_slab)
            per_dir = []
            for d in range(2):
                mid, half = consts[d]
                z = logits[slab][offset * pair:(offset + 1) * pair, d * HEAD_DIM:(d + 1) * HEAD_DIM]
                t = jnp.tanh(0.5 * z)
                f = mid + half * t
                k = half * (1.0 - t)
                per_dir.append((k, _chunk_cumsum(jnp.log(f) * LOG2_E, row_in_chunk, reverse=(d == 1))))
            gates.append(per_dir)
            if offset == pairs_per_slab - 1 and slab + 2 < n_slabs:
                logits.append(project_logits(slab + 2))

        mixes = []
        for pi, rows, per_dir in zip(pairs, rows_of, gates):
            q = q_ref[rows, :].astype(F32)
            vb = v_ref[rows, :]
            v_t = vb.astype(F32).T.astype(BF16)
            probs = None
            for d, (k, c) in enumerate(per_dir):
                c3 = c.reshape(2, CHUNK, HEAD_DIM)
                tot = c3[:, 0:1, :] if d == 1 else c3[:, CHUNK - 1:CHUNK, :]
                k_tail = (k * jnp.exp2(tot - c3).reshape(pair, HEAD_DIM)).astype(BF16)
                q_dec = (q * jnp.exp2(c)).astype(BF16)
                k_dec = (k * jnp.exp2(-c)).astype(BF16)
                qd_ref[rows, d * HEAD_DIM:(d + 1) * HEAD_DIM] = q_dec
                dec_ref[d, pl.ds(pi * 2, 2), :] = jnp.exp2(tot).reshape(2, HEAD_DIM)
                sc = lax.dot_general(q_dec, k_dec, contract_last, preferred_element_type=F32)
                sc = jnp.where(tri[d], sc, 0.0)
                probs = sc if probs is None else probs + sc
                rhs = jnp.concatenate(
                    [jnp.concatenate([k_tail[:CHUNK], zero_blk], axis=1),
                     jnp.concatenate([zero_blk, k_tail[CHUNK:]], axis=1)], axis=0)
                kv = jnp.dot(v_t, rhs, preferred_element_type=F32)
                kv_ref[d, pi * 2] = kv[:, :HEAD_DIM]
                kv_ref[d, pi * 2 + 1] = kv[:, HEAD_DIM:]
            mixes.append((probs.astype(BF16), vb))

        for rows, (probs, vb) in zip(rows_of, mixes):
            oi_ref[rows, :] = jnp.dot(probs, vb, preferred_element_type=F32)
        return carry

    lax.fori_loop(0, n_chunks // (2 * PAIRS_PER_TRIP), trip_body, 0)

    def scan_body(t, carry):
        sf, sb = carry
        nf, nb = t, n_chunks - 1 - t
        st_ref[nf, :, :HEAD_DIM] = sf.astype(BF16)
        st_ref[nb, :, HEAD_DIM:] = sb.astype(BF16)
        sf = sf * dec_ref[0, pl.ds(nf, 1), :] + kv_ref[0, nf]
        sb = sb * dec_ref[1, pl.ds(nb, 1), :] + kv_ref[1, nb]
        return sf, sb

    zero = jnp.zeros((HEAD_DIM, HEAD_DIM), F32)
    lax.fori_loop(0, n_chunks, scan_body, (zero, zero), unroll=2)

    def out_body(trip, carry):
        chunks = [trip * CHUNKS_PER_OUT_TRIP + i for i in range(CHUNKS_PER_OUT_TRIP)]
        rows_of = [pl.ds(pl.multiple_of(n * CHUNK, CHUNK), CHUNK) for n in chunks]
        inter = [lax.dot_general(qd_ref[rows, :], st_ref[n], contract_last, preferred_element_type=F32)
                 for n, rows in zip(chunks, rows_of)]
        for rows, o_inter in zip(rows_of, inter):
            o = _rms(oi_ref[rows, :] + o_inter, gain_ref[...])
            g = g_ref[rows, :].astype(F32)
            o_ref[rows, :] = (o * (0.5 * g * (1.0 + jnp.tanh(0.5 * g)))).astype(o_ref.dtype)
        return carry

    lax.fori_loop(0, n_chunks // CHUNKS_PER_OUT_TRIP, out_body, 0)


def _hgrn(p16, xn, w_in, lbf, lbb, out_gain, w_a, w_b, batch, seq):
    hw = HGRN_HEADS
    steps = batch * hw
    d = xn.shape[1]

    def w_col(group):
        return lambda b, h: (0, group * hw + h)

    def slab(group):
        return lambda b, h: (group * hw + h, b, 0)

    def row_slab(w):
        return pl.BlockSpec((w.shape[0] // steps, w.shape[1]), lambda b, h: (b * hw + h, 0))

    blk = (None, seq, HEAD_DIM)
    return pl.pallas_call(
        _hgrn_kernel,
        grid=(batch, hw),
        in_specs=[
            pl.BlockSpec(blk, slab(0)),
            pl.BlockSpec(blk, slab(1)),
            pl.BlockSpec((seq, d), lambda b, h: (b, 0)),
            pl.BlockSpec((d, HEAD_DIM), w_col(2)),
            pl.BlockSpec((d, HEAD_DIM), w_col(3)),
            pl.BlockSpec(blk, slab(2)),
            pl.BlockSpec((lbf.shape[0], HEAD_DIM), lambda b, h: (0, h)),
            pl.BlockSpec((lbb.shape[0], HEAD_DIM), lambda b, h: (0, h)),
            pl.BlockSpec((1, HEAD_DIM), lambda b, h: (0, h)),
            row_slab(w_a),
            row_slab(w_b),
        ],
        out_specs=(pl.BlockSpec(blk, lambda b, h: (h, b, 0)), row_slab(w_a), row_slab(w_b)),
        out_shape=(jax.ShapeDtypeStruct((hw, batch * seq, HEAD_DIM), BF16),
                   jax.ShapeDtypeStruct(w_a.shape, BF16), jax.ShapeDtypeStruct(w_b.shape, BF16)),
        scratch_shapes=[
            pltpu.VMEM((seq, 2 * HEAD_DIM), BF16),
            pltpu.VMEM((2, seq // CHUNK, HEAD_DIM, HEAD_DIM), F32),
            pltpu.VMEM((2, seq // CHUNK, HEAD_DIM), F32),
            pltpu.VMEM((seq // CHUNK, HEAD_DIM, 2 * HEAD_DIM), BF16),
            pltpu.VMEM((seq, HEAD_DIM), F32),
        ],
        compiler_params=pltpu.CompilerParams(
            dimension_semantics=("parallel", "parallel"), vmem_limit_bytes=VMEM_LIMIT_BYTES),
        name="hgrn",
    )(p16, p16, xn, w_in, w_in, p16, lbf, lbb, out_gain, w_a, w_b)


def _t5_bucket_table():
    nb = REL_BUCKETS // 2
    max_exact = nb // 2
    c = np.arange(ATTN_BLOCK)[:, None]
    s = np.arange(KEY_SPAN)[None, :]
    rel = s - WINDOW - c
    bucket = (rel > 0).astype(np.int32) * nb
    n = np.abs(rel)
    large = max_exact + (np.log(np.maximum(n, 1) / max_exact) / np.log(REL_MAX_DIST / max_exact)
                         * (nb - max_exact)).astype(np.int32)
    large = np.minimum(large, nb - 1)
    bucket = bucket + np.where(n < max_exact, n, large).astype(np.int32)
    return np.where(np.abs(rel) <= WINDOW, bucket, -1).astype(np.int32)


def _attn_kernel(table_ref, sink_ref, bucket_ref, q_ref, k_ref, v_ref, w_ref, o_ref, w16_ref, bias_ref):
    w16_ref[...] = w_ref[...].astype(BF16)

    x = pl.program_id(1)
    seq = k_ref.shape[0]
    n_blocks = seq // ATTN_BLOCK
    scale = 1.0 / math.sqrt(HEAD_DIM)
    rows = GROUP * ATTN_BLOCK

    @pl.when((pl.program_id(0) == 0) & (x == 0))
    def _():
        bucket = bucket_ref[...]
        for head in range(ATTN_HEADS):
            bias = jnp.full((ATTN_BLOCK, KEY_SPAN), NEG_INF, F32)
            for b in range(REL_BUCKETS):
                bias = jnp.where(bucket == b, table_ref[b, head], bias)
            bias_ref[pl.ds(head * ATTN_BLOCK, ATTN_BLOCK), :] = bias

    row_head = lax.broadcasted_iota(jnp.int32, (rows, 1), 0) // ATTN_BLOCK
    sink = jnp.zeros((rows, 1), F32)
    for g in range(GROUP):
        sink = jnp.where(row_head == g, sink_ref[0, x * GROUP + g], sink)
    bias_rows = pl.ds(pl.multiple_of(x * rows, rows), rows)

    def block_rows(i):
        return pl.ds(i * ATTN_BLOCK, ATTN_BLOCK)

    def band(ref, i):
        return jnp.concatenate([ref[block_rows(max(i - 1, 0)), :], ref[block_rows(i), :],
                                ref[block_rows(min(i + 1, n_blocks - 1)), :]], axis=0)

    def scores(i):
        qs = jnp.concatenate([q_ref[g, block_rows(i), :] for g in range(GROUP)], axis=0)
        sc = lax.dot_general(qs, band(k_ref, i), (((1,), (1,)), ((), ())), preferred_element_type=F32)
        sc = sc * scale + bias_ref[bias_rows, :]
        if i in (0, n_blocks - 1):
            key_pos = (i - 1) * ATTN_BLOCK + lax.broadcasted_iota(jnp.int32, (rows, KEY_SPAN), 1)
            sc = jnp.where((key_pos >= 0) & (key_pos < seq), sc, NEG_INF)
        return sc

    def finish(i, sc):
        m = jnp.maximum(jnp.max(sc, axis=-1, keepdims=True), sink)
        pr = jnp.exp(sc - m)
        den = jnp.sum(pr, axis=-1, keepdims=True) + jnp.exp(sink - m)
        o = jnp.dot(pr.astype(BF16), band(v_ref, i), preferred_element_type=F32) / den
        for g in range(GROUP):
            o_ref[g, block_rows(i), :] = o[g * ATTN_BLOCK:(g + 1) * ATTN_BLOCK].astype(o_ref.dtype)

    for first in range(0, n_blocks, ATTN_BLOCKS_PER_GROUP):
        group = range(first, first + ATTN_BLOCKS_PER_GROUP)
        logits = [scores(i) for i in group]
        for i, sc in zip(group, logits):
            finish(i, sc)


def _attention(p16, sink, rel_table, w, batch, seq):
    q_blk0 = 3 * HGRN_HEADS // GROUP
    k_slab0 = 3 * HGRN_HEADS + ATTN_HEADS
    v_slab0 = k_slab0 + KV_HEADS
    bucket = jnp.asarray(_t5_bucket_table())
    smem = pl.BlockSpec(memory_space=pltpu.SMEM)
    w_slab = pl.BlockSpec((w.shape[0] // (batch * KV_HEADS), w.shape[1]), lambda b, x: (b * KV_HEADS + x, 0))
    return pl.pallas_call(
        _attn_kernel,
        grid=(batch, KV_HEADS),
        in_specs=[
            smem,
            smem,
            pl.BlockSpec((ATTN_BLOCK, KEY_SPAN), lambda b, x: (0, 0)),
            pl.BlockSpec((GROUP, seq, HEAD_DIM), lambda b, x: (q_blk0 + x, b, 0)),
            pl.BlockSpec((None, seq, HEAD_DIM), lambda b, x: (k_slab0 + x, b, 0)),
            pl.BlockSpec((None, seq, HEAD_DIM), lambda b, x: (v_slab0 + x, b, 0)),
            w_slab,
        ],
        out_specs=(pl.BlockSpec((GROUP, seq, HEAD_DIM), lambda b, x: (x, b, 0)), w_slab),
        out_shape=(jax.ShapeDtypeStruct((ATTN_HEADS, batch * seq, HEAD_DIM), BF16),
                   jax.ShapeDtypeStruct(w.shape, BF16)),
        scratch_shapes=[
            pltpu.VMEM((ATTN_HEADS * ATTN_BLOCK, KEY_SPAN), F32),
        ],
        compiler_params=pltpu.CompilerParams(
            dimension_semantics=("arbitrary", "arbitrary"), vmem_limit_bytes=VMEM_LIMIT_BYTES),
        name="attention",
    )(rel_table, sink, bucket, p16, p16, p16, w)


def _outproj_kernel(yh_ref, ya_ref, x_ref, wh_ref, wa_ref, g_ref, o_ref):
    y_h = jnp.concatenate([yh_ref[c] for c in range(yh_ref.shape[0])], axis=1)
    y_a = jnp.concatenate([ya_ref[c] for c in range(ya_ref.shape[0])], axis=1)
    mixed = jnp.dot(y_h, wh_ref[...], preferred_element_type=F32)
    mixed = mixed + jnp.dot(y_a, wa_ref[...], preferred_element_type=F32)
    o_ref[...] = x_ref[...] + _rms(mixed, g_ref[...])


def _outproj(y_h, y_a, x, w_out, gain, *, tm=512):
    m, d = x.shape
    return pl.pallas_call(
        _outproj_kernel,
        grid=(m // tm,),
        in_specs=[
            pl.BlockSpec((HGRN_HEADS, tm, HEAD_DIM), lambda i: (0, i, 0)),
            pl.BlockSpec((ATTN_HEADS, tm, HEAD_DIM), lambda i: (0, i, 0)),
            pl.BlockSpec((tm, d), lambda i: (i, 0)),
            pl.BlockSpec((HGRN_WIDTH, d), lambda i: (0, 0)),
            pl.BlockSpec((ATTN_WIDTH, d), lambda i: (1, 0)),
            pl.BlockSpec((1, d), lambda i: (0, 0)),
        ],
        out_specs=pl.BlockSpec((tm, d), lambda i: (i, 0)),
        out_shape=jax.ShapeDtypeStruct((m, d), F32),
        compiler_params=pltpu.CompilerParams(
            dimension_semantics=("parallel",), vmem_limit_bytes=VMEM_LIMIT_BYTES),
        name="outproj",
    )(y_h, y_a, x, w_out, w_out, gain)


def kernel(x, pre_norm_ffn1, post_norm_ffn1, w_ffn1_gate_up, w_ffn1_down, pre_norm_mix, post_norm_mix,
           w_mix_in, hgrn_lower_bounds_fwd, hgrn_lower_bounds_bwd, hgrn_out_norm, attn_sink, w_mix_out,
           pre_norm_ffn2, post_norm_ffn2, w_ffn2_gate_up, w_ffn2_down, rel_bias_table):
    batch, seq, d = x.shape
    depth = pre_norm_ffn1.shape[0]
    assert depth == 1 and d == D_MODEL
    xf = x.reshape(batch * seq, d)
    layer = 0
    xf, xn = _ffn(xf, pre_norm_ffn1[layer:layer + 1], post_norm_ffn1[layer:layer + 1],
                  w_ffn1_gate_up[layer], w_ffn1_down[layer], gnext=pre_norm_mix[layer:layer + 1])
    w_in = w_mix_in[layer]
    p16 = _proj(xn, w_in, [0, 1, 2, 3, 8, 9, 10, 11, 12], BF16, tm=4096)
    y_h, w2_gate_up, w2_down = _hgrn(p16, xn, w_in, hgrn_lower_bounds_fwd, hgrn_lower_bounds_bwd,
                                     hgrn_out_norm[layer:layer + 1], w_ffn2_gate_up[layer], w_ffn2_down[layer],
                                     batch, seq)
    y_a, w_out = _attention(p16, attn_sink[layer:layer + 1], rel_bias_table, w_mix_out[layer], batch, seq)
    xf = _outproj(y_h, y_a, xf, w_out, post_norm_mix[layer:layer + 1])
    xf = _ffn(xf, pre_norm_ffn2[layer:layer + 1], post_norm_ffn2[layer:layer + 1], w2_gate_up, w2_down, tf=512)
    return xf.reshape(batch, seq, d)
```

```python
import functools
import math

import jax
import jax.numpy as jnp
import numpy as np
from jax import lax
from jax.experimental import pallas as pl
from jax.experimental.pallas import tpu as pltpu

F32 = jnp.float32
BF16 = jnp.bfloat16

D_MODEL = 2048
HGRN_WIDTH = 1024
HEAD_DIM = 128
HGRN_HEADS = HGRN_WIDTH // HEAD_DIM
CHUNK = 64
ATTN_WIDTH = 1024
ATTN_HEADS = ATTN_WIDTH // HEAD_DIM
KV_HEADS = 2
GROUP = ATTN_HEADS // KV_HEADS
KV_WIDTH = KV_HEADS * HEAD_DIM
WINDOW = 128
ATTN_BLOCK = 128
KEY_SPAN = ATTN_BLOCK + 2 * WINDOW
REL_BUCKETS = 32
REL_MAX_DIST = 128
D_FF = 5632
EPS = 1e-6
NEG_INF = -1e30
LOG2_E = 1.0 / math.log(2.0)

MIB = 1024 * 1024
VMEM_LIMIT_BYTES = 60 * MIB
ROW_BLOCK = 32
PAIRS_PER_TRIP = 16
CHUNKS_PER_OUT_TRIP = 16
LOGIT_SLAB = 256
ATTN_BLOCKS_PER_GROUP = 2


def _rms(x, gain):
    return x * lax.rsqrt(jnp.mean(x * x, axis=-1, keepdims=True) + EPS) * gain


def _sigmoid(x):
    return 1.0 / (1.0 + jnp.exp(-x))


def _for_row_blocks(n_rows, fn):
    def body(r, carry):
        fn(pl.ds(pl.multiple_of(r * ROW_BLOCK, ROW_BLOCK), ROW_BLOCK))
        return carry

    lax.fori_loop(0, n_rows // ROW_BLOCK, body, 0, unroll=4)


def _bf16_weight(w_ref, axis):
    if len(w_ref.shape) == 3:
        return jnp.concatenate([w_ref[c] for c in range(w_ref.shape[0])], axis=axis)
    return w_ref[...].astype(BF16)


def _ffn_kernel(*refs, emit_next_norm, emit_weights, n_passthrough):
    x_ref, gpre_ref, gpost_ref = refs[:3]
    k = 3
    gnext_ref = refs[k] if emit_next_norm else None
    k += int(emit_next_norm)
    wg_ref, wu_ref, wd_ref = refs[k:k + 3]
    k += 3 + n_passthrough
    o_ref, h_ref = refs[k:k + 2]
    j = pl.program_id(1)
    tm = x_ref.shape[0]

    @pl.when(j == 0)
    def _():
        def prologue(rows):
            h_ref[rows, :] = _rms(x_ref[rows, :], gpre_ref[...]).astype(BF16)
            o_ref[rows, :] = jnp.zeros((ROW_BLOCK, o_ref.shape[1]), F32)

        _for_row_blocks(tm, prologue)

    w_gate, w_up, w_down = _bf16_weight(wg_ref, 1), _bf16_weight(wu_ref, 1), _bf16_weight(wd_ref, 0)
    if emit_weights:
        for w16_ref, w in zip(refs[k + 2:k + 5], (w_gate, w_up, w_down)):
            w16_ref[0] = w

    h = h_ref[...]
    gate = jnp.dot(h, w_gate, preferred_element_type=F32)
    up = jnp.dot(h, w_up, preferred_element_type=F32)
    act = (gate * _sigmoid(gate) * up).astype(BF16)
    o_ref[...] += jnp.dot(act, w_down, preferred_element_type=F32)

    @pl.when(j == pl.num_programs(1) - 1)
    def _():
        for r in range(0, tm, ROW_BLOCK):
            rows = pl.ds(r, ROW_BLOCK)
            new_x = x_ref[rows, :] + 0.5 * _rms(o_ref[rows, :], gpost_ref[...])
            o_ref[rows, :] = new_x
            if emit_next_norm:
                h_ref[rows, :] = _rms(new_x, gnext_ref[...]).astype(BF16)


def _ffn(x, gpre, gpost, w_gate_up, w_down, gnext=None, *, tm=1024, tf=256):
    m, d = x.shape
    nj = D_FF // tf
    emit = gnext is not None
    row_tile = pl.BlockSpec((tm, d), lambda i, j: (i, 0))
    gain_spec = pl.BlockSpec((1, d), lambda i, j: (0, 0))
    weight_specs = [
        pl.BlockSpec((d, tf), lambda i, j: (0, j)),
        pl.BlockSpec((d, tf), lambda i, j: (0, j + nj)),
        pl.BlockSpec((tf, d), lambda i, j: (j, 0)),
    ]
    gains = (gpre, gpost, gnext) if emit else (gpre, gpost)
    return pl.pallas_call(
        functools.partial(_ffn_kernel, emit_next_norm=emit, emit_weights=False, n_passthrough=0),
        grid=(m // tm, nj),
        in_specs=[row_tile] + [gain_spec] * len(gains) + weight_specs,
        out_specs=(row_tile, row_tile) if emit else row_tile,
        out_shape=((jax.ShapeDtypeStruct((m, d), F32), jax.ShapeDtypeStruct((m, d), BF16)) if emit
                   else jax.ShapeDtypeStruct((m, d), F32)),
        scratch_shapes=[] if emit else [pltpu.VMEM((tm, d), BF16)],
        compiler_params=pltpu.CompilerParams(
            dimension_semantics=("parallel", "arbitrary"), vmem_limit_bytes=VMEM_LIMIT_BYTES),
        name="ffn",
    )(x, *gains, w_gate_up, w_gate_up, w_down)


def _ffn_f32_weights(x, gpre, gpost, gnext, w_gate_up, w_down, *, tm=1024, tf=256, slabs_per_step=2):
    m, d = x.shape
    nj = D_FF // tf
    gains = (gpre, gpost, gnext)
    gain_spec = pl.BlockSpec((1, d), lambda i, j: (0, 0))
    params = pltpu.CompilerParams(dimension_semantics=("parallel", "arbitrary"), vmem_limit_bytes=VMEM_LIMIT_BYTES)
    results = (jax.ShapeDtypeStruct((m, d), F32), jax.ShapeDtypeStruct((m, d), BF16))

    first_tile = pl.BlockSpec((tm, d), lambda i, j: (0, 0))
    col_slab = pl.BlockSpec((1, d, tf), lambda i, j: (j, 0, 0))
    row_slab = pl.BlockSpec((1, tf, d), lambda i, j: (j, 0, 0))
    out, xn, wg16, wu16, wd16 = pl.pallas_call(
        functools.partial(_ffn_kernel, emit_next_norm=True, emit_weights=True, n_passthrough=0),
        grid=(1, nj),
        in_specs=[pl.BlockSpec((tm, d), lambda i, j: (0, 0), pipeline_mode=pl.Buffered(1))] + [gain_spec] * 3 + [
            pl.BlockSpec((d, tf), lambda i, j: (0, j)),
            pl.BlockSpec((d, tf), lambda i, j: (0, j + nj)),
            pl.BlockSpec((tf, d), lambda i, j: (j, 0)),
        ],
        out_specs=(first_tile, first_tile, col_slab, col_slab, row_slab),
        out_shape=results + (jax.ShapeDtypeStruct((nj, d, tf), BF16), jax.ShapeDtypeStruct((nj, d, tf), BF16),
                             jax.ShapeDtypeStruct((nj, tf, d), BF16)),
        compiler_params=params,
        name="ffn_first",
    )(x, *gains, w_gate_up, w_gate_up, w_down)

    per = slabs_per_step
    later_tile = pl.BlockSpec((tm, d), lambda i, j: (i + 1, 0))
    untouched = pl.BlockSpec(memory_space=pl.ANY)
    return pl.pallas_call(
        functools.partial(_ffn_kernel, emit_next_norm=True, emit_weights=False, n_passthrough=2),
        grid=(m // tm - 1, nj // per),
        in_specs=[later_tile] + [gain_spec] * 3 + [
            pl.BlockSpec((per, d, tf), lambda i, j: (j, 0, 0)),
            pl.BlockSpec((per, d, tf), lambda i, j: (j, 0, 0)),
            pl.BlockSpec((per, tf, d), lambda i, j: (j, 0, 0)),
            untouched, untouched,
        ],
        out_specs=(later_tile, later_tile),
        out_shape=results,
        input_output_aliases={7: 0, 8: 1},
        compiler_params=params,
        name="ffn_rest",
    )(x, *gains, wg16, wu16, wd16, out, xn)


def _proj_kernel(xn_ref, w_ref, o_ref):
    res = jnp.dot(xn_ref[...], w_ref[...].astype(BF16), preferred_element_type=F32).astype(o_ref.dtype)
    for c in range(o_ref.shape[0]):
        o_ref[c] = res[:, c * HEAD_DIM:(c + 1) * HEAD_DIM]


def _proj(xn, w_in, col_blocks, out_dtype, *, tm=2048, tn=512):
    m, d = xn.shape
    per = tn // HEAD_DIM
    jumps = [(pos, col_blocks[pos] - col_blocks[pos - 1] - 1) for pos in range(1, len(col_blocks))
             if col_blocks[pos] != col_blocks[pos - 1] + 1]

    def w_block(i, j):
        blk = j + col_blocks[0]
        for pos, gap in jumps:
            blk = blk + jnp.where(j >= pos, gap, 0)
        return (0, blk)

    return pl.pallas_call(
        _proj_kernel,
        grid=(m // tm, len(col_blocks)),
        in_specs=[
            pl.BlockSpec((tm, d), lambda i, j: (i, 0)),
            pl.BlockSpec((d, tn), w_block),
        ],
        out_specs=pl.BlockSpec((per, tm, HEAD_DIM), lambda i, j: (j, i, 0)),
        out_shape=jax.ShapeDtypeStruct((len(col_blocks) * per, m, HEAD_DIM), out_dtype),
        compiler_params=pltpu.CompilerParams(
            dimension_semantics=("parallel", "arbitrary"), vmem_limit_bytes=VMEM_LIMIT_BYTES),
        name="inproj",
    )(xn, w_in)


def _chunk_cumsum(x, row_in_chunk, reverse):
    n_rows = x.shape[0]
    c = x
    s = 1
    while s < CHUNK:
        if reverse:
            shifted = pltpu.roll(c, n_rows - s, axis=0)
            keep = row_in_chunk < CHUNK - s
        else:
            shifted = pltpu.roll(c, s, axis=0)
            keep = row_in_chunk >= s
        c = c + jnp.where(keep, shifted, 0.0)
        s *= 2
    return c


def _hgrn_kernel(q_ref, v_ref, xn_ref, wff_ref, wfb_ref, g_ref, lbf_ref, lbb_ref, gain_ref, wa_ref, wb_ref,
                 o_ref, wa16_ref, wb16_ref, qd_ref, kv_ref, dec_ref, st_ref, oi_ref):
    wa16_ref[...] = wa_ref[...].astype(BF16)
    wb16_ref[...] = wb_ref[...].astype(BF16)

    seq = q_ref.shape[0]
    n_chunks = seq // CHUNK
    pair = 2 * CHUNK
    contract_last = (((1,), (1,)), ((), ()))

    def gate_consts(lb_param_ref):
        a = lb_param_ref[...]
        e = jnp.exp(a - jnp.max(a, axis=0, keepdims=True))
        lb = e[0:1] / jnp.sum(e, axis=0, keepdims=True)
        return 0.5 * (1.0 + lb), 0.5 * (1.0 - lb)

    consts = (gate_consts(lbf_ref), gate_consts(lbb_ref))
    w_logits = jnp.concatenate([wff_ref[...].astype(BF16), wfb_ref[...].astype(BF16)], axis=1)

    r = lax.broadcasted_iota(jnp.int32, (pair, pair), 0)
    s = lax.broadcasted_iota(jnp.int32, (pair, pair), 1)
    same_chunk = (r // CHUNK) == (s // CHUNK)
    tri = (same_chunk & (r >= s), same_chunk & (s >= r))
    row_in_chunk = lax.broadcasted_iota(jnp.int32, (pair, HEAD_DIM), 0) % CHUNK
    zero_blk = jnp.zeros((CHUNK, HEAD_DIM), BF16)

    def trip_body(trip, carry):
        pairs = [trip * PAIRS_PER_TRIP + i for i in range(PAIRS_PER_TRIP)]
        rows_of = [pl.ds(pl.multiple_of(pi * pair, pair), pair) for pi in pairs]

        pairs_per_slab = LOGIT_SLAB // pair
        n_slabs = PAIRS_PER_TRIP // pairs_per_slab

        def project_logits(slab):
            start = pl.multiple_of((trip * n_slabs + slab) * LOGIT_SLAB, LOGIT_SLAB)
            return jnp.dot(xn_ref[pl.ds(start, LOGIT_SLAB), :], w_logits, preferred_element_type=F32)

        logits = [project_logits(s) for s in range(min(2, n_slabs))]
        gates = []
        for i in range(PAIRS_PER_TRIP):
            slab, offset = divmod(i, pairs_per_slab)
            per_dir = []
            for d in range(2):
                mid, half = consts[d]
                z = logits[slab][offset * pair:(offset + 1) * pair, d * HEAD_DIM:(d + 1) * HEAD_DIM]
                t = jnp.tanh(0.5 * z)
                f = mid + half * t
                k = half * (1.0 - t)
                per_dir.append((k, _chunk_cumsum(jnp.log(f) * LOG2_E, row_in_chunk, reverse=(d == 1))))
            gates.append(per_dir)
            if offset == pairs_per_slab - 1 and slab + 2 < n_slabs:
                logits.append(project_logits(slab + 2))

        mixes = []
        for pi, rows, per_dir in zip(pairs, rows_of, gates):
            q = q_ref[rows, :].astype(F32)
            vb = v_ref[rows, :]
            v_t = vb.astype(F32).T.astype(BF16)
            probs = None
            for d, (k, c) in enumerate(per_dir):
                c3 = c.reshape(2, CHUNK, HEAD_DIM)
                tot = c3[:, 0:1, :] if d == 1 else c3[:, CHUNK - 1:CHUNK, :]
                k_tail = (k * jnp.exp2(tot - c3).reshape(pair, HEAD_DIM)).astype(BF16)
                q_dec = (q * jnp.exp2(c)).astype(BF16)
                k_dec = (k * jnp.exp2(-c)).astype(BF16)
                qd_ref[rows, d * HEAD_DIM:(d + 1) * HEAD_DIM] = q_dec
                dec_ref[d, pl.ds(pi * 2, 2), :] = jnp.exp2(tot).reshape(2, HEAD_DIM)
                sc = lax.dot_general(q_dec, k_dec, contract_last, preferred_element_type=F32)
                sc = jnp.where(tri[d], sc, 0.0)
                probs = sc if probs is None else probs + sc
                rhs = jnp.concatenate(
                    [jnp.concatenate([k_tail[:CHUNK], zero_blk], axis=1),
                     jnp.concatenate([zero_blk, k_tail[CHUNK:]], axis=1)], axis=0)
                kv = jnp.dot(v_t, rhs, preferred_element_type=F32)
                kv_ref[d, pi * 2] = kv[:, :HEAD_DIM]
                kv_ref[d, pi * 2 + 1] = kv[:, HEAD_DIM:]
            mixes.append((probs.astype(BF16), vb))

        for rows, (probs, vb) in zip(rows_of, mixes):
            oi_ref[rows, :] = jnp.dot(probs, vb, preferred_element_type=F32)
        return carry

    lax.fori_loop(0, n_chunks // (2 * PAIRS_PER_TRIP), trip_body, 0)

    def scan_body(t, carry):
        sf, sb = carry
        nf, nb = t, n_chunks - 1 - t
        st_ref[nf, :, :HEAD_DIM] = sf.astype(BF16)
        st_ref[nb, :, HEAD_DIM:] = sb.astype(BF16)
        sf = sf * dec_ref[0, pl.ds(nf, 1), :] + kv_ref[0, nf]
        sb = sb * dec_ref[1, pl.ds(nb, 1), :] + kv_ref[1, nb]
        return sf, sb

    zero = jnp.zeros((HEAD_DIM, HEAD_DIM), F32)
    lax.fori_loop(0, n_chunks, scan_body, (zero, zero), unroll=2)

    def out_body(trip, carry):
        chunks = [trip * CHUNKS_PER_OUT_TRIP + i for i in range(CHUNKS_PER_OUT_TRIP)]
        rows_of = [pl.ds(pl.multiple_of(n * CHUNK, CHUNK), CHUNK) for n in chunks]
        inter = [lax.dot_general(qd_ref[rows, :], st_ref[n], contract_last, preferred_element_type=F32)
                 for n, rows in zip(chunks, rows_of)]
        for rows, o_inter in zip(rows_of, inter):
            o = _rms(oi_ref[rows, :] + o_inter, gain_ref[...])
            g = g_ref[rows, :].astype(F32)
            o_ref[rows, :] = (o * (0.5 * g * (1.0 + jnp.tanh(0.5 * g)))).astype(o_ref.dtype)
        return carry

    lax.fori_loop(0, n_chunks // CHUNKS_PER_OUT_TRIP, out_body, 0)


def _hgrn(p16, xn, w_in, lbf, lbb, out_gain, w_a, w_b, batch, seq):
    hw = HGRN_HEADS
    steps = batch * hw
    d = xn.shape[1]

    def w_col(group):
        return lambda b, h: (0, group * hw + h)

    def slab(group):
        return lambda b, h: (group * hw + h, b, 0)

    def row_slab(w):
        return pl.BlockSpec((w.shape[0] // steps, w.shape[1]), lambda b, h: (b * hw + h, 0))

    blk = (None, seq, HEAD_DIM)
    return pl.pallas_call(
        _hgrn_kernel,
        grid=(batch, hw),
        in_specs=[
            pl.BlockSpec(blk, slab(0)),
            pl.BlockSpec(blk, slab(1)),
            pl.BlockSpec((seq, d), lambda b, h: (b, 0)),
            pl.BlockSpec((d, HEAD_DIM), w_col(2)),
            pl.BlockSpec((d, HEAD_DIM), w_col(3)),
            pl.BlockSpec(blk, slab(2)),
            pl.BlockSpec((lbf.shape[0], HEAD_DIM), lambda b, h: (0, h)),
            pl.BlockSpec((lbb.shape[0], HEAD_DIM), lambda b, h: (0, h)),
            pl.BlockSpec((1, HEAD_DIM), lambda b, h: (0, h)),
            row_slab(w_a),
            row_slab(w_b),
        ],
        out_specs=(pl.BlockSpec(blk, lambda b, h: (h, b, 0)), row_slab(w_a), row_slab(w_b)),
        out_shape=(jax.ShapeDtypeStruct((hw, batch * seq, HEAD_DIM), BF16),
                   jax.ShapeDtypeStruct(w_a.shape, BF16), jax.ShapeDtypeStruct(w_b.shape, BF16)),
        scratch_shapes=[
            pltpu.VMEM((seq, 2 * HEAD_DIM), BF16),
            pltpu.VMEM((2, seq // CHUNK, HEAD_DIM, HEAD_DIM), F32),
            pltpu.VMEM((2, seq // CHUNK, HEAD_DIM), F32),
            pltpu.VMEM((seq // CHUNK, HEAD_DIM, 2 * HEAD_DIM), BF16),
            pltpu.VMEM((seq, HEAD_DIM), F32),
        ],
        compiler_params=pltpu.CompilerParams(
            dimension_semantics=("parallel", "parallel"), vmem_limit_bytes=VMEM_LIMIT_BYTES),
        name="hgrn",
    )(p16, p16, xn, w_in, w_in, p16, lbf, lbb, out_gain, w_a, w_b)


def _t5_bucket_table():
    nb = REL_BUCKETS // 2
    max_exact = nb // 2
    c = np.arange(ATTN_BLOCK)[:, None]
    s = np.arange(KEY_SPAN)[None, :]
    rel = s - WINDOW - c
    bucket = (rel > 0).astype(np.int32) * nb
    n = np.abs(rel)
    large = max_exact + (np.log(np.maximum(n, 1) / max_exact) / np.log(REL_MAX_DIST / max_exact)
                         * (nb - max_exact)).astype(np.int32)
    large = np.minimum(large, nb - 1)
    bucket = bucket + np.where(n < max_exact, n, large).astype(np.int32)
    return np.where(np.abs(rel) <= WINDOW, bucket, -1).astype(np.int32)


def _attn_kernel(table_ref, sink_ref, bucket_ref, q_ref, k_ref, v_ref, w_ref, o_ref, w16_ref, bias_ref):
    w16_ref[...] = w_ref[...].astype(BF16)

    x = pl.program_id(1)
    seq = k_ref.shape[0]
    n_blocks = seq // ATTN_BLOCK
    scale = 1.0 / math.sqrt(HEAD_DIM)
    rows = GROUP * ATTN_BLOCK

    @pl.when((pl.program_id(0) == 0) & (x == 0))
    def _():
        bucket = bucket_ref[...]
        for head in range(ATTN_HEADS):
            bias = jnp.full((ATTN_BLOCK, KEY_SPAN), NEG_INF, F32)
            for b in range(REL_BUCKETS):
                bias = jnp.where(bucket == b, table_ref[b, head], bias)
            bias_ref[pl.ds(head * ATTN_BLOCK, ATTN_BLOCK), :] = bias

    row_head = lax.broadcasted_iota(jnp.int32, (rows, 1), 0) // ATTN_BLOCK
    sink = jnp.zeros((rows, 1), F32)
    for g in range(GROUP):
        sink = jnp.where(row_head == g, sink_ref[0, x * GROUP + g], sink)
    bias_rows = pl.ds(pl.multiple_of(x * rows, rows), rows)

    def block_rows(i):
        return pl.ds(i * ATTN_BLOCK, ATTN_BLOCK)

    def band(ref, i):
        return jnp.concatenate([ref[block_rows(max(i - 1, 0)), :], ref[block_rows(i), :],
                                ref[block_rows(min(i + 1, n_blocks - 1)), :]], axis=0)

    def scores(i):
        qs = jnp.concatenate([q_ref[g, block_rows(i), :] for g in range(GROUP)], axis=0)
        sc = lax.dot_general(qs, band(k_ref, i), (((1,), (1,)), ((), ())), preferred_element_type=F32)
        sc = sc * scale + bias_ref[bias_rows, :]
        if i in (0, n_blocks - 1):
            key_pos = (i - 1) * ATTN_BLOCK + lax.broadcasted_iota(jnp.int32, (rows, KEY_SPAN), 1)
            sc = jnp.where((key_pos >= 0) & (key_pos < seq), sc, NEG_INF)
        return sc

    def finish(i, sc):
        m = jnp.maximum(jnp.max(sc, axis=-1, keepdims=True), sink)
        pr = jnp.exp(sc - m)
        den = jnp.sum(pr, axis=-1, keepdims=True) + jnp.exp(sink - m)
        o = jnp.dot(pr.astype(BF16), band(v_ref, i), preferred_element_type=F32) / den
        for g in range(GROUP):
            o_ref[g, block_rows(i), :] = o[g * ATTN_BLOCK:(g + 1) * ATTN_BLOCK].astype(o_ref.dtype)

    for first in range(0, n_blocks, ATTN_BLOCKS_PER_GROUP):
        group = range(first, first + ATTN_BLOCKS_PER_GROUP)
        logits = [scores(i) for i in group]
        for i, sc in zip(group, logits):
            finish(i, sc)


def _attention(p16, sink, rel_table, w, batch, seq):
    q_blk0 = 3 * HGRN_HEADS // GROUP
    k_slab0 = 3 * HGRN_HEADS + ATTN_HEADS
    v_slab0 = k_slab0 + KV_HEADS
    bucket = jnp.asarray(_t5_bucket_table())
    smem = pl.BlockSpec(memory_space=pltpu.SMEM)
    w_slab = pl.BlockSpec((w.shape[0] // (batch * KV_HEADS), w.shape[1]), lambda b, x: (b * KV_HEADS + x, 0))
    return pl.pallas_call(
        _attn_kernel,
        grid=(batch, KV_HEADS),
        in_specs=[
            smem,
            smem,
            pl.BlockSpec((ATTN_BLOCK, KEY_SPAN), lambda b, x: (0, 0)),
            pl.BlockSpec((GROUP, seq, HEAD_DIM), lambda b, x: (q_blk0 + x, b, 0)),
            pl.BlockSpec((None, seq, HEAD_DIM), lambda b, x: (k_slab0 + x, b, 0)),
            pl.BlockSpec((None, seq, HEAD_DIM), lambda b, x: (v_slab0 + x, b, 0)),
            w_slab,
        ],
        out_specs=(pl.BlockSpec((GROUP, seq, HEAD_DIM), lambda b, x: (x, b, 0)), w_slab),
        out_shape=(jax.ShapeDtypeStruct((ATTN_HEADS, batch * seq, HEAD_DIM), BF16),
                   jax.ShapeDtypeStruct(w.shape, BF16)),
        scratch_shapes=[
            pltpu.VMEM((ATTN_HEADS * ATTN_BLOCK, KEY_SPAN), F32),
        ],
        compiler_params=pltpu.CompilerParams(
            dimension_semantics=("arbitrary", "arbitrary"), vmem_limit_bytes=VMEM_LIMIT_BYTES),
        name="attention",
    )(rel_table, sink, bucket, p16, p16, p16, w)


def _outproj_kernel(yh_ref, ya_ref, x_ref, wh_ref, wa_ref, g_ref, o_ref):
    y_h = jnp.concatenate([yh_ref[c] for c in range(yh_ref.shape[0])], axis=1)
    y_a = jnp.concatenate([ya_ref[c] for c in range(ya_ref.shape[0])], axis=1)
    mixed = jnp.dot(y_h, wh_ref[...], preferred_element_type=F32)
    mixed = mixed + jnp.dot(y_a, wa_ref[...], preferred_element_type=F32)
    o_ref[...] = x_ref[...] + _rms(mixed, g_ref[...])


def _outproj(y_h, y_a, x, w_out, gain, *, tm=512):
    m, d = x.shape
    return pl.pallas_call(
        _outproj_kernel,
        grid=(m // tm,),
        in_specs=[
            pl.BlockSpec((HGRN_HEADS, tm, HEAD_DIM), lambda i: (0, i, 0)),
            pl.BlockSpec((ATTN_HEADS, tm, HEAD_DIM), lambda i: (0, i, 0)),
            pl.BlockSpec((tm, d), lambda i: (i, 0)),
            pl.BlockSpec((HGRN_WIDTH, d), lambda i: (0, 0)),
            pl.BlockSpec((ATTN_WIDTH, d), lambda i: (1, 0)),
            pl.BlockSpec((1, d), lambda i: (0, 0)),
        ],
        out_specs=pl.BlockSpec((tm, d), lambda i: (i, 0)),
        out_shape=jax.ShapeDtypeStruct((m, d), F32),
        compiler_params=pltpu.CompilerParams(
            dimension_semantics=("parallel",), vmem_limit_bytes=VMEM_LIMIT_BYTES),
        name="outproj",
    )(y_h, y_a, x, w_out, w_out, gain)


def kernel(x, pre_norm_ffn1, post_norm_ffn1, w_ffn1_gate_up, w_ffn1_down, pre_norm_mix, post_norm_mix,
           w_mix_in, hgrn_lower_bounds_fwd, hgrn_lower_bounds_bwd, hgrn_out_norm, attn_sink, w_mix_out,
           pre_norm_ffn2, post_norm_ffn2, w_ffn2_gate_up, w_ffn2_down, rel_bias_table):
    batch, seq, d = x.shape
    depth = pre_norm_ffn1.shape[0]
    assert depth == 1 and d == D_MODEL
    xf = x.reshape(batch * seq, d)
    layer = 0
    xf, xn = _ffn_f32_weights(xf, pre_norm_ffn1[layer:layer + 1], post_norm_ffn1[layer:layer + 1],
                              pre_norm_mix[layer:layer + 1], w_ffn1_gate_up[layer], w_ffn1_down[layer])
    w_in = w_mix_in[layer]
    p16 = _proj(xn, w_in, [0, 1, 2, 3, 8, 9, 10, 11, 12], BF16, tm=4096)
    y_h, w2_gate_up, w2_down = _hgrn(p16, xn, w_in, hgrn_lower_bounds_fwd, hgrn_lower_bounds_bwd,
                                     hgrn_out_norm[layer:layer + 1], w_ffn2_gate_up[layer], w_ffn2_down[layer],
                                     batch, seq)
    y_a, w_out = _attention(p16, attn_sink[layer:layer + 1], rel_bias_table, w_mix_out[layer], batch, seq)
    xf = _outproj(y_h, y_a, xf, w_out, post_norm_mix[layer:layer + 1])
    xf = _ffn(xf, pre_norm_ffn2[layer:layer + 1], post_norm_ffn2[layer:layer + 1], w2_gate_up, w2_down, tf=512)
    return xf.reshape(batch, seq, d)
```

```python
import functools
import math

import jax
import jax.numpy as jnp
import numpy as np
from jax import lax
from jax.experimental import pallas as pl
from jax.experimental.pallas import tpu as pltpu

F32 = jnp.float32
BF16 = jnp.bfloat16

D_MODEL = 2048
HGRN_WIDTH = 1024
HEAD_DIM = 128
HGRN_HEADS = HGRN_WIDTH // HEAD_DIM
CHUNK = 64
ATTN_WIDTH = 1024
ATTN_HEADS = ATTN_WIDTH // HEAD_DIM
KV_HEADS = 2
GROUP = ATTN_HEADS // KV_HEADS
KV_WIDTH = KV_HEADS * HEAD_DIM
WINDOW = 128
ATTN_BLOCK = 128
KEY_SPAN = ATTN_BLOCK + 2 * WINDOW
REL_BUCKETS = 32
REL_MAX_DIST = 128
D_FF = 5632
EPS = 1e-6
NEG_INF = -1e30
LOG2_E = 1.0 / math.log(2.0)

MIB = 1024 * 1024
VMEM_LIMIT_BYTES = 60 * MIB
ROW_BLOCK = 32
TAIL_CHUNK = 256
PAIRS_PER_TRIP = 16
CHUNKS_PER_OUT_TRIP = 16
LOGIT_SLAB = 256
ATTN_BLOCKS_PER_GROUP = 2


def _rms(x, gain):
    return x * lax.rsqrt(jnp.mean(x * x, axis=-1, keepdims=True) + EPS) * gain


def _sigmoid(x):
    return 1.0 / (1.0 + jnp.exp(-x))


def _for_row_blocks(n_rows, fn):
    def body(r, carry):
        fn(pl.ds(pl.multiple_of(r * ROW_BLOCK, ROW_BLOCK), ROW_BLOCK))
        return carry

    lax.fori_loop(0, n_rows // ROW_BLOCK, body, 0, unroll=4)


def _bf16_weight(w_ref, axis):
    if len(w_ref.shape) == 3:
        return jnp.concatenate([w_ref[c] for c in range(w_ref.shape[0])], axis=axis)
    return w_ref[...].astype(BF16)


def _ffn_kernel(*refs, emit_next_norm, emit_weights, n_passthrough):
    x_ref, gpre_ref, gpost_ref = refs[:3]
    k = 3
    gnext_ref = refs[k] if emit_next_norm else None
    k += int(emit_next_norm)
    wg_ref, wu_ref, wd_ref = refs[k:k + 3]
    k += 3 + n_passthrough
    o_ref, h_ref = refs[k:k + 2]
    j = pl.program_id(1)
    tm = x_ref.shape[0]

    @pl.when(j == 0)
    def _():
        def prologue(rows):
            h_ref[rows, :] = _rms(x_ref[rows, :], gpre_ref[...]).astype(BF16)
            o_ref[rows, :] = jnp.zeros((ROW_BLOCK, o_ref.shape[1]), F32)

        _for_row_blocks(tm, prologue)

    def finish_rows(r0, r1):
        for r in range(r0, r1, ROW_BLOCK):
            rows = pl.ds(r, ROW_BLOCK)
            new_x = x_ref[rows, :] + 0.5 * _rms(o_ref[rows, :], gpost_ref[...])
            o_ref[rows, :] = new_x
            if emit_next_norm:
                h_ref[rows, :] = _rms(new_x, gnext_ref[...]).astype(BF16)

    def step(last):
        w_gate, w_up, w_down = _bf16_weight(wg_ref, 1), _bf16_weight(wu_ref, 1), _bf16_weight(wd_ref, 0)
        if emit_weights:
            for w16_ref, w in zip(refs[k + 2:k + 5], (w_gate, w_up, w_down)):
                w16_ref[0] = w
        h = h_ref[...]
        gate = jnp.dot(h, w_gate, preferred_element_type=F32)
        up = jnp.dot(h, w_up, preferred_element_type=F32)
        act = (gate * _sigmoid(gate) * up).astype(BF16)
        if not last:
            o_ref[...] += jnp.dot(act, w_down, preferred_element_type=F32)
            return
        for c0 in range(0, tm, TAIL_CHUNK):
            rows = pl.ds(c0, TAIL_CHUNK)
            o_ref[rows, :] += jnp.dot(act[c0:c0 + TAIL_CHUNK], w_down, preferred_element_type=F32)
            finish_rows(c0, c0 + TAIL_CHUNK)

    is_last = j == pl.num_programs(1) - 1
    pl.when(jnp.logical_not(is_last))(functools.partial(step, False))
    pl.when(is_last)(functools.partial(step, True))


def _ffn(x, gpre, gpost, w_gate_up, w_down, gnext=None, *, tm=1024, tf=256):
    m, d = x.shape
    nj = D_FF // tf
    emit = gnext is not None
    row_tile = pl.BlockSpec((tm, d), lambda i, j: (i, 0))
    gain_spec = pl.BlockSpec((1, d), lambda i, j: (0, 0))
    weight_specs = [
        pl.BlockSpec((d, tf), lambda i, j: (0, j)),
        pl.BlockSpec((d, tf), lambda i, j: (0, j + nj)),
        pl.BlockSpec((tf, d), lambda i, j: (j, 0)),
    ]
    gains = (gpre, gpost, gnext) if emit else (gpre, gpost)
    return pl.pallas_call(
        functools.partial(_ffn_kernel, emit_next_norm=emit, emit_weights=False, n_passthrough=0),
        grid=(m // tm, nj),
        in_specs=[row_tile] + [gain_spec] * len(gains) + weight_specs,
        out_specs=(row_tile, row_tile) if emit else row_tile,
        out_shape=((jax.ShapeDtypeStruct((m, d), F32), jax.ShapeDtypeStruct((m, d), BF16)) if emit
                   else jax.ShapeDtypeStruct((m, d), F32)),
        scratch_shapes=[] if emit else [pltpu.VMEM((tm, d), BF16)],
        compiler_params=pltpu.CompilerParams(
            dimension_semantics=("parallel", "arbitrary"), vmem_limit_bytes=VMEM_LIMIT_BYTES),
        name="ffn",
    )(x, *gains, w_gate_up, w_gate_up, w_down)


def _ffn_f32_weights(x, gpre, gpost, gnext, w_gate_up, w_down, *, tm=1024, tf=256, slabs_per_step=2):
    m, d = x.shape
    nj = D_FF // tf
    gains = (gpre, gpost, gnext)
    gain_spec = pl.BlockSpec((1, d), lambda i, j: (0, 0))
    params = pltpu.CompilerParams(dimension_semantics=("parallel", "arbitrary"), vmem_limit_bytes=VMEM_LIMIT_BYTES)
    results = (jax.ShapeDtypeStruct((m, d), F32), jax.ShapeDtypeStruct((m, d), BF16))

    first_tile = pl.BlockSpec((tm, d), lambda i, j: (0, 0))
    col_slab = pl.BlockSpec((1, d, tf), lambda i, j: (j, 0, 0))
    row_slab = pl.BlockSpec((1, tf, d), lambda i, j: (j, 0, 0))
    out, xn, wg16, wu16, wd16 = pl.pallas_call(
        functools.partial(_ffn_kernel, emit_next_norm=True, emit_weights=True, n_passthrough=0),
        grid=(1, nj),
        in_specs=[pl.BlockSpec((tm, d), lambda i, j: (0, 0), pipeline_mode=pl.Buffered(1))] + [gain_spec] * 3 + [
            pl.BlockSpec((d, tf), lambda i, j: (0, j)),
            pl.BlockSpec((d, tf), lambda i, j: (0, j + nj)),
            pl.BlockSpec((tf, d), lambda i, j: (j, 0)),
        ],
        out_specs=(first_tile, first_tile, col_slab, col_slab, row_slab),
        out_shape=results + (jax.ShapeDtypeStruct((nj, d, tf), BF16), jax.ShapeDtypeStruct((nj, d, tf), BF16),
                             jax.ShapeDtypeStruct((nj, tf, d), BF16)),
        compiler_params=params,
        name="ffn_first",
    )(x, *gains, w_gate_up, w_gate_up, w_down)

    per = slabs_per_step
    later_tile = pl.BlockSpec((tm, d), lambda i, j: (i + 1, 0))
    untouched = pl.BlockSpec(memory_space=pl.ANY)
    return pl.pallas_call(
        functools.partial(_ffn_kernel, emit_next_norm=True, emit_weights=False, n_passthrough=2),
        grid=(m // tm - 1, nj // per),
        in_specs=[later_tile] + [gain_spec] * 3 + [
            pl.BlockSpec((per, d, tf), lambda i, j: (j, 0, 0)),
            pl.BlockSpec((per, d, tf), lambda i, j: (j, 0, 0)),
            pl.BlockSpec((per, tf, d), lambda i, j: (j, 0, 0)),
            untouched, untouched,
        ],
        out_specs=(later_tile, later_tile),
        out_shape=results,
        input_output_aliases={7: 0, 8: 1},
        compiler_params=params,
        name="ffn_rest",
    )(x, *gains, wg16, wu16, wd16, out, xn)


def _proj_kernel(xn_ref, w_ref, o_ref):
    res = jnp.dot(xn_ref[...], w_ref[...].astype(BF16), preferred_element_type=F32).astype(o_ref.dtype)
    for c in range(o_ref.shape[0]):
        o_ref[c] = res[:, c * HEAD_DIM:(c + 1) * HEAD_DIM]


def _proj(xn, w_in, col_blocks, out_dtype, *, tm=2048, tn=512):
    m, d = xn.shape
    per = tn // HEAD_DIM
    jumps = [(pos, col_blocks[pos] - col_blocks[pos - 1] - 1) for pos in range(1, len(col_blocks))
             if col_blocks[pos] != col_blocks[pos - 1] + 1]

    def w_block(i, j):
        blk = j + col_blocks[0]
        for pos, gap in jumps:
            blk = blk + jnp.where(j >= pos, gap, 0)
        return (0, blk)

    return pl.pallas_call(
        _proj_kernel,
        grid=(m // tm, len(col_blocks)),
        in_specs=[
            pl.BlockSpec((tm, d), lambda i, j: (i, 0)),
            pl.BlockSpec((d, tn), w_block),
        ],
        out_specs=pl.BlockSpec((per, tm, HEAD_DIM), lambda i, j: (j, i, 0)),
        out_shape=jax.ShapeDtypeStruct((len(col_blocks) * per, m, HEAD_DIM), out_dtype),
        compiler_params=pltpu.CompilerParams(
            dimension_semantics=("parallel", "arbitrary"), vmem_limit_bytes=VMEM_LIMIT_BYTES),
        name="inproj",
    )(xn, w_in)


def _chunk_cumsum(x, row_in_chunk, reverse):
    n_rows = x.shape[0]
    c = x
    s = 1
    while s < CHUNK:
        if reverse:
            shifted = pltpu.roll(c, n_rows - s, axis=0)
            keep = row_in_chunk < CHUNK - s
        else:
            shifted = pltpu.roll(c, s, axis=0)
            keep = row_in_chunk >= s
        c = c + jnp.where(keep, shifted, 0.0)
        s *= 2
    return c


def _hgrn_kernel(q_ref, v_ref, xn_ref, wff_ref, wfb_ref, g_ref, lbf_ref, lbb_ref, gain_ref, wa_ref, wb_ref,
                 o_ref, wa16_ref, wb16_ref, qd_ref, kv_ref, dec_ref, st_ref, oi_ref):
    wa16_ref[...] = wa_ref[...].astype(BF16)
    wb16_ref[...] = wb_ref[...].astype(BF16)

    seq = q_ref.shape[0]
    n_chunks = seq // CHUNK
    pair = 2 * CHUNK
    contract_last = (((1,), (1,)), ((), ()))

    def gate_consts(lb_param_ref):
        a = lb_param_ref[...]
        e = jnp.exp(a - jnp.max(a, axis=0, keepdims=True))
        lb = e[0:1] / jnp.sum(e, axis=0, keepdims=True)
        return 0.5 * (1.0 + lb), 0.5 * (1.0 - lb)

    consts = (gate_consts(lbf_ref), gate_consts(lbb_ref))
    w_logits = jnp.concatenate([wff_ref[...].astype(BF16), wfb_ref[...].astype(BF16)], axis=1)

    r = lax.broadcasted_iota(jnp.int32, (pair, pair), 0)
    s = lax.broadcasted_iota(jnp.int32, (pair, pair), 1)
    same_chunk = (r // CHUNK) == (s // CHUNK)
    tri = (same_chunk & (r >= s), same_chunk & (s >= r))
    row_in_chunk = lax.broadcasted_iota(jnp.int32, (pair, HEAD_DIM), 0) % CHUNK
    zero_blk = jnp.zeros((CHUNK, HEAD_DIM), BF16)

    def trip_body(trip, carry):
        pairs = [trip * PAIRS_PER_TRIP + i for i in range(PAIRS_PER_TRIP)]
        rows_of = [pl.ds(pl.multiple_of(pi * pair, pair), pair) for pi in pairs]

        pairs_per_slab = LOGIT_SLAB // pair
        n_slabs = PAIRS_PER_TRIP // pairs_per_slab

        def project_logits(slab):
            start = pl.multiple_of((trip * n_slabs + slab) * LOGIT_SLAB, LOGIT_SLAB)
            return jnp.dot(xn_ref[pl.ds(start, LOGIT_SLAB), :], w_logits, preferred_element_type=F32)

        logits = [project_logits(s) for s in range(min(2, n_slabs))]
        gates = []
        for i in range(PAIRS_PER_TRIP):
            slab, offset = divmod(i, pairs_per_slab)
            per_dir = []
            for d in range(2):
                mid, half = consts[d]
                z = logits[slab][offset * pair:(offset + 1) * pair, d * HEAD_DIM:(d + 1) * HEAD_DIM]
                t = jnp.tanh(0.5 * z)
                f = mid + half * t
                k = half * (1.0 - t)
                per_dir.append((k, _chunk_cumsum(jnp.log(f) * LOG2_E, row_in_chunk, reverse=(d == 1))))
            gates.append(per_dir)
            if offset == pairs_per_slab - 1 and slab + 2 < n_slabs:
                logits.append(project_logits(slab + 2))

        mixes = []
        for pi, rows, per_dir in zip(pairs, rows_of, gates):
            q = q_ref[rows, :].astype(F32)
            vb = v_ref[rows, :]
            v_t = vb.astype(F32).T.astype(BF16)
            probs = None
            for d, (k, c) in enumerate(per_dir):
                c3 = c.reshape(2, CHUNK, HEAD_DIM)
                tot = c3[:, 0:1, :] if d == 1 else c3[:, CHUNK - 1:CHUNK, :]
                k_tail = (k * jnp.exp2(tot - c3).reshape(pair, HEAD_DIM)).astype(BF16)
                q_dec = (q * jnp.exp2(c)).astype(BF16)
                k_dec = (k * jnp.exp2(-c)).astype(BF16)
                qd_ref[rows, d * HEAD_DIM:(d + 1) * HEAD_DIM] = q_dec
                dec_ref[d, pl.ds(pi * 2, 2), :] = jnp.exp2(tot).reshape(2, HEAD_DIM)
                sc = lax.dot_general(q_dec, k_dec, contract_last, preferred_element_type=F32)
                sc = jnp.where(tri[d], sc, 0.0)
                probs = sc if probs is None else probs + sc
                rhs = jnp.concatenate(
                    [jnp.concatenate([k_tail[:CHUNK], zero_blk], axis=1),
                     jnp.concatenate([zero_blk, k_tail[CHUNK:]], axis=1)], axis=0)
                kv = jnp.dot(v_t, rhs, preferred_element_type=F32)
                kv_ref[d, pi * 2] = kv[:, :HEAD_DIM]
                kv_ref[d, pi * 2 + 1] = kv[:, HEAD_DIM:]
            mixes.append((probs.astype(BF16), vb))

        for rows, (probs, vb) in zip(rows_of, mixes):
            oi_ref[rows, :] = jnp.dot(probs, vb, preferred_element_type=F32)
        return carry

    lax.fori_loop(0, n_chunks // (2 * PAIRS_PER_TRIP), trip_body, 0)

    def scan_body(t, carry):
        sf, sb = carry
        nf, nb = t, n_chunks - 1 - t
        st_ref[nf, :, :HEAD_DIM] = sf.astype(BF16)
        st_ref[nb, :, HEAD_DIM:] = sb.astype(BF16)
        sf = sf * dec_ref[0, pl.ds(nf, 1), :] + kv_ref[0, nf]
        sb = sb * dec_ref[1, pl.ds(nb, 1), :] + kv_ref[1, nb]
        return sf, sb

    zero = jnp.zeros((HEAD_DIM, HEAD_DIM), F32)
    lax.fori_loop(0, n_chunks, scan_body, (zero, zero), unroll=2)

    def out_body(trip, carry):
        chunks = [trip * CHUNKS_PER_OUT_TRIP + i for i in range(CHUNKS_PER_OUT_TRIP)]
        rows_of = [pl.ds(pl.multiple_of(n * CHUNK, CHUNK), CHUNK) for n in chunks]
        inter = [lax.dot_general(qd_ref[rows, :], st_ref[n], contract_last, preferred_element_type=F32)
                 for n, rows in zip(chunks, rows_of)]
        for rows, o_inter in zip(rows_of, inter):
            o = _rms(oi_ref[rows, :] + o_inter, gain_ref[...])
            g = g_ref[rows, :].astype(F32)
            o_ref[rows, :] = (o * (0.5 * g * (1.0 + jnp.tanh(0.5 * g)))).astype(o_ref.dtype)
        return carry

    lax.fori_loop(0, n_chunks // CHUNKS_PER_OUT_TRIP, out_body, 0)


def _hgrn(p16, xn, w_in, lbf, lbb, out_gain, w_a, w_b, batch, seq):
    hw = HGRN_HEADS
    steps = batch * hw
    d = xn.shape[1]

    def w_col(group):
        return lambda b, h: (0, group * hw + h)

    def slab(group):
        return lambda b, h: (group * hw + h, b, 0)

    def row_slab(w):
        return pl.BlockSpec((w.shape[0] // steps, w.shape[1]), lambda b, h: (b * hw + h, 0))

    blk = (None, seq, HEAD_DIM)
    return pl.pallas_call(
        _hgrn_kernel,
        grid=(batch, hw),
        in_specs=[
            pl.BlockSpec(blk, slab(0)),
            pl.BlockSpec(blk, slab(1)),
            pl.BlockSpec((seq, d), lambda b, h: (b, 0)),
            pl.BlockSpec((d, HEAD_DIM), w_col(2)),
            pl.BlockSpec((d, HEAD_DIM), w_col(3)),
            pl.BlockSpec(blk, slab(2)),
            pl.BlockSpec((lbf.shape[0], HEAD_DIM), lambda b, h: (0, h)),
            pl.BlockSpec((lbb.shape[0], HEAD_DIM), lambda b, h: (0, h)),
            pl.BlockSpec((1, HEAD_DIM), lambda b, h: (0, h)),
            row_slab(w_a),
            row_slab(w_b),
        ],
        out_specs=(pl.BlockSpec(blk, lambda b, h: (h, b, 0)), row_slab(w_a), row_slab(w_b)),
        out_shape=(jax.ShapeDtypeStruct((hw, batch * seq, HEAD_DIM), BF16),
                   jax.ShapeDtypeStruct(w_a.shape, BF16), jax.ShapeDtypeStruct(w_b.shape, BF16)),
        scratch_shapes=[
            pltpu.VMEM((seq, 2 * HEAD_DIM), BF16),
            pltpu.VMEM((2, seq // CHUNK, HEAD_DIM, HEAD_DIM), F32),
            pltpu.VMEM((2, seq // CHUNK, HEAD_DIM), F32),
            pltpu.VMEM((seq // CHUNK, HEAD_DIM, 2 * HEAD_DIM), BF16),
            pltpu.VMEM((seq, HEAD_DIM), F32),
        ],
        compiler_params=pltpu.CompilerParams(
            dimension_semantics=("parallel", "parallel"), vmem_limit_bytes=VMEM_LIMIT_BYTES),
        name="hgrn",
    )(p16, p16, xn, w_in, w_in, p16, lbf, lbb, out_gain, w_a, w_b)


def _t5_bucket_table():
    nb = REL_BUCKETS // 2
    max_exact = nb // 2
    c = np.arange(ATTN_BLOCK)[:, None]
    s = np.arange(KEY_SPAN)[None, :]
    rel = s - WINDOW - c
    bucket = (rel > 0).astype(np.int32) * nb
    n = np.abs(rel)
    large = max_exact + (np.log(np.maximum(n, 1) / max_exact) / np.log(REL_MAX_DIST / max_exact)
                         * (nb - max_exact)).astype(np.int32)
    large = np.minimum(large, nb - 1)
    bucket = bucket + np.where(n < max_exact, n, large).astype(np.int32)
    return np.where(np.abs(rel) <= WINDOW, bucket, -1).astype(np.int32)


def _attn_kernel(table_ref, sink_ref, bucket_ref, q_ref, k_ref, v_ref, w_ref, o_ref, w16_ref, bias_ref):
    w16_ref[...] = w_ref[...].astype(BF16)

    x = pl.program_id(1)
    seq = k_ref.shape[0]
    n_blocks = seq // ATTN_BLOCK
    scale = 1.0 / math.sqrt(HEAD_DIM)
    rows = GROUP * ATTN_BLOCK

    @pl.when((pl.program_id(0) == 0) & (x == 0))
    def _():
        bucket = bucket_ref[...]
        for head in range(ATTN_HEADS):
            bias = jnp.full((ATTN_BLOCK, KEY_SPAN), NEG_INF, F32)
            for b in range(REL_BUCKETS):
                bias = jnp.where(bucket == b, table_ref[b, head], bias)
            bias_ref[pl.ds(head * ATTN_BLOCK, ATTN_BLOCK), :] = bias

    row_head = lax.broadcasted_iota(jnp.int32, (rows, 1), 0) // ATTN_BLOCK
    sink = jnp.zeros((rows, 1), F32)
    for g in range(GROUP):
        sink = jnp.where(row_head == g, sink_ref[0, x * GROUP + g], sink)
    bias_rows = pl.ds(pl.multiple_of(x * rows, rows), rows)

    def block_rows(i):
        return pl.ds(i * ATTN_BLOCK, ATTN_BLOCK)

    def band(ref, i):
        return jnp.concatenate([ref[block_rows(max(i - 1, 0)), :], ref[block_rows(i), :],
                                ref[block_rows(min(i + 1, n_blocks - 1)), :]], axis=0)

    def scores(i):
        qs = jnp.concatenate([q_ref[g, block_rows(i), :] for g in range(GROUP)], axis=0)
        sc = lax.dot_general(qs, band(k_ref, i), (((1,), (1,)), ((), ())), preferred_element_type=F32)
        sc = sc * scale + bias_ref[bias_rows, :]
        if i in (0, n_blocks - 1):
            key_pos = (i - 1) * ATTN_BLOCK + lax.broadcasted_iota(jnp.int32, (rows, KEY_SPAN), 1)
            sc = jnp.where((key_pos >= 0) & (key_pos < seq), sc, NEG_INF)
        return sc

    def finish(i, sc):
        m = jnp.maximum(jnp.max(sc, axis=-1, keepdims=True), sink)
        pr = jnp.exp(sc - m)
        den = jnp.sum(pr, axis=-1, keepdims=True) + jnp.exp(sink - m)
        o = jnp.dot(pr.astype(BF16), band(v_ref, i), preferred_element_type=F32) / den
        for g in range(GROUP):
            o_ref[g, block_rows(i), :] = o[g * ATTN_BLOCK:(g + 1) * ATTN_BLOCK].astype(o_ref.dtype)

    for first in range(0, n_blocks, ATTN_BLOCKS_PER_GROUP):
        group = range(first, first + ATTN_BLOCKS_PER_GROUP)
        logits = [scores(i) for i in group]
        for i, sc in zip(group, logits):
            finish(i, sc)


def _attention(p16, sink, rel_table, w, batch, seq):
    q_blk0 = 3 * HGRN_HEADS // GROUP
    k_slab0 = 3 * HGRN_HEADS + ATTN_HEADS
    v_slab0 = k_slab0 + KV_HEADS
    bucket = jnp.asarray(_t5_bucket_table())
    smem = pl.BlockSpec(memory_space=pltpu.SMEM)
    w_slab = pl.BlockSpec((w.shape[0] // (batch * KV_HEADS), w.shape[1]), lambda b, x: (b * KV_HEADS + x, 0))
    return pl.pallas_call(
        _attn_kernel,
        grid=(batch, KV_HEADS),
        in_specs=[
            smem,
            smem,
            pl.BlockSpec((ATTN_BLOCK, KEY_SPAN), lambda b, x: (0, 0)),
            pl.BlockSpec((GROUP, seq, HEAD_DIM), lambda b, x: (q_blk0 + x, b, 0)),
            pl.BlockSpec((None, seq, HEAD_DIM), lambda b, x: (k_slab0 + x, b, 0)),
            pl.BlockSpec((None, seq, HEAD_DIM), lambda b, x: (v_slab0 + x, b, 0)),
            w_slab,
        ],
        out_specs=(pl.BlockSpec((GROUP, seq, HEAD_DIM), lambda b, x: (x, b, 0)), w_slab),
        out_shape=(jax.ShapeDtypeStruct((ATTN_HEADS, batch * seq, HEAD_DIM), BF16),
                   jax.ShapeDtypeStruct(w.shape, BF16)),
        scratch_shapes=[
            pltpu.VMEM((ATTN_HEADS * ATTN_BLOCK, KEY_SPAN), F32),
        ],
        compiler_params=pltpu.CompilerParams(
            dimension_semantics=("arbitrary", "arbitrary"), vmem_limit_bytes=VMEM_LIMIT_BYTES),
        name="attention",
    )(rel_table, sink, bucket, p16, p16, p16, w)


def _outproj_kernel(yh_ref, ya_ref, x_ref, wh_ref, wa_ref, g_ref, o_ref):
    y_h = jnp.concatenate([yh_ref[c] for c in range(yh_ref.shape[0])], axis=1)
    y_a = jnp.concatenate([ya_ref[c] for c in range(ya_ref.shape[0])], axis=1)
    mixed = jnp.dot(y_h, wh_ref[...], preferred_element_type=F32)
    mixed = mixed + jnp.dot(y_a, wa_ref[...], preferred_element_type=F32)
    o_ref[...] = x_ref[...] + _rms(mixed, g_ref[...])


def _outproj(y_h, y_a, x, w_out, gain, *, tm=512):
    m, d = x.shape
    return pl.pallas_call(
        _outproj_kernel,
        grid=(m // tm,),
        in_specs=[
            pl.BlockSpec((HGRN_HEADS, tm, HEAD_DIM), lambda i: (0, i, 0)),
            pl.BlockSpec((ATTN_HEADS, tm, HEAD_DIM), lambda i: (0, i, 0)),
            pl.BlockSpec((tm, d), lambda i: (i, 0)),
            pl.BlockSpec((HGRN_WIDTH, d), lambda i: (0, 0)),
            pl.BlockSpec((ATTN_WIDTH, d), lambda i: (1, 0)),
            pl.BlockSpec((1, d), lambda i: (0, 0)),
        ],
        out_specs=pl.BlockSpec((tm, d), lambda i: (i, 0)),
        out_shape=jax.ShapeDtypeStruct((m, d), F32),
        compiler_params=pltpu.CompilerParams(
            dimension_semantics=("parallel",), vmem_limit_bytes=VMEM_LIMIT_BYTES),
        name="outproj",
    )(y_h, y_a, x, w_out, w_out, gain)


def kernel(x, pre_norm_ffn1, post_norm_ffn1, w_ffn1_gate_up, w_ffn1_down, pre_norm_mix, post_norm_mix,
           w_mix_in, hgrn_lower_bounds_fwd, hgrn_lower_bounds_bwd, hgrn_out_norm, attn_sink, w_mix_out,
           pre_norm_ffn2, post_norm_ffn2, w_ffn2_gate_up, w_ffn2_down, rel_bias_table):
    batch, seq, d = x.shape
    depth = pre_norm_ffn1.shape[0]
    assert depth == 1 and d == D_MODEL
    xf = x.reshape(batch * seq, d)
    layer = 0
    xf, xn = _ffn_f32_weights(xf, pre_norm_ffn1[layer:layer + 1], post_norm_ffn1[layer:layer + 1],
                              pre_norm_mix[layer:layer + 1], w_ffn1_gate_up[layer], w_ffn1_down[layer])
    w_in = w_mix_in[layer]
    p16 = _proj(xn, w_in, [0, 1, 2, 3, 8, 9, 10, 11, 12], BF16, tm=4096)
    y_h, w2_gate_up, w2_down = _hgrn(p16, xn, w_in, hgrn_lower_bounds_fwd, hgrn_lower_bounds_bwd,
                                     hgrn_out_norm[layer:layer + 1], w_ffn2_gate_up[layer], w_ffn2_down[layer],
                                     batch, seq)
    y_a, w_out = _attention(p16, attn_sink[layer:layer + 1], rel_bias_table, w_mix_out[layer], batch, seq)
    xf = _outproj(y_h, y_a, xf, w_out, post_norm_mix[layer:layer + 1])
    xf = _ffn(xf, pre_norm_ffn2[layer:layer + 1], post_norm_ffn2[layer:layer + 1], w2_gate_up, w2_down, tf=512)
    return xf.reshape(batch, seq, d)
```

```python
import functools
import math

import jax
import jax.numpy as jnp
import numpy as np
from jax import lax
from jax.experimental import pallas as pl
from jax.experimental.pallas import tpu as pltpu

F32 = jnp.float32
BF16 = jnp.bfloat16

D_MODEL = 2048
HGRN_WIDTH = 1024
HEAD_DIM = 128
HGRN_HEADS = HGRN_WIDTH // HEAD_DIM
CHUNK = 64
ATTN_WIDTH = 1024
ATTN_HEADS = ATTN_WIDTH // HEAD_DIM
KV_HEADS = 2
GROUP = ATTN_HEADS // KV_HEADS
KV_WIDTH = KV_HEADS * HEAD_DIM
WINDOW = 128
ATTN_BLOCK = 128
KEY_SPAN = ATTN_BLOCK + 2 * WINDOW
REL_BUCKETS = 32
REL_MAX_DIST = 128
D_FF = 5632
EPS = 1e-6
NEG_INF = -1e30
LOG2_E = 1.0 / math.log(2.0)

MIB = 1024 * 1024
VMEM_LIMIT_BYTES = 60 * MIB
ROW_BLOCK = 32
EDGE_CHUNK = 256
PAIRS_PER_TRIP = 16
CHUNKS_PER_OUT_TRIP = 16
LOGIT_SLAB = 256
ATTN_BLOCKS_PER_GROUP = 2


def _rms(x, gain):
    return x * lax.rsqrt(jnp.mean(x * x, axis=-1, keepdims=True) + EPS) * gain


def _sigmoid(x):
    return 1.0 / (1.0 + jnp.exp(-x))


def _bf16_weight(w_ref, axis):
    if len(w_ref.shape) == 3:
        return jnp.concatenate([w_ref[c] for c in range(w_ref.shape[0])], axis=axis)
    return w_ref[...].astype(BF16)


def _ffn_kernel(*refs, emit_next_norm, emit_weights, n_passthrough):
    x_ref, gpre_ref, gpost_ref = refs[:3]
    k = 3
    gnext_ref = refs[k] if emit_next_norm else None
    k += int(emit_next_norm)
    wg_ref, wu_ref, wd_ref = refs[k:k + 3]
    k += 3 + n_passthrough
    o_ref, h_ref = refs[k:k + 2]
    j = pl.program_id(1)
    tm = x_ref.shape[0]

    def pre_norm_rows(r0, r1):
        for r in range(r0, r1, ROW_BLOCK):
            rows = pl.ds(r, ROW_BLOCK)
            h_ref[rows, :] = _rms(x_ref[rows, :], gpre_ref[...]).astype(BF16)

    def finish_rows(r0, r1):
        for r in range(r0, r1, ROW_BLOCK):
            rows = pl.ds(r, ROW_BLOCK)
            new_x = x_ref[rows, :] + 0.5 * _rms(o_ref[rows, :], gpost_ref[...])
            o_ref[rows, :] = new_x
            if emit_next_norm:
                h_ref[rows, :] = _rms(new_x, gnext_ref[...]).astype(BF16)

    def swiglu(h, w_gate, w_up):
        gate = jnp.dot(h, w_gate, preferred_element_type=F32)
        up = jnp.dot(h, w_up, preferred_element_type=F32)
        return (gate * _sigmoid(gate) * up).astype(BF16)

    def step(position):
        w_gate, w_up, w_down = _bf16_weight(wg_ref, 1), _bf16_weight(wu_ref, 1), _bf16_weight(wd_ref, 0)
        if emit_weights:
            for w16_ref, w in zip(refs[k + 2:k + 5], (w_gate, w_up, w_down)):
                w16_ref[0] = w
        if position == "first":
            acts = []
            for c0 in range(0, tm, EDGE_CHUNK):
                pre_norm_rows(c0, c0 + EDGE_CHUNK)
                acts.append(swiglu(h_ref[pl.ds(c0, EDGE_CHUNK), :], w_gate, w_up))
            o_ref[...] = jnp.dot(jnp.concatenate(acts, axis=0), w_down, preferred_element_type=F32)
            return
        act = swiglu(h_ref[...], w_gate, w_up)
        if position == "middle":
            o_ref[...] += jnp.dot(act, w_down, preferred_element_type=F32)
            return
        for c0 in range(0, tm, EDGE_CHUNK):
            rows = pl.ds(c0, EDGE_CHUNK)
            o_ref[rows, :] += jnp.dot(act[c0:c0 + EDGE_CHUNK], w_down, preferred_element_type=F32)
            finish_rows(c0, c0 + EDGE_CHUNK)

    last = pl.num_programs(1) - 1
    pl.when(j == 0)(functools.partial(step, "first"))
    pl.when((j > 0) & (j < last))(functools.partial(step, "middle"))
    pl.when(j == last)(functools.partial(step, "last"))


def _ffn(x, gpre, gpost, w_gate_up, w_down, gnext=None, *, tm=1024, tf=256):
    m, d = x.shape
    nj = D_FF // tf
    emit = gnext is not None
    row_tile = pl.BlockSpec((tm, d), lambda i, j: (i, 0))
    gain_spec = pl.BlockSpec((1, d), lambda i, j: (0, 0))
    weight_specs = [
        pl.BlockSpec((d, tf), lambda i, j: (0, j)),
        pl.BlockSpec((d, tf), lambda i, j: (0, j + nj)),
        pl.BlockSpec((tf, d), lambda i, j: (j, 0)),
    ]
    gains = (gpre, gpost, gnext) if emit else (gpre, gpost)
    return pl.pallas_call(
        functools.partial(_ffn_kernel, emit_next_norm=emit, emit_weights=False, n_passthrough=0),
        grid=(m // tm, nj),
        in_specs=[row_tile] + [gain_spec] * len(gains) + weight_specs,
        out_specs=(row_tile, row_tile) if emit else row_tile,
        out_shape=((jax.ShapeDtypeStruct((m, d), F32), jax.ShapeDtypeStruct((m, d), BF16)) if emit
                   else jax.ShapeDtypeStruct((m, d), F32)),
        scratch_shapes=[] if emit else [pltpu.VMEM((tm, d), BF16)],
        compiler_params=pltpu.CompilerParams(
            dimension_semantics=("parallel", "arbitrary"), vmem_limit_bytes=VMEM_LIMIT_BYTES),
        name="ffn",
    )(x, *gains, w_gate_up, w_gate_up, w_down)


def _ffn_f32_weights(x, gpre, gpost, gnext, w_gate_up, w_down, *, tm=1024, tf=256, slabs_per_step=2):
    m, d = x.shape
    nj = D_FF // tf
    gains = (gpre, gpost, gnext)
    gain_spec = pl.BlockSpec((1, d), lambda i, j: (0, 0))
    params = pltpu.CompilerParams(dimension_semantics=("parallel", "arbitrary"), vmem_limit_bytes=VMEM_LIMIT_BYTES)
    results = (jax.ShapeDtypeStruct((m, d), F32), jax.ShapeDtypeStruct((m, d), BF16))

    first_tile = pl.BlockSpec((tm, d), lambda i, j: (0, 0))
    col_slab = pl.BlockSpec((1, d, tf), lambda i, j: (j, 0, 0))
    row_slab = pl.BlockSpec((1, tf, d), lambda i, j: (j, 0, 0))
    out, xn, wg16, wu16, wd16 = pl.pallas_call(
        functools.partial(_ffn_kernel, emit_next_norm=True, emit_weights=True, n_passthrough=0),
        grid=(1, nj),
        in_specs=[pl.BlockSpec((tm, d), lambda i, j: (0, 0), pipeline_mode=pl.Buffered(1))] + [gain_spec] * 3 + [
            pl.BlockSpec((d, tf), lambda i, j: (0, j)),
            pl.BlockSpec((d, tf), lambda i, j: (0, j + nj)),
            pl.BlockSpec((tf, d), lambda i, j: (j, 0)),
        ],
        out_specs=(first_tile, first_tile, col_slab, col_slab, row_slab),
        out_shape=results + (jax.ShapeDtypeStruct((nj, d, tf), BF16), jax.ShapeDtypeStruct((nj, d, tf), BF16),
                             jax.ShapeDtypeStruct((nj, tf, d), BF16)),
        compiler_params=params,
        name="ffn_first",
    )(x, *gains, w_gate_up, w_gate_up, w_down)

    per = slabs_per_step
    later_tile = pl.BlockSpec((tm, d), lambda i, j: (i + 1, 0))
    untouched = pl.BlockSpec(memory_space=pl.ANY)
    return pl.pallas_call(
        functools.partial(_ffn_kernel, emit_next_norm=True, emit_weights=False, n_passthrough=2),
        grid=(m // tm - 1, nj // per),
        in_specs=[later_tile] + [gain_spec] * 3 + [
            pl.BlockSpec((per, d, tf), lambda i, j: (j, 0, 0)),
            pl.BlockSpec((per, d, tf), lambda i, j: (j, 0, 0)),
            pl.BlockSpec((per, tf, d), lambda i, j: (j, 0, 0)),
            untouched, untouched,
        ],
        out_specs=(later_tile, later_tile),
        out_shape=results,
        input_output_aliases={7: 0, 8: 1},
        compiler_params=params,
        name="ffn_rest",
    )(x, *gains, wg16, wu16, wd16, out, xn)


def _proj_kernel(xn_ref, w_ref, o_ref):
    res = jnp.dot(xn_ref[...], w_ref[...].astype(BF16), preferred_element_type=F32).astype(o_ref.dtype)
    for c in range(o_ref.shape[0]):
        o_ref[c] = res[:, c * HEAD_DIM:(c + 1) * HEAD_DIM]


def _proj(xn, w_in, col_blocks, out_dtype, *, tm=2048, tn=512):
    m, d = xn.shape
    per = tn // HEAD_DIM
    jumps = [(pos, col_blocks[pos] - col_blocks[pos - 1] - 1) for pos in range(1, len(col_blocks))
             if col_blocks[pos] != col_blocks[pos - 1] + 1]

    def w_block(i, j):
        blk = j + col_blocks[0]
        for pos, gap in jumps:
            blk = blk + jnp.where(j >= pos, gap, 0)
        return (0, blk)

    return pl.pallas_call(
        _proj_kernel,
        grid=(m // tm, len(col_blocks)),
        in_specs=[
            pl.BlockSpec((tm, d), lambda i, j: (i, 0)),
            pl.BlockSpec((d, tn), w_block),
        ],
        out_specs=pl.BlockSpec((per, tm, HEAD_DIM), lambda i, j: (j, i, 0)),
        out_shape=jax.ShapeDtypeStruct((len(col_blocks) * per, m, HEAD_DIM), out_dtype),
        compiler_params=pltpu.CompilerParams(
            dimension_semantics=("parallel", "arbitrary"), vmem_limit_bytes=VMEM_LIMIT_BYTES),
        name="inproj",
    )(xn, w_in)


def _chunk_cumsum(x, row_in_chunk, reverse):
    n_rows = x.shape[0]
    c = x
    s = 1
    while s < CHUNK:
        if reverse:
            shifted = pltpu.roll(c, n_rows - s, axis=0)
            keep = row_in_chunk < CHUNK - s
        else:
            shifted = pltpu.roll(c, s, axis=0)
            keep = row_in_chunk >= s
        c = c + jnp.where(keep, shifted, 0.0)
        s *= 2
    return c


def _hgrn_kernel(q_ref, v_ref, xn_ref, wff_ref, wfb_ref, g_ref, lbf_ref, lbb_ref, gain_ref, wa_ref, wb_ref,
                 o_ref, wa16_ref, wb16_ref, qd_ref, kv_ref, dec_ref, st_ref, oi_ref):
    wa16_ref[...] = wa_ref[...].astype(BF16)
    wb16_ref[...] = wb_ref[...].astype(BF16)

    seq = q_ref.shape[0]
    n_chunks = seq // CHUNK
    pair = 2 * CHUNK
    contract_last = (((1,), (1,)), ((), ()))

    def gate_consts(lb_param_ref):
        a = lb_param_ref[...]
        e = jnp.exp(a - jnp.max(a, axis=0, keepdims=True))
        lb = e[0:1] / jnp.sum(e, axis=0, keepdims=True)
        return 0.5 * (1.0 + lb), 0.5 * (1.0 - lb)

    consts = (gate_consts(lbf_ref), gate_consts(lbb_ref))
    w_logits = jnp.concatenate([wff_ref[...].astype(BF16), wfb_ref[...].astype(BF16)], axis=1)

    r = lax.broadcasted_iota(jnp.int32, (pair, pair), 0)
    s = lax.broadcasted_iota(jnp.int32, (pair, pair), 1)
    same_chunk = (r // CHUNK) == (s // CHUNK)
    tri = (same_chunk & (r >= s), same_chunk & (s >= r))
    row_in_chunk = lax.broadcasted_iota(jnp.int32, (pair, HEAD_DIM), 0) % CHUNK
    zero_blk = jnp.zeros((CHUNK, HEAD_DIM), BF16)

    def trip_body(trip, carry):
        pairs = [trip * PAIRS_PER_TRIP + i for i in range(PAIRS_PER_TRIP)]
        rows_of = [pl.ds(pl.multiple_of(pi * pair, pair), pair) for pi in pairs]

        pairs_per_slab = LOGIT_SLAB // pair
        n_slabs = PAIRS_PER_TRIP // pairs_per_slab

        def project_logits(slab):
            start = pl.multiple_of((trip * n_slabs + slab) * LOGIT_SLAB, LOGIT_SLAB)
            return jnp.dot(xn_ref[pl.ds(start, LOGIT_SLAB), :], w_logits, preferred_element_type=F32)

        logits = [project_logits(s) for s in range(min(2, n_slabs))]
        gates = []
        for i in range(PAIRS_PER_TRIP):
            slab, offset = divmod(i, pairs_per_slab)
            per_dir = []
            for d in range(2):
                mid, half = consts[d]
                z = logits[slab][offset * pair:(offset + 1) * pair, d * HEAD_DIM:(d + 1) * HEAD_DIM]
                t = jnp.tanh(0.5 * z)
                f = mid + half * t
                k = half * (1.0 - t)
                per_dir.append((k, _chunk_cumsum(jnp.log(f) * LOG2_E, row_in_chunk, reverse=(d == 1))))
            gates.append(per_dir)
            if offset == pairs_per_slab - 1 and slab + 2 < n_slabs:
                logits.append(project_logits(slab + 2))

        mixes = []
        for pi, rows, per_dir in zip(pairs, rows_of, gates):
            q = q_ref[rows, :].astype(F32)
            vb = v_ref[rows, :]
            v_t = vb.astype(F32).T.astype(BF16)
            probs = None
            for d, (k, c) in enumerate(per_dir):
                c3 = c.reshape(2, CHUNK, HEAD_DIM)
                tot = c3[:, 0:1, :] if d == 1 else c3[:, CHUNK - 1:CHUNK, :]
                k_tail = (k * jnp.exp2(tot - c3).reshape(pair, HEAD_DIM)).astype(BF16)
                q_dec = (q * jnp.exp2(c)).astype(BF16)
                k_dec = (k * jnp.exp2(-c)).astype(BF16)
                qd_ref[rows, d * HEAD_DIM:(d + 1) * HEAD_DIM] = q_dec
                dec_ref[d, pl.ds(pi * 2, 2), :] = jnp.exp2(tot).reshape(2, HEAD_DIM)
                sc = lax.dot_general(q_dec, k_dec, contract_last, preferred_element_type=F32)
                sc = jnp.where(tri[d], sc, 0.0)
                probs = sc if probs is None else probs + sc
                rhs = jnp.concatenate(
                    [jnp.concatenate([k_tail[:CHUNK], zero_blk], axis=1),
                     jnp.concatenate([zero_blk, k_tail[CHUNK:]], axis=1)], axis=0)
                kv = jnp.dot(v_t, rhs, preferred_element_type=F32)
                kv_ref[d, pi * 2] = kv[:, :HEAD_DIM]
                kv_ref[d, pi * 2 + 1] = kv[:, HEAD_DIM:]
            mixes.append((probs.astype(BF16), vb))

        for rows, (probs, vb) in zip(rows_of, mixes):
            oi_ref[rows, :] = jnp.dot(probs, vb, preferred_element_type=F32)
        return carry

    lax.fori_loop(0, n_chunks // (2 * PAIRS_PER_TRIP), trip_body, 0)

    def scan_body(t, carry):
        sf, sb = carry
        nf, nb = t, n_chunks - 1 - t
        st_ref[nf, :, :HEAD_DIM] = sf.astype(BF16)
        st_ref[nb, :, HEAD_DIM:] = sb.astype(BF16)
        sf = sf * dec_ref[0, pl.ds(nf, 1), :] + kv_ref[0, nf]
        sb = sb * dec_ref[1, pl.ds(nb, 1), :] + kv_ref[1, nb]
        return sf, sb

    zero = jnp.zeros((HEAD_DIM, HEAD_DIM), F32)
    lax.fori_loop(0, n_chunks, scan_body, (zero, zero), unroll=2)

    def out_body(trip, carry):
        chunks = [trip * CHUNKS_PER_OUT_TRIP + i for i in range(CHUNKS_PER_OUT_TRIP)]
        rows_of = [pl.ds(pl.multiple_of(n * CHUNK, CHUNK), CHUNK) for n in chunks]
        inter = [lax.dot_general(qd_ref[rows, :], st_ref[n], contract_last, preferred_element_type=F32)
                 for n, rows in zip(chunks, rows_of)]
        for rows, o_inter in zip(rows_of, inter):
            o = _rms(oi_ref[rows, :] + o_inter, gain_ref[...])
            g = g_ref[rows, :].astype(F32)
            o_ref[rows, :] = (o * (0.5 * g * (1.0 + jnp.tanh(0.5 * g)))).astype(o_ref.dtype)
        return carry

    lax.fori_loop(0, n_chunks // CHUNKS_PER_OUT_TRIP, out_body, 0)


def _hgrn(p16, xn, w_in, lbf, lbb, out_gain, w_a, w_b, batch, seq):
    hw = HGRN_HEADS
    steps = batch * hw
    d = xn.shape[1]

    def w_col(group):
        return lambda b, h: (0, group * hw + h)

    def slab(group):
        return lambda b, h: (group * hw + h, b, 0)

    def row_slab(w):
        return pl.BlockSpec((w.shape[0] // steps, w.shape[1]), lambda b, h: (b * hw + h, 0))

    blk = (None, seq, HEAD_DIM)
    return pl.pallas_call(
        _hgrn_kernel,
        grid=(batch, hw),
        in_specs=[
            pl.BlockSpec(blk, slab(0)),
            pl.BlockSpec(blk, slab(1)),
            pl.BlockSpec((seq, d), lambda b, h: (b, 0)),
            pl.BlockSpec((d, HEAD_DIM), w_col(2)),
            pl.BlockSpec((d, HEAD_DIM), w_col(3)),
            pl.BlockSpec(blk, slab(2)),
            pl.BlockSpec((lbf.shape[0], HEAD_DIM), lambda b, h: (0, h)),
            pl.BlockSpec((lbb.shape[0], HEAD_DIM), lambda b, h: (0, h)),
            pl.BlockSpec((1, HEAD_DIM), lambda b, h: (0, h)),
            row_slab(w_a),
            row_slab(w_b),
        ],
        out_specs=(pl.BlockSpec(blk, lambda b, h: (h, b, 0)), row_slab(w_a), row_slab(w_b)),
        out_shape=(jax.ShapeDtypeStruct((hw, batch * seq, HEAD_DIM), BF16),
                   jax.ShapeDtypeStruct(w_a.shape, BF16), jax.ShapeDtypeStruct(w_b.shape, BF16)),
        scratch_shapes=[
            pltpu.VMEM((seq, 2 * HEAD_DIM), BF16),
            pltpu.VMEM((2, seq // CHUNK, HEAD_DIM, HEAD_DIM), F32),
            pltpu.VMEM((2, seq // CHUNK, HEAD_DIM), F32),
            pltpu.VMEM((seq // CHUNK, HEAD_DIM, 2 * HEAD_DIM), BF16),
            pltpu.VMEM((seq, HEAD_DIM), F32),
        ],
        compiler_params=pltpu.CompilerParams(
            dimension_semantics=("parallel", "parallel"), vmem_limit_bytes=VMEM_LIMIT_BYTES),
        name="hgrn",
    )(p16, p16, xn, w_in, w_in, p16, lbf, lbb, out_gain, w_a, w_b)


def _t5_bucket_table():
    nb = REL_BUCKETS // 2
    max_exact = nb // 2
    c = np.arange(ATTN_BLOCK)[:, None]
    s = np.arange(KEY_SPAN)[None, :]
    rel = s - WINDOW - c
    bucket = (rel > 0).astype(np.int32) * nb
    n = np.abs(rel)
    large = max_exact + (np.log(np.maximum(n, 1) / max_exact) / np.log(REL_MAX_DIST / max_exact)
                         * (nb - max_exact)).astype(np.int32)
    large = np.minimum(large, nb - 1)
    bucket = bucket + np.where(n < max_exact, n, large).astype(np.int32)
    return np.where(np.abs(rel) <= WINDOW, bucket, -1).astype(np.int32)


def _attn_kernel(table_ref, sink_ref, bucket_ref, q_ref, k_ref, v_ref, w_ref, o_ref, w16_ref, bias_ref):
    w16_ref[...] = w_ref[...].astype(BF16)

    x = pl.program_id(1)
    seq = k_ref.shape[0]
    n_blocks = seq // ATTN_BLOCK
    scale = 1.0 / math.sqrt(HEAD_DIM)
    rows = GROUP * ATTN_BLOCK

    @pl.when((pl.program_id(0) == 0) & (x == 0))
    def _():
        bucket = bucket_ref[...]
        for head in range(ATTN_HEADS):
            bias = jnp.full((ATTN_BLOCK, KEY_SPAN), NEG_INF, F32)
            for b in range(REL_BUCKETS):
                bias = jnp.where(bucket == b, table_ref[b, head], bias)
            bias_ref[pl.ds(head * ATTN_BLOCK, ATTN_BLOCK), :] = bias

    row_head = lax.broadcasted_iota(jnp.int32, (rows, 1), 0) // ATTN_BLOCK
    sink = jnp.zeros((rows, 1), F32)
    for g in range(GROUP):
        sink = jnp.where(row_head == g, sink_ref[0, x * GROUP + g], sink)
    bias_rows = pl.ds(pl.multiple_of(x * rows, rows), rows)

    def block_rows(i):
        return pl.ds(i * ATTN_BLOCK, ATTN_BLOCK)

    def band(ref, i):
        return jnp.concatenate([ref[block_rows(max(i - 1, 0)), :], ref[block_rows(i), :],
                                ref[block_rows(min(i + 1, n_blocks - 1)), :]], axis=0)

    def scores(i):
        qs = jnp.concatenate([q_ref[g, block_rows(i), :] for g in range(GROUP)], axis=0)
        sc = lax.dot_general(qs, band(k_ref, i), (((1,), (1,)), ((), ())), preferred_element_type=F32)
        sc = sc * scale + bias_ref[bias_rows, :]
        if i in (0, n_blocks - 1):
            key_pos = (i - 1) * ATTN_BLOCK + lax.broadcasted_iota(jnp.int32, (rows, KEY_SPAN), 1)
            sc = jnp.where((key_pos >= 0) & (key_pos < seq), sc, NEG_INF)
        return sc

    def finish(i, sc):
        m = jnp.maximum(jnp.max(sc, axis=-1, keepdims=True), sink)
        pr = jnp.exp(sc - m)
        den = jnp.sum(pr, axis=-1, keepdims=True) + jnp.exp(sink - m)
        o = jnp.dot(pr.astype(BF16), band(v_ref, i), preferred_element_type=F32) / den
        for g in range(GROUP):
            o_ref[g, block_rows(i), :] = o[g * ATTN_BLOCK:(g + 1) * ATTN_BLOCK].astype(o_ref.dtype)

    for first in range(0, n_blocks, ATTN_BLOCKS_PER_GROUP):
        group = range(first, first + ATTN_BLOCKS_PER_GROUP)
        logits = [scores(i) for i in group]
        for i, sc in zip(group, logits):
            finish(i, sc)


def _attention(p16, sink, rel_table, w, batch, seq):
    q_blk0 = 3 * HGRN_HEADS // GROUP
    k_slab0 = 3 * HGRN_HEADS + ATTN_HEADS
    v_slab0 = k_slab0 + KV_HEADS
    bucket = jnp.asarray(_t5_bucket_table())
    smem = pl.BlockSpec(memory_space=pltpu.SMEM)
    w_slab = pl.BlockSpec((w.shape[0] // (batch * KV_HEADS), w.shape[1]), lambda b, x: (b * KV_HEADS + x, 0))
    return pl.pallas_call(
        _attn_kernel,
        grid=(batch, KV_HEADS),
        in_specs=[
            smem,
            smem,
            pl.BlockSpec((ATTN_BLOCK, KEY_SPAN), lambda b, x: (0, 0)),
            pl.BlockSpec((GROUP, seq, HEAD_DIM), lambda b, x: (q_blk0 + x, b, 0)),
            pl.BlockSpec((None, seq, HEAD_DIM), lambda b, x: (k_slab0 + x, b, 0)),
            pl.BlockSpec((None, seq, HEAD_DIM), lambda b, x: (v_slab0 + x, b, 0)),
            w_slab,
        ],
        out_specs=(pl.BlockSpec((GROUP, seq, HEAD_DIM), lambda b, x: (x, b, 0)), w_slab),
        out_shape=(jax.ShapeDtypeStruct((ATTN_HEADS, batch * seq, HEAD_DIM), BF16),
                   jax.ShapeDtypeStruct(w.shape, BF16)),
        scratch_shapes=[
            pltpu.VMEM((ATTN_HEADS * ATTN_BLOCK, KEY_SPAN), F32),
        ],
        compiler_params=pltpu.CompilerParams(
            dimension_semantics=("arbitrary", "arbitrary"), vmem_limit_bytes=VMEM_LIMIT_BYTES),
        name="attention",
    )(rel_table, sink, bucket, p16, p16, p16, w)


def _outproj_kernel(yh_ref, ya_ref, x_ref, wh_ref, wa_ref, g_ref, o_ref):
    y_h = jnp.concatenate([yh_ref[c] for c in range(yh_ref.shape[0])], axis=1)
    y_a = jnp.concatenate([ya_ref[c] for c in range(ya_ref.shape[0])], axis=1)
    mixed = jnp.dot(y_h, wh_ref[...], preferred_element_type=F32)
    mixed = mixed + jnp.dot(y_a, wa_ref[...], preferred_element_type=F32)
    o_ref[...] = x_ref[...] + _rms(mixed, g_ref[...])


def _outproj(y_h, y_a, x, w_out, gain, *, tm=512):
    m, d = x.shape
    return pl.pallas_call(
        _outproj_kernel,
        grid=(m // tm,),
        in_specs=[
            pl.BlockSpec((HGRN_HEADS, tm, HEAD_DIM), lambda i: (0, i, 0)),
            pl.BlockSpec((ATTN_HEADS, tm, HEAD_DIM), lambda i: (0, i, 0)),
            pl.BlockSpec((tm, d), lambda i: (i, 0)),
            pl.BlockSpec((HGRN_WIDTH, d), lambda i: (0, 0)),
            pl.BlockSpec((ATTN_WIDTH, d), lambda i: (1, 0)),
            pl.BlockSpec((1, d), lambda i: (0, 0)),
        ],
        out_specs=pl.BlockSpec((tm, d), lambda i: (i, 0)),
        out_shape=jax.ShapeDtypeStruct((m, d), F32),
        compiler_params=pltpu.CompilerParams(
            dimension_semantics=("parallel",), vmem_limit_bytes=VMEM_LIMIT_BYTES),
        name="outproj",
    )(y_h, y_a, x, w_out, w_out, gain)


def kernel(x, pre_norm_ffn1, post_norm_ffn1, w_ffn1_gate_up, w_ffn1_down, pre_norm_mix, post_norm_mix,
           w_mix_in, hgrn_lower_bounds_fwd, hgrn_lower_bounds_bwd, hgrn_out_norm, attn_sink, w_mix_out,
           pre_norm_ffn2, post_norm_ffn2, w_ffn2_gate_up, w_ffn2_down, rel_bias_table):
    batch, seq, d = x.shape
    depth = pre_norm_ffn1.shape[0]
    assert depth == 1 and d == D_MODEL
    xf = x.reshape(batch * seq, d)
    layer = 0
    xf, xn = _ffn_f32_weights(xf, pre_norm_ffn1[layer:layer + 1], post_norm_ffn1[layer:layer + 1],
                              pre_norm_mix[layer:layer + 1], w_ffn1_gate_up[layer], w_ffn1_down[layer])
    w_in = w_mix_in[layer]
    p16 = _proj(xn, w_in, [0, 1, 2, 3, 8, 9, 10, 11, 12], BF16, tm=4096)
    y_h, w2_gate_up, w2_down = _hgrn(p16, xn, w_in, hgrn_lower_bounds_fwd, hgrn_lower_bounds_bwd,
                                     hgrn_out_norm[layer:layer + 1], w_ffn2_gate_up[layer], w_ffn2_down[layer],
                                     batch, seq)
    y_a, w_out = _attention(p16, attn_sink[layer:layer + 1], rel_bias_table, w_mix_out[layer], batch, seq)
    xf = _outproj(y_h, y_a, xf, w_out, post_norm_mix[layer:layer + 1])
    xf = _ffn(xf, pre_norm_ffn2[layer:layer + 1], post_norm_ffn2[layer:layer + 1], w2_gate_up, w2_down, tf=512)
    return xf.reshape(batch, seq, d)
```

```python
import functools
import math

import jax
import jax.numpy as jnp
import numpy as np
from jax import lax
from jax.experimental import pallas as pl
from jax.experimental.pallas import tpu as pltpu

F32 = jnp.float32
BF16 = jnp.bfloat16

D_MODEL = 2048
HGRN_WIDTH = 1024
HEAD_DIM = 128
HGRN_HEADS = HGRN_WIDTH // HEAD_DIM
CHUNK = 64
ATTN_WIDTH = 1024
ATTN_HEADS = ATTN_WIDTH // HEAD_DIM
KV_HEADS = 2
GROUP = ATTN_HEADS // KV_HEADS
KV_WIDTH = KV_HEADS * HEAD_DIM
WINDOW = 128
ATTN_BLOCK = 128
KEY_SPAN = ATTN_BLOCK + 2 * WINDOW
REL_BUCKETS = 32
REL_MAX_DIST = 128
D_FF = 5632
EPS = 1e-6
NEG_INF = -1e30
LOG2_E = 1.0 / math.log(2.0)

MIB = 1024 * 1024
VMEM_LIMIT_BYTES = 60 * MIB
ROW_BLOCK = 32
EDGE_CHUNK = 256
PAIRS_PER_TRIP = 16
STAGE_LAG = 10
CHUNKS_PER_OUT_TRIP = 32
LOGIT_SLAB = 256
ATTN_BLOCKS_PER_GROUP = 2


def _rms(x, gain):
    return x * lax.rsqrt(jnp.mean(x * x, axis=-1, keepdims=True) + EPS) * gain


def _sigmoid(x):
    return 1.0 / (1.0 + jnp.exp(-x))


def _aligned(start, multiple):
    return start if isinstance(start, int) else pl.multiple_of(start, multiple)


def _static_or_fori(trips, body):
    if trips == 1:
        body(0, 0)
    else:
        lax.fori_loop(0, trips, body, 0)


def _bf16_weight(w_ref, axis):
    if len(w_ref.shape) == 3:
        return jnp.concatenate([w_ref[c] for c in range(w_ref.shape[0])], axis=axis)
    return w_ref[...].astype(BF16)


def _ffn_kernel(*refs, emit_next_norm, emit_weights, n_passthrough):
    x_ref, gpre_ref, gpost_ref = refs[:3]
    k = 3
    gnext_ref = refs[k] if emit_next_norm else None
    k += int(emit_next_norm)
    wg_ref, wu_ref, wd_ref = refs[k:k + 3]
    k += 3 + n_passthrough
    o_ref, h_ref = refs[k:k + 2]
    j = pl.program_id(1)
    tm = x_ref.shape[0]

    def pre_norm_rows(r0, r1):
        for r in range(r0, r1, ROW_BLOCK):
            rows = pl.ds(r, ROW_BLOCK)
            h_ref[rows, :] = _rms(x_ref[rows, :], gpre_ref[...]).astype(BF16)

    def finish_rows(r0, r1):
        for r in range(r0, r1, ROW_BLOCK):
            rows = pl.ds(r, ROW_BLOCK)
            new_x = x_ref[rows, :] + 0.5 * _rms(o_ref[rows, :], gpost_ref[...])
            o_ref[rows, :] = new_x
            if emit_next_norm:
                h_ref[rows, :] = _rms(new_x, gnext_ref[...]).astype(BF16)

    def swiglu(h, w_gate, w_up):
        gate = jnp.dot(h, w_gate, preferred_element_type=F32)
        up = jnp.dot(h, w_up, preferred_element_type=F32)
        return (gate * _sigmoid(gate) * up).astype(BF16)

    def step(position):
        w_gate, w_up, w_down = _bf16_weight(wg_ref, 1), _bf16_weight(wu_ref, 1), _bf16_weight(wd_ref, 0)
        if emit_weights:
            for w16_ref, w in zip(refs[k + 2:k + 5], (w_gate, w_up, w_down)):
                w16_ref[0] = w
        if position == "first":
            acts = []
            for c0 in range(0, tm, EDGE_CHUNK):
                pre_norm_rows(c0, c0 + EDGE_CHUNK)
                acts.append(swiglu(h_ref[pl.ds(c0, EDGE_CHUNK), :], w_gate, w_up))
            o_ref[...] = jnp.dot(jnp.concatenate(acts, axis=0), w_down, preferred_element_type=F32)
            return
        act = swiglu(h_ref[...], w_gate, w_up)
        if position == "middle":
            o_ref[...] += jnp.dot(act, w_down, preferred_element_type=F32)
            return
        for c0 in range(0, tm, EDGE_CHUNK):
            rows = pl.ds(c0, EDGE_CHUNK)
            o_ref[rows, :] += jnp.dot(act[c0:c0 + EDGE_CHUNK], w_down, preferred_element_type=F32)
            finish_rows(c0, c0 + EDGE_CHUNK)

    last = pl.num_programs(1) - 1
    pl.when(j == 0)(functools.partial(step, "first"))
    pl.when((j > 0) & (j < last))(functools.partial(step, "middle"))
    pl.when(j == last)(functools.partial(step, "last"))


def _ffn(x, gpre, gpost, w_gate_up, w_down, gnext=None, *, tm=1024, tf=256):
    m, d = x.shape
    nj = D_FF // tf
    emit = gnext is not None
    row_tile = pl.BlockSpec((tm, d), lambda i, j: (i, 0))
    gain_spec = pl.BlockSpec((1, d), lambda i, j: (0, 0))
    weight_specs = [
        pl.BlockSpec((d, tf), lambda i, j: (0, j)),
        pl.BlockSpec((d, tf), lambda i, j: (0, j + nj)),
        pl.BlockSpec((tf, d), lambda i, j: (j, 0)),
    ]
    gains = (gpre, gpost, gnext) if emit else (gpre, gpost)
    return pl.pallas_call(
        functools.partial(_ffn_kernel, emit_next_norm=emit, emit_weights=False, n_passthrough=0),
        grid=(m // tm, nj),
        in_specs=[row_tile] + [gain_spec] * len(gains) + weight_specs,
        out_specs=(row_tile, row_tile) if emit else row_tile,
        out_shape=((jax.ShapeDtypeStruct((m, d), F32), jax.ShapeDtypeStruct((m, d), BF16)) if emit
                   else jax.ShapeDtypeStruct((m, d), F32)),
        scratch_shapes=[] if emit else [pltpu.VMEM((tm, d), BF16)],
        compiler_params=pltpu.CompilerParams(
            dimension_semantics=("parallel", "arbitrary"), vmem_limit_bytes=VMEM_LIMIT_BYTES),
        name="ffn",
    )(x, *gains, w_gate_up, w_gate_up, w_down)


def _ffn_f32_weights(x, gpre, gpost, gnext, w_gate_up, w_down, *, tm=1024, tf=256, slabs_per_step=2):
    m, d = x.shape
    nj = D_FF // tf
    gains = (gpre, gpost, gnext)
    gain_spec = pl.BlockSpec((1, d), lambda i, j: (0, 0))
    params = pltpu.CompilerParams(dimension_semantics=("parallel", "arbitrary"), vmem_limit_bytes=VMEM_LIMIT_BYTES)
    results = (jax.ShapeDtypeStruct((m, d), F32), jax.ShapeDtypeStruct((m, d), BF16))

    first_tile = pl.BlockSpec((tm, d), lambda i, j: (0, 0))
    col_slab = pl.BlockSpec((1, d, tf), lambda i, j: (j, 0, 0))
    row_slab = pl.BlockSpec((1, tf, d), lambda i, j: (j, 0, 0))
    out, xn, wg16, wu16, wd16 = pl.pallas_call(
        functools.partial(_ffn_kernel, emit_next_norm=True, emit_weights=True, n_passthrough=0),
        grid=(1, nj),
        in_specs=[pl.BlockSpec((tm, d), lambda i, j: (0, 0), pipeline_mode=pl.Buffered(1))] + [gain_spec] * 3 + [
            pl.BlockSpec((d, tf), lambda i, j: (0, j)),
            pl.BlockSpec((d, tf), lambda i, j: (0, j + nj)),
            pl.BlockSpec((tf, d), lambda i, j: (j, 0)),
        ],
        out_specs=(first_tile, first_tile, col_slab, col_slab, row_slab),
        out_shape=results + (jax.ShapeDtypeStruct((nj, d, tf), BF16), jax.ShapeDtypeStruct((nj, d, tf), BF16),
                             jax.ShapeDtypeStruct((nj, tf, d), BF16)),
        compiler_params=params,
        name="ffn_first",
    )(x, *gains, w_gate_up, w_gate_up, w_down)

    per = slabs_per_step
    later_tile = pl.BlockSpec((tm, d), lambda i, j: (i + 1, 0))
    untouched = pl.BlockSpec(memory_space=pl.ANY)
    return pl.pallas_call(
        functools.partial(_ffn_kernel, emit_next_norm=True, emit_weights=False, n_passthrough=2),
        grid=(m // tm - 1, nj // per),
        in_specs=[later_tile] + [gain_spec] * 3 + [
            pl.BlockSpec((per, d, tf), lambda i, j: (j, 0, 0)),
            pl.BlockSpec((per, d, tf), lambda i, j: (j, 0, 0)),
            pl.BlockSpec((per, tf, d), lambda i, j: (j, 0, 0)),
            untouched, untouched,
        ],
        out_specs=(later_tile, later_tile),
        out_shape=results,
        input_output_aliases={7: 0, 8: 1},
        compiler_params=params,
        name="ffn_rest",
    )(x, *gains, wg16, wu16, wd16, out, xn)


def _proj_kernel(xn_ref, w_ref, o_ref):
    res = jnp.dot(xn_ref[...], w_ref[...].astype(BF16), preferred_element_type=F32).astype(o_ref.dtype)
    for c in range(o_ref.shape[0]):
        o_ref[c] = res[:, c * HEAD_DIM:(c + 1) * HEAD_DIM]


def _proj(xn, w_in, col_blocks, out_dtype, *, tm=2048, tn=512):
    m, d = xn.shape
    per = tn // HEAD_DIM
    jumps = [(pos, col_blocks[pos] - col_blocks[pos - 1] - 1) for pos in range(1, len(col_blocks))
             if col_blocks[pos] != col_blocks[pos - 1] + 1]

    def w_block(i, j):
        blk = j + col_blocks[0]
        for pos, gap in jumps:
            blk = blk + jnp.where(j >= pos, gap, 0)
        return (0, blk)

    return pl.pallas_call(
        _proj_kernel,
        grid=(m // tm, len(col_blocks)),
        in_specs=[
            pl.BlockSpec((tm, d), lambda i, j: (i, 0)),
            pl.BlockSpec((d, tn), w_block),
        ],
        out_specs=pl.BlockSpec((per, tm, HEAD_DIM), lambda i, j: (j, i, 0)),
        out_shape=jax.ShapeDtypeStruct((len(col_blocks) * per, m, HEAD_DIM), out_dtype),
        compiler_params=pltpu.CompilerParams(
            dimension_semantics=("parallel", "arbitrary"), vmem_limit_bytes=VMEM_LIMIT_BYTES),
        name="inproj",
    )(xn, w_in)


def _chunk_cumsum(x, row_in_chunk, reverse):
    n_rows = x.shape[0]
    c = x
    s = 1
    while s < CHUNK:
        if reverse:
            shifted = pltpu.roll(c, n_rows - s, axis=0)
            keep = row_in_chunk < CHUNK - s
        else:
            shifted = pltpu.roll(c, s, axis=0)
            keep = row_in_chunk >= s
        c = c + jnp.where(keep, shifted, 0.0)
        s *= 2
    return c


def _hgrn_kernel(q_ref, v_ref, xn_ref, wff_ref, wfb_ref, g_ref, lbf_ref, lbb_ref, gain_ref, wa_ref, wb_ref,
                 o_ref, wa16_ref, wb16_ref, qd_ref, kv_ref, dec_ref, st_ref, oi_ref):
    wa16_ref[...] = wa_ref[...].astype(BF16)
    wb16_ref[...] = wb_ref[...].astype(BF16)

    seq = q_ref.shape[0]
    n_chunks = seq // CHUNK
    pair = 2 * CHUNK
    contract_last = (((1,), (1,)), ((), ()))

    def gate_consts(lb_param_ref):
        a = lb_param_ref[...]
        e = jnp.exp(a - jnp.max(a, axis=0, keepdims=True))
        lb = e[0:1] / jnp.sum(e, axis=0, keepdims=True)
        return 0.5 * (1.0 + lb), 0.5 * (1.0 - lb)

    consts = (gate_consts(lbf_ref), gate_consts(lbb_ref))
    w_logits = jnp.concatenate([wff_ref[...].astype(BF16), wfb_ref[...].astype(BF16)], axis=1)

    r = lax.broadcasted_iota(jnp.int32, (pair, pair), 0)
    s = lax.broadcasted_iota(jnp.int32, (pair, pair), 1)
    same_chunk = (r // CHUNK) == (s // CHUNK)
    tri = (same_chunk & (r >= s), same_chunk & (s >= r))
    row_in_chunk = lax.broadcasted_iota(jnp.int32, (pair, HEAD_DIM), 0) % CHUNK
    zero_blk = jnp.zeros((CHUNK, HEAD_DIM), BF16)

    def trip_body(trip, carry):
        pairs = [trip * PAIRS_PER_TRIP + i for i in range(PAIRS_PER_TRIP)]
        rows_of = [pl.ds(_aligned(pi * pair, pair), pair) for pi in pairs]
        pairs_per_slab = LOGIT_SLAB // pair
        n_slabs = PAIRS_PER_TRIP // pairs_per_slab

        def project_logits(slab):
            start = _aligned((trip * n_slabs + slab) * LOGIT_SLAB, LOGIT_SLAB)
            return jnp.dot(xn_ref[pl.ds(start, LOGIT_SLAB), :], w_logits, preferred_element_type=F32)

        def gate_stage(i):
            slab, offset = divmod(i, pairs_per_slab)
            per_dir = []
            for d in range(2):
                mid, half = consts[d]
                z = logits[slab][offset * pair:(offset + 1) * pair, d * HEAD_DIM:(d + 1) * HEAD_DIM]
                t = jnp.tanh(0.5 * z)
                f = mid + half * t
                k = half * (1.0 - t)
                per_dir.append((k, _chunk_cumsum(jnp.log(f) * LOG2_E, row_in_chunk, reverse=(d == 1))))
            if offset == pairs_per_slab - 1 and slab + 2 < n_slabs:
                logits.append(project_logits(slab + 2))
            return per_dir

        def mix_stage(i, per_dir):
            pi, rows = pairs[i], rows_of[i]
            q = q_ref[rows, :].astype(F32)
            vb = v_ref[rows, :]
            v_t = vb.astype(F32).T.astype(BF16)
            probs = None
            for d, (k, c) in enumerate(per_dir):
                c3 = c.reshape(2, CHUNK, HEAD_DIM)
                tot = c3[:, 0:1, :] if d == 1 else c3[:, CHUNK - 1:CHUNK, :]
                k_tail = (k * jnp.exp2(tot - c3).reshape(pair, HEAD_DIM)).astype(BF16)
                q_dec = (q * jnp.exp2(c)).astype(BF16)
                k_dec = (k * jnp.exp2(-c)).astype(BF16)
                qd_ref[rows, d * HEAD_DIM:(d + 1) * HEAD_DIM] = q_dec
                dec_ref[d, pl.ds(pi * 2, 2), :] = jnp.exp2(tot).reshape(2, HEAD_DIM)
                sc = lax.dot_general(q_dec, k_dec, contract_last, preferred_element_type=F32)
                sc = jnp.where(tri[d], sc, 0.0)
                probs = sc if probs is None else probs + sc
                rhs = jnp.concatenate(
                    [jnp.concatenate([k_tail[:CHUNK], zero_blk], axis=1),
                     jnp.concatenate([zero_blk, k_tail[CHUNK:]], axis=1)], axis=0)
                kv = jnp.dot(v_t, rhs, preferred_element_type=F32)
                kv_ref[d, pi * 2] = kv[:, :HEAD_DIM]
                kv_ref[d, pi * 2 + 1] = kv[:, HEAD_DIM:]
            return probs.astype(BF16), vb

        logits = [project_logits(s) for s in range(min(2, n_slabs))]
        gates, mixes = {}, {}
        for t in range(PAIRS_PER_TRIP + 2 * STAGE_LAG):
            if t < PAIRS_PER_TRIP:
                gates[t] = gate_stage(t)
            i = t - STAGE_LAG
            if 0 <= i < PAIRS_PER_TRIP:
                mixes[i] = mix_stage(i, gates.pop(i))
            i = t - 2 * STAGE_LAG
            if 0 <= i < PAIRS_PER_TRIP:
                probs, vb = mixes.pop(i)
                oi_ref[rows_of[i], :] = jnp.dot(probs, vb, preferred_element_type=F32)
        return carry

    _static_or_fori(n_chunks // (2 * PAIRS_PER_TRIP), trip_body)

    def scan_body(t, carry):
        sf, sb = carry
        nf, nb = t, n_chunks - 1 - t
        st_ref[nf, :, :HEAD_DIM] = sf.astype(BF16)
        st_ref[nb, :, HEAD_DIM:] = sb.astype(BF16)
        sf = sf * dec_ref[0, pl.ds(nf, 1), :] + kv_ref[0, nf]
        sb = sb * dec_ref[1, pl.ds(nb, 1), :] + kv_ref[1, nb]
        return sf, sb

    states = (jnp.zeros((HEAD_DIM, HEAD_DIM), F32),) * 2
    for t in range(n_chunks):
        states = scan_body(t, states)

    def out_body(trip, carry):
        chunks = [trip * CHUNKS_PER_OUT_TRIP + i for i in range(CHUNKS_PER_OUT_TRIP)]
        rows_of = [pl.ds(_aligned(n * CHUNK, CHUNK), CHUNK) for n in chunks]
        inter = [lax.dot_general(qd_ref[rows, :], st_ref[n], contract_last, preferred_element_type=F32)
                 for n, rows in zip(chunks, rows_of)]
        for rows, o_inter in zip(rows_of, inter):
            o = _rms(oi_ref[rows, :] + o_inter, gain_ref[...])
            g = g_ref[rows, :].astype(F32)
            o_ref[rows, :] = (o * (0.5 * g * (1.0 + jnp.tanh(0.5 * g)))).astype(o_ref.dtype)
        return carry

    _static_or_fori(n_chunks // CHUNKS_PER_OUT_TRIP, out_body)


def _hgrn(p16, xn, w_in, lbf, lbb, out_gain, w_a, w_b, batch, seq):
    hw = HGRN_HEADS
    steps = batch * hw
    d = xn.shape[1]

    def w_col(group):
        return lambda b, h: (0, group * hw + h)

    def slab(group):
        return lambda b, h: (group * hw + h, b, 0)

    def row_slab(w):
        return pl.BlockSpec((w.shape[0] // steps, w.shape[1]), lambda b, h: (b * hw + h, 0))

    blk = (None, seq, HEAD_DIM)
    return pl.pallas_call(
        _hgrn_kernel,
        grid=(batch, hw),
        in_specs=[
            pl.BlockSpec(blk, slab(0)),
            pl.BlockSpec(blk, slab(1)),
            pl.BlockSpec((seq, d), lambda b, h: (b, 0)),
            pl.BlockSpec((d, HEAD_DIM), w_col(2)),
            pl.BlockSpec((d, HEAD_DIM), w_col(3)),
            pl.BlockSpec(blk, slab(2)),
            pl.BlockSpec((lbf.shape[0], HEAD_DIM), lambda b, h: (0, h)),
            pl.BlockSpec((lbb.shape[0], HEAD_DIM), lambda b, h: (0, h)),
            pl.BlockSpec((1, HEAD_DIM), lambda b, h: (0, h)),
            row_slab(w_a),
            row_slab(w_b),
        ],
        out_specs=(pl.BlockSpec(blk, lambda b, h: (h, b, 0)), row_slab(w_a), row_slab(w_b)),
        out_shape=(jax.ShapeDtypeStruct((hw, batch * seq, HEAD_DIM), BF16),
                   jax.ShapeDtypeStruct(w_a.shape, BF16), jax.ShapeDtypeStruct(w_b.shape, BF16)),
        scratch_shapes=[
            pltpu.VMEM((seq, 2 * HEAD_DIM), BF16),
            pltpu.VMEM((2, seq // CHUNK, HEAD_DIM, HEAD_DIM), F32),
            pltpu.VMEM((2, seq // CHUNK, HEAD_DIM), F32),
            pltpu.VMEM((seq // CHUNK, HEAD_DIM, 2 * HEAD_DIM), BF16),
            pltpu.VMEM((seq, HEAD_DIM), F32),
        ],
        compiler_params=pltpu.CompilerParams(
            dimension_semantics=("parallel", "parallel"), vmem_limit_bytes=VMEM_LIMIT_BYTES),
        name="hgrn",
    )(p16, p16, xn, w_in, w_in, p16, lbf, lbb, out_gain, w_a, w_b)


def _t5_bucket_table():
    nb = REL_BUCKETS // 2
    max_exact = nb // 2
    c = np.arange(ATTN_BLOCK)[:, None]
    s = np.arange(KEY_SPAN)[None, :]
    rel = s - WINDOW - c
    bucket = (rel > 0).astype(np.int32) * nb
    n = np.abs(rel)
    large = max_exact + (np.log(np.maximum(n, 1) / max_exact) / np.log(REL_MAX_DIST / max_exact)
                         * (nb - max_exact)).astype(np.int32)
    large = np.minimum(large, nb - 1)
    bucket = bucket + np.where(n < max_exact, n, large).astype(np.int32)
    return np.where(np.abs(rel) <= WINDOW, bucket, -1).astype(np.int32)


def _attn_kernel(table_ref, sink_ref, bucket_ref, q_ref, k_ref, v_ref, w_ref, o_ref, w16_ref, bias_ref):
    w16_ref[...] = w_ref[...].astype(BF16)

    x = pl.program_id(1)
    seq = k_ref.shape[0]
    n_blocks = seq // ATTN_BLOCK
    scale = 1.0 / math.sqrt(HEAD_DIM)
    rows = GROUP * ATTN_BLOCK

    @pl.when((pl.program_id(0) == 0) & (x == 0))
    def _():
        bucket = bucket_ref[...]
        for head in range(ATTN_HEADS):
            bias = jnp.full((ATTN_BLOCK, KEY_SPAN), NEG_INF, F32)
            for b in range(REL_BUCKETS):
                bias = jnp.where(bucket == b, table_ref[b, head], bias)
            bias_ref[pl.ds(head * ATTN_BLOCK, ATTN_BLOCK), :] = bias

    row_head = lax.broadcasted_iota(jnp.int32, (rows, 1), 0) // ATTN_BLOCK
    sink = jnp.zeros((rows, 1), F32)
    for g in range(GROUP):
        sink = jnp.where(row_head == g, sink_ref[0, x * GROUP + g], sink)
    bias_rows = pl.ds(pl.multiple_of(x * rows, rows), rows)

    def block_rows(i):
        return pl.ds(i * ATTN_BLOCK, ATTN_BLOCK)

    def band(ref, i):
        return jnp.concatenate([ref[block_rows(max(i - 1, 0)), :], ref[block_rows(i), :],
                                ref[block_rows(min(i + 1, n_blocks - 1)), :]], axis=0)

    def scores(i):
        qs = jnp.concatenate([q_ref[g, block_rows(i), :] for g in range(GROUP)], axis=0)
        sc = lax.dot_general(qs, band(k_ref, i), (((1,), (1,)), ((), ())), preferred_element_type=F32)
        sc = sc * scale + bias_ref[bias_rows, :]
        if i in (0, n_blocks - 1):
            key_pos = (i - 1) * ATTN_BLOCK + lax.broadcasted_iota(jnp.int32, (rows, KEY_SPAN), 1)
            sc = jnp.where((key_pos >= 0) & (key_pos < seq), sc, NEG_INF)
        return sc

    def finish(i, sc):
        m = jnp.maximum(jnp.max(sc, axis=-1, keepdims=True), sink)
        pr = jnp.exp(sc - m)
        den = jnp.sum(pr, axis=-1, keepdims=True) + jnp.exp(sink - m)
        o = jnp.dot(pr.astype(BF16), band(v_ref, i), preferred_element_type=F32) / den
        for g in range(GROUP):
            o_ref[g, block_rows(i), :] = o[g * ATTN_BLOCK:(g + 1) * ATTN_BLOCK].astype(o_ref.dtype)

    for first in range(0, n_blocks, ATTN_BLOCKS_PER_GROUP):
        group = range(first, first + ATTN_BLOCKS_PER_GROUP)
        logits = [scores(i) for i in group]
        for i, sc in zip(group, logits):
            finish(i, sc)


def _attention(p16, sink, rel_table, w, batch, seq):
    q_blk0 = 3 * HGRN_HEADS // GROUP
    k_slab0 = 3 * HGRN_HEADS + ATTN_HEADS
    v_slab0 = k_slab0 + KV_HEADS
    bucket = jnp.asarray(_t5_bucket_table())
    smem = pl.BlockSpec(memory_space=pltpu.SMEM)
    w_slab = pl.BlockSpec((w.shape[0] // (batch * KV_HEADS), w.shape[1]), lambda b, x: (b * KV_HEADS + x, 0))
    return pl.pallas_call(
        _attn_kernel,
        grid=(batch, KV_HEADS),
        in_specs=[
            smem,
            smem,
            pl.BlockSpec((ATTN_BLOCK, KEY_SPAN), lambda b, x: (0, 0)),
            pl.BlockSpec((GROUP, seq, HEAD_DIM), lambda b, x: (q_blk0 + x, b, 0)),
            pl.BlockSpec((None, seq, HEAD_DIM), lambda b, x: (k_slab0 + x, b, 0)),
            pl.BlockSpec((None, seq, HEAD_DIM), lambda b, x: (v_slab0 + x, b, 0)),
            w_slab,
        ],
        out_specs=(pl.BlockSpec((GROUP, seq, HEAD_DIM), lambda b, x: (x, b, 0)), w_slab),
        out_shape=(jax.ShapeDtypeStruct((ATTN_HEADS, batch * seq, HEAD_DIM), BF16),
                   jax.ShapeDtypeStruct(w.shape, BF16)),
        scratch_shapes=[
            pltpu.VMEM((ATTN_HEADS * ATTN_BLOCK, KEY_SPAN), F32),
        ],
        compiler_params=pltpu.CompilerParams(
            dimension_semantics=("arbitrary", "arbitrary"), vmem_limit_bytes=VMEM_LIMIT_BYTES),
        name="attention",
    )(rel_table, sink, bucket, p16, p16, p16, w)


def _outproj_kernel(yh_ref, ya_ref, x_ref, wh_ref, wa_ref, g_ref, o_ref):
    y_h = jnp.concatenate([yh_ref[c] for c in range(yh_ref.shape[0])], axis=1)
    y_a = jnp.concatenate([ya_ref[c] for c in range(ya_ref.shape[0])], axis=1)
    mixed = jnp.dot(y_h, wh_ref[...], preferred_element_type=F32)
    mixed = mixed + jnp.dot(y_a, wa_ref[...], preferred_element_type=F32)
    o_ref[...] = x_ref[...] + _rms(mixed, g_ref[...])


def _outproj(y_h, y_a, x, w_out, gain, *, tm=512):
    m, d = x.shape
    return pl.pallas_call(
        _outproj_kernel,
        grid=(m // tm,),
        in_specs=[
            pl.BlockSpec((HGRN_HEADS, tm, HEAD_DIM), lambda i: (0, i, 0)),
            pl.BlockSpec((ATTN_HEADS, tm, HEAD_DIM), lambda i: (0, i, 0)),
            pl.BlockSpec((tm, d), lambda i: (i, 0)),
            pl.BlockSpec((HGRN_WIDTH, d), lambda i: (0, 0)),
            pl.BlockSpec((ATTN_WIDTH, d), lambda i: (1, 0)),
            pl.BlockSpec((1, d), lambda i: (0, 0)),
        ],
        out_specs=pl.BlockSpec((tm, d), lambda i: (i, 0)),
        out_shape=jax.ShapeDtypeStruct((m, d), F32),
        compiler_params=pltpu.CompilerParams(
            dimension_semantics=("parallel",), vmem_limit_bytes=VMEM_LIMIT_BYTES),
        name="outproj",
    )(y_h, y_a, x, w_out, w_out, gain)


def kernel(x, pre_norm_ffn1, post_norm_ffn1, w_ffn1_gate_up, w_ffn1_down, pre_norm_mix, post_norm_mix,
           w_mix_in, hgrn_lower_bounds_fwd, hgrn_lower_bounds_bwd, hgrn_out_norm, attn_sink, w_mix_out,
           pre_norm_ffn2, post_norm_ffn2, w_ffn2_gate_up, w_ffn2_down, rel_bias_table):
    batch, seq, d = x.shape
    depth = pre_norm_ffn1.shape[0]
    assert depth == 1 and d == D_MODEL
    xf = x.reshape(batch * seq, d)
    layer = 0
    xf, xn = _ffn_f32_weights(xf, pre_norm_ffn1[layer:layer + 1], post_norm_ffn1[layer:layer + 1],
                              pre_norm_mix[layer:layer + 1], w_ffn1_gate_up[layer], w_ffn1_down[layer])
    w_in = w_mix_in[layer]
    p16 = _proj(xn, w_in, [0, 1, 2, 3, 8, 9, 10, 11, 12], BF16, tm=4096)
    y_h, w2_gate_up, w2_down = _hgrn(p16, xn, w_in, hgrn_lower_bounds_fwd, hgrn_lower_bounds_bwd,
                                     hgrn_out_norm[layer:layer + 1], w_ffn2_gate_up[layer], w_ffn2_down[layer],
                                     batch, seq)
    y_a, w_out = _attention(p16, attn_sink[layer:layer + 1], rel_bias_table, w_mix_out[layer], batch, seq)
    xf = _outproj(y_h, y_a, xf, w_out, post_norm_mix[layer:layer + 1])
    xf = _ffn(xf, pre_norm_ffn2[layer:layer + 1], post_norm_ffn2[layer:layer + 1], w2_gate_up, w2_down, tf=512)
    return xf.reshape(batch, seq, d)
```

```python
import functools
import math

import jax
import jax.numpy as jnp
import numpy as np
from jax import lax
from jax.experimental import pallas as pl
from jax.experimental.pallas import tpu as pltpu

F32 = jnp.float32
BF16 = jnp.bfloat16

D_MODEL = 2048
HGRN_WIDTH = 1024
HEAD_DIM = 128
HGRN_HEADS = HGRN_WIDTH // HEAD_DIM
CHUNK = 64
ATTN_WIDTH = 1024
ATTN_HEADS = ATTN_WIDTH // HEAD_DIM
KV_HEADS = 2
GROUP = ATTN_HEADS // KV_HEADS
KV_WIDTH = KV_HEADS * HEAD_DIM
WINDOW = 128
ATTN_BLOCK = 128
KEY_SPAN = ATTN_BLOCK + 2 * WINDOW
REL_BUCKETS = 32
REL_MAX_DIST = 128
D_FF = 5632
EPS = 1e-6
NEG_INF = -1e30
LOG2_E = 1.0 / math.log(2.0)

MIB = 1024 * 1024
VMEM_LIMIT_BYTES = 60 * MIB
ROW_BLOCK = 32
EDGE_CHUNK = 256
STAGE_LAG = 10
LOGIT_SLAB = 256
ATTN_BLOCKS_PER_GROUP = 2


def _rms(x, gain):
    return x * lax.rsqrt(jnp.mean(x * x, axis=-1, keepdims=True) + EPS) * gain


def _sigmoid(x):
    return 1.0 / (1.0 + jnp.exp(-x))


def _bf16_weight(w_ref, axis):
    if len(w_ref.shape) == 3:
        return jnp.concatenate([w_ref[c] for c in range(w_ref.shape[0])], axis=axis)
    return w_ref[...].astype(BF16)


def _ffn_kernel(*refs, emit_next_norm, emit_weights, n_passthrough):
    x_ref, gpre_ref, gpost_ref = refs[:3]
    k = 3
    gnext_ref = refs[k] if emit_next_norm else None
    k += int(emit_next_norm)
    wg_ref, wu_ref, wd_ref = refs[k:k + 3]
    k += 3 + n_passthrough
    o_ref, h_ref = refs[k:k + 2]
    j = pl.program_id(1)
    tm = x_ref.shape[0]

    def pre_norm_rows(r0, r1):
        for r in range(r0, r1, ROW_BLOCK):
            rows = pl.ds(r, ROW_BLOCK)
            h_ref[rows, :] = _rms(x_ref[rows, :], gpre_ref[...]).astype(BF16)

    def finish_rows(r0, r1):
        for r in range(r0, r1, ROW_BLOCK):
            rows = pl.ds(r, ROW_BLOCK)
            new_x = x_ref[rows, :] + 0.5 * _rms(o_ref[rows, :], gpost_ref[...])
            o_ref[rows, :] = new_x
            if emit_next_norm:
                h_ref[rows, :] = _rms(new_x, gnext_ref[...]).astype(BF16)

    def swiglu(h, w_gate, w_up):
        gate = jnp.dot(h, w_gate, preferred_element_type=F32)
        up = jnp.dot(h, w_up, preferred_element_type=F32)
        return (gate * _sigmoid(gate) * up).astype(BF16)

    def step(position):
        w_gate, w_up, w_down = _bf16_weight(wg_ref, 1), _bf16_weight(wu_ref, 1), _bf16_weight(wd_ref, 0)
        if emit_weights:
            for w16_ref, w in zip(refs[k + 2:k + 5], (w_gate, w_up, w_down)):
                w16_ref[0] = w
        if position == "first":
            acts = []
            for c0 in range(0, tm, EDGE_CHUNK):
                pre_norm_rows(c0, c0 + EDGE_CHUNK)
                acts.append(swiglu(h_ref[pl.ds(c0, EDGE_CHUNK), :], w_gate, w_up))
            o_ref[...] = jnp.dot(jnp.concatenate(acts, axis=0), w_down, preferred_element_type=F32)
            return
        act = swiglu(h_ref[...], w_gate, w_up)
        if position == "middle":
            o_ref[...] += jnp.dot(act, w_down, preferred_element_type=F32)
            return
        for c0 in range(0, tm, EDGE_CHUNK):
            rows = pl.ds(c0, EDGE_CHUNK)
            o_ref[rows, :] += jnp.dot(act[c0:c0 + EDGE_CHUNK], w_down, preferred_element_type=F32)
            finish_rows(c0, c0 + EDGE_CHUNK)

    last = pl.num_programs(1) - 1
    pl.when(j == 0)(functools.partial(step, "first"))
    pl.when((j > 0) & (j < last))(functools.partial(step, "middle"))
    pl.when(j == last)(functools.partial(step, "last"))


def _ffn(x, gpre, gpost, w_gate_up, w_down, *, tm=1024, tf=512):
    m, d = x.shape
    nj = D_FF // tf
    row_tile = pl.BlockSpec((tm, d), lambda i, j: (i, 0))
    gain_spec = pl.BlockSpec((1, d), lambda i, j: (0, 0))
    return pl.pallas_call(
        functools.partial(_ffn_kernel, emit_next_norm=False, emit_weights=False, n_passthrough=0),
        grid=(m // tm, nj),
        in_specs=[row_tile, gain_spec, gain_spec,
                  pl.BlockSpec((d, tf), lambda i, j: (0, j)),
                  pl.BlockSpec((d, tf), lambda i, j: (0, j + nj)),
                  pl.BlockSpec((tf, d), lambda i, j: (j, 0))],
        out_specs=row_tile,
        out_shape=jax.ShapeDtypeStruct((m, d), F32),
        scratch_shapes=[pltpu.VMEM((tm, d), BF16)],
        compiler_params=pltpu.CompilerParams(
            dimension_semantics=("parallel", "arbitrary"), vmem_limit_bytes=VMEM_LIMIT_BYTES),
        name="ffn",
    )(x, gpre, gpost, w_gate_up, w_gate_up, w_down)


def _ffn_f32_weights(x, gpre, gpost, gnext, w_gate_up, w_down, *, tm=1024, tf=256, slabs_per_step=2):
    m, d = x.shape
    nj = D_FF // tf
    gains = (gpre, gpost, gnext)
    gain_spec = pl.BlockSpec((1, d), lambda i, j: (0, 0))
    params = pltpu.CompilerParams(dimension_semantics=("parallel", "arbitrary"), vmem_limit_bytes=VMEM_LIMIT_BYTES)
    results = (jax.ShapeDtypeStruct((m, d), F32), jax.ShapeDtypeStruct((m, d), BF16))

    first_tile = pl.BlockSpec((tm, d), lambda i, j: (0, 0))
    col_slab = pl.BlockSpec((1, d, tf), lambda i, j: (j, 0, 0))
    row_slab = pl.BlockSpec((1, tf, d), lambda i, j: (j, 0, 0))
    out, xn, wg16, wu16, wd16 = pl.pallas_call(
        functools.partial(_ffn_kernel, emit_next_norm=True, emit_weights=True, n_passthrough=0),
        grid=(1, nj),
        in_specs=[pl.BlockSpec((tm, d), lambda i, j: (0, 0), pipeline_mode=pl.Buffered(1))] + [gain_spec] * 3 + [
            pl.BlockSpec((d, tf), lambda i, j: (0, j)),
            pl.BlockSpec((d, tf), lambda i, j: (0, j + nj)),
            pl.BlockSpec((tf, d), lambda i, j: (j, 0)),
        ],
        out_specs=(first_tile, first_tile, col_slab, col_slab, row_slab),
        out_shape=results + (jax.ShapeDtypeStruct((nj, d, tf), BF16), jax.ShapeDtypeStruct((nj, d, tf), BF16),
                             jax.ShapeDtypeStruct((nj, tf, d), BF16)),
        compiler_params=params,
        name="ffn_first",
    )(x, *gains, w_gate_up, w_gate_up, w_down)

    per = slabs_per_step
    later_tile = pl.BlockSpec((tm, d), lambda i, j: (i + 1, 0))
    untouched = pl.BlockSpec(memory_space=pl.ANY)
    return pl.pallas_call(
        functools.partial(_ffn_kernel, emit_next_norm=True, emit_weights=False, n_passthrough=2),
        grid=(m // tm - 1, nj // per),
        in_specs=[later_tile] + [gain_spec] * 3 + [
            pl.BlockSpec((per, d, tf), lambda i, j: (j, 0, 0)),
            pl.BlockSpec((per, d, tf), lambda i, j: (j, 0, 0)),
            pl.BlockSpec((per, tf, d), lambda i, j: (j, 0, 0)),
            untouched, untouched,
        ],
        out_specs=(later_tile, later_tile),
        out_shape=results,
        input_output_aliases={7: 0, 8: 1},
        compiler_params=params,
        name="ffn_rest",
    )(x, *gains, wg16, wu16, wd16, out, xn)


def _proj_kernel(xn_ref, w_ref, o_ref):
    res = jnp.dot(xn_ref[...], w_ref[...].astype(BF16), preferred_element_type=F32).astype(BF16)
    for c in range(o_ref.shape[0]):
        o_ref[c] = res[:, c * HEAD_DIM:(c + 1) * HEAD_DIM]


def _proj(xn, w_in, col_blocks, *, tm=4096, tn=512):
    m, d = xn.shape
    per = tn // HEAD_DIM
    jumps = [(pos, col_blocks[pos] - col_blocks[pos - 1] - 1) for pos in range(1, len(col_blocks))
             if col_blocks[pos] != col_blocks[pos - 1] + 1]

    def w_block(i, j):
        blk = j + col_blocks[0]
        for pos, gap in jumps:
            blk = blk + jnp.where(j >= pos, gap, 0)
        return (0, blk)

    return pl.pallas_call(
        _proj_kernel,
        grid=(m // tm, len(col_blocks)),
        in_specs=[
            pl.BlockSpec((tm, d), lambda i, j: (i, 0)),
            pl.BlockSpec((d, tn), w_block),
        ],
        out_specs=pl.BlockSpec((per, tm, HEAD_DIM), lambda i, j: (j, i, 0)),
        out_shape=jax.ShapeDtypeStruct((len(col_blocks) * per, m, HEAD_DIM), BF16),
        compiler_params=pltpu.CompilerParams(
            dimension_semantics=("parallel", "arbitrary"), vmem_limit_bytes=VMEM_LIMIT_BYTES),
        name="inproj",
    )(xn, w_in)


def _chunk_cumsum(x, row_in_chunk, reverse):
    n_rows = x.shape[0]
    c = x
    s = 1
    while s < CHUNK:
        if reverse:
            shifted = pltpu.roll(c, n_rows - s, axis=0)
            keep = row_in_chunk < CHUNK - s
        else:
            shifted = pltpu.roll(c, s, axis=0)
            keep = row_in_chunk >= s
        c = c + jnp.where(keep, shifted, 0.0)
        s *= 2
    return c


def _hgrn_kernel(q_ref, v_ref, xn_ref, wff_ref, wfb_ref, g_ref, lbf_ref, lbb_ref, gain_ref, wa_ref, wb_ref,
                 o_ref, wa16_ref, wb16_ref, qd_ref, kv_ref, dec_ref, st_ref, oi_ref):
    wa16_ref[...] = wa_ref[...].astype(BF16)
    wb16_ref[...] = wb_ref[...].astype(BF16)

    seq = q_ref.shape[0]
    n_chunks = seq // CHUNK
    pair = 2 * CHUNK
    contract_last = (((1,), (1,)), ((), ()))

    def gate_consts(lb_param_ref):
        a = lb_param_ref[...]
        e = jnp.exp(a - jnp.max(a, axis=0, keepdims=True))
        lb = e[0:1] / jnp.sum(e, axis=0, keepdims=True)
        return 0.5 * (1.0 + lb), 0.5 * (1.0 - lb)

    consts = (gate_consts(lbf_ref), gate_consts(lbb_ref))
    w_logits = jnp.concatenate([wff_ref[...].astype(BF16), wfb_ref[...].astype(BF16)], axis=1)

    r = lax.broadcasted_iota(jnp.int32, (pair, pair), 0)
    s = lax.broadcasted_iota(jnp.int32, (pair, pair), 1)
    same_chunk = (r // CHUNK) == (s // CHUNK)
    tri = (same_chunk & (r >= s), same_chunk & (s >= r))
    row_in_chunk = lax.broadcasted_iota(jnp.int32, (pair, HEAD_DIM), 0) % CHUNK
    zero_blk = jnp.zeros((CHUNK, HEAD_DIM), BF16)

    n_pairs = n_chunks // 2
    rows_of = [pl.ds(i * pair, pair) for i in range(n_pairs)]
    pairs_per_slab = LOGIT_SLAB // pair
    n_slabs = seq // LOGIT_SLAB

    def project_logits(slab):
        return jnp.dot(xn_ref[pl.ds(slab * LOGIT_SLAB, LOGIT_SLAB), :], w_logits, preferred_element_type=F32)

    def gate_stage(i):
        slab, offset = divmod(i, pairs_per_slab)
        per_dir = []
        for d in range(2):
            mid, half = consts[d]
            z = logits[slab][offset * pair:(offset + 1) * pair, d * HEAD_DIM:(d + 1) * HEAD_DIM]
            t = jnp.tanh(0.5 * z)
            f = mid + half * t
            k = half * (1.0 - t)
            per_dir.append((k, _chunk_cumsum(jnp.log(f) * LOG2_E, row_in_chunk, reverse=(d == 1))))
        if offset == pairs_per_slab - 1 and slab + 2 < n_slabs:
            logits.append(project_logits(slab + 2))
        return per_dir

    def mix_stage(i, per_dir):
        rows = rows_of[i]
        q = q_ref[rows, :].astype(F32)
        vb = v_ref[rows, :]
        v_t = vb.astype(F32).T.astype(BF16)
        probs = None
        for d, (k, c) in enumerate(per_dir):
            c3 = c.reshape(2, CHUNK, HEAD_DIM)
            tot = c3[:, 0:1, :] if d == 1 else c3[:, CHUNK - 1:CHUNK, :]
            k_tail = (k * jnp.exp2(tot - c3).reshape(pair, HEAD_DIM)).astype(BF16)
            q_dec = (q * jnp.exp2(c)).astype(BF16)
            k_dec = (k * jnp.exp2(-c)).astype(BF16)
            qd_ref[rows, d * HEAD_DIM:(d + 1) * HEAD_DIM] = q_dec
            dec_ref[d, pl.ds(i * 2, 2), :] = jnp.exp2(tot).reshape(2, HEAD_DIM)
            sc = lax.dot_general(q_dec, k_dec, contract_last, preferred_element_type=F32)
            sc = jnp.where(tri[d], sc, 0.0)
            probs = sc if probs is None else probs + sc
            rhs = jnp.concatenate(
                [jnp.concatenate([k_tail[:CHUNK], zero_blk], axis=1),
                 jnp.concatenate([zero_blk, k_tail[CHUNK:]], axis=1)], axis=0)
            kv = jnp.dot(v_t, rhs, preferred_element_type=F32)
            kv_ref[d, i * 2] = kv[:, :HEAD_DIM]
            kv_ref[d, i * 2 + 1] = kv[:, HEAD_DIM:]
        return probs.astype(BF16), vb

    logits = [project_logits(s) for s in range(min(2, n_slabs))]
    gates, mixes = {}, {}
    for t in range(n_pairs + 2 * STAGE_LAG):
        if t < n_pairs:
            gates[t] = gate_stage(t)
        i = t - STAGE_LAG
        if 0 <= i < n_pairs:
            mixes[i] = mix_stage(i, gates.pop(i))
        i = t - 2 * STAGE_LAG
        if 0 <= i < n_pairs:
            probs, vb = mixes.pop(i)
            oi_ref[rows_of[i], :] = jnp.dot(probs, vb, preferred_element_type=F32)

    sf = sb = jnp.zeros((HEAD_DIM, HEAD_DIM), F32)
    for nf in range(n_chunks):
        nb = n_chunks - 1 - nf
        st_ref[nf, :, :HEAD_DIM] = sf.astype(BF16)
        st_ref[nb, :, HEAD_DIM:] = sb.astype(BF16)
        sf = sf * dec_ref[0, pl.ds(nf, 1), :] + kv_ref[0, nf]
        sb = sb * dec_ref[1, pl.ds(nb, 1), :] + kv_ref[1, nb]

    chunk_rows = [pl.ds(n * CHUNK, CHUNK) for n in range(n_chunks)]
    inter = [lax.dot_general(qd_ref[rows, :], st_ref[n], contract_last, preferred_element_type=F32)
             for n, rows in enumerate(chunk_rows)]
    for rows, o_inter in zip(chunk_rows, inter):
        o = _rms(oi_ref[rows, :] + o_inter, gain_ref[...])
        g = g_ref[rows, :].astype(F32)
        o_ref[rows, :] = (o * (0.5 * g * (1.0 + jnp.tanh(0.5 * g)))).astype(o_ref.dtype)


def _hgrn(p16, xn, w_in, lbf, lbb, out_gain, w_a, w_b, batch, seq):
    hw = HGRN_HEADS
    steps = batch * hw
    d = xn.shape[1]

    def w_col(group):
        return lambda b, h: (0, group * hw + h)

    def slab(group):
        return lambda b, h: (group * hw + h, b, 0)

    def row_slab(w):
        assert w.shape[0] % (steps * 16) == 0
        return pl.BlockSpec((w.shape[0] // steps, w.shape[1]), lambda b, h: (b * hw + h, 0))

    blk = (None, seq, HEAD_DIM)
    return pl.pallas_call(
        _hgrn_kernel,
        grid=(batch, hw),
        in_specs=[
            pl.BlockSpec(blk, slab(0)),
            pl.BlockSpec(blk, slab(1)),
            pl.BlockSpec((seq, d), lambda b, h: (b, 0)),
            pl.BlockSpec((d, HEAD_DIM), w_col(2)),
            pl.BlockSpec((d, HEAD_DIM), w_col(3)),
            pl.BlockSpec(blk, slab(2)),
            pl.BlockSpec((lbf.shape[0], HEAD_DIM), lambda b, h: (0, h)),
            pl.BlockSpec((lbb.shape[0], HEAD_DIM), lambda b, h: (0, h)),
            pl.BlockSpec((1, HEAD_DIM), lambda b, h: (0, h)),
            row_slab(w_a),
            row_slab(w_b),
        ],
        out_specs=(pl.BlockSpec(blk, lambda b, h: (h, b, 0)), row_slab(w_a), row_slab(w_b)),
        out_shape=(jax.ShapeDtypeStruct((hw, batch * seq, HEAD_DIM), BF16),
                   jax.ShapeDtypeStruct(w_a.shape, BF16), jax.ShapeDtypeStruct(w_b.shape, BF16)),
        scratch_shapes=[
            pltpu.VMEM((seq, 2 * HEAD_DIM), BF16),
            pltpu.VMEM((2, seq // CHUNK, HEAD_DIM, HEAD_DIM), F32),
            pltpu.VMEM((2, seq // CHUNK, HEAD_DIM), F32),
            pltpu.VMEM((seq // CHUNK, HEAD_DIM, 2 * HEAD_DIM), BF16),
            pltpu.VMEM((seq, HEAD_DIM), F32),
        ],
        compiler_params=pltpu.CompilerParams(
            dimension_semantics=("parallel", "parallel"), vmem_limit_bytes=VMEM_LIMIT_BYTES),
        name="hgrn",
    )(p16, p16, xn, w_in, w_in, p16, lbf, lbb, out_gain, w_a, w_b)


def _t5_bucket_table():
    nb = REL_BUCKETS // 2
    max_exact = nb // 2
    c = np.arange(ATTN_BLOCK)[:, None]
    s = np.arange(KEY_SPAN)[None, :]
    rel = s - WINDOW - c
    bucket = (rel > 0).astype(np.int32) * nb
    n = np.abs(rel)
    large = max_exact + (np.log(np.maximum(n, 1) / max_exact) / np.log(REL_MAX_DIST / max_exact)
                         * (nb - max_exact)).astype(np.int32)
    large = np.minimum(large, nb - 1)
    bucket = bucket + np.where(n < max_exact, n, large).astype(np.int32)
    return np.where(np.abs(rel) <= WINDOW, bucket, -1).astype(np.int32)


def _attn_kernel(table_ref, sink_ref, bucket_ref, q_ref, k_ref, v_ref, w_ref, o_ref, w16_ref, bias_ref):
    w16_ref[...] = w_ref[...].astype(BF16)

    x = pl.program_id(1)
    seq = k_ref.shape[0]
    n_blocks = seq // ATTN_BLOCK
    scale = 1.0 / math.sqrt(HEAD_DIM)
    rows = GROUP * ATTN_BLOCK

    @pl.when((pl.program_id(0) == 0) & (x == 0))
    def _():
        bucket = bucket_ref[...]
        for head in range(ATTN_HEADS):
            bias = jnp.full((ATTN_BLOCK, KEY_SPAN), NEG_INF, F32)
            for b in range(REL_BUCKETS):
                bias = jnp.where(bucket == b, table_ref[b, head], bias)
            bias_ref[pl.ds(head * ATTN_BLOCK, ATTN_BLOCK), :] = bias

    row_head = lax.broadcasted_iota(jnp.int32, (rows, 1), 0) // ATTN_BLOCK
    sink = jnp.zeros((rows, 1), F32)
    for g in range(GROUP):
        sink = jnp.where(row_head == g, sink_ref[0, x * GROUP + g], sink)
    bias_rows = pl.ds(pl.multiple_of(x * rows, rows), rows)

    def block_rows(i):
        return pl.ds(i * ATTN_BLOCK, ATTN_BLOCK)

    def band(ref, i):
        return jnp.concatenate([ref[block_rows(max(i - 1, 0)), :], ref[block_rows(i), :],
                                ref[block_rows(min(i + 1, n_blocks - 1)), :]], axis=0)

    def scores(i):
        qs = jnp.concatenate([q_ref[g, block_rows(i), :] for g in range(GROUP)], axis=0)
        sc = lax.dot_general(qs, band(k_ref, i), (((1,), (1,)), ((), ())), preferred_element_type=F32)
        sc = sc * scale + bias_ref[bias_rows, :]
        if i in (0, n_blocks - 1):
            key_pos = (i - 1) * ATTN_BLOCK + lax.broadcasted_iota(jnp.int32, (rows, KEY_SPAN), 1)
            sc = jnp.where((key_pos >= 0) & (key_pos < seq), sc, NEG_INF)
        return sc

    def finish(i, sc):
        m = jnp.maximum(jnp.max(sc, axis=-1, keepdims=True), sink)
        pr = jnp.exp(sc - m)
        den = jnp.sum(pr, axis=-1, keepdims=True) + jnp.exp(sink - m)
        o = jnp.dot(pr.astype(BF16), band(v_ref, i), preferred_element_type=F32) / den
        for g in range(GROUP):
            o_ref[g, block_rows(i), :] = o[g * ATTN_BLOCK:(g + 1) * ATTN_BLOCK].astype(o_ref.dtype)

    for first in range(0, n_blocks, ATTN_BLOCKS_PER_GROUP):
        group = range(first, first + ATTN_BLOCKS_PER_GROUP)
        logits = [scores(i) for i in group]
        for i, sc in zip(group, logits):
            finish(i, sc)


def _attention(p16, sink, rel_table, w, batch, seq):
    q_blk0 = 3 * HGRN_HEADS // GROUP
    k_slab0 = 3 * HGRN_HEADS + ATTN_HEADS
    v_slab0 = k_slab0 + KV_HEADS
    bucket = jnp.asarray(_t5_bucket_table())
    smem = pl.BlockSpec(memory_space=pltpu.SMEM)
    assert w.shape[0] % (batch * KV_HEADS * 16) == 0
    w_slab = pl.BlockSpec((w.shape[0] // (batch * KV_HEADS), w.shape[1]), lambda b, x: (b * KV_HEADS + x, 0))
    return pl.pallas_call(
        _attn_kernel,
        grid=(batch, KV_HEADS),
        in_specs=[
            smem,
            smem,
            pl.BlockSpec((ATTN_BLOCK, KEY_SPAN), lambda b, x: (0, 0)),
            pl.BlockSpec((GROUP, seq, HEAD_DIM), lambda b, x: (q_blk0 + x, b, 0)),
            pl.BlockSpec((None, seq, HEAD_DIM), lambda b, x: (k_slab0 + x, b, 0)),
            pl.BlockSpec((None, seq, HEAD_DIM), lambda b, x: (v_slab0 + x, b, 0)),
            w_slab,
        ],
        out_specs=(pl.BlockSpec((GROUP, seq, HEAD_DIM), lambda b, x: (x, b, 0)), w_slab),
        out_shape=(jax.ShapeDtypeStruct((ATTN_HEADS, batch * seq, HEAD_DIM), BF16),
                   jax.ShapeDtypeStruct(w.shape, BF16)),
        scratch_shapes=[
            pltpu.VMEM((ATTN_HEADS * ATTN_BLOCK, KEY_SPAN), F32),
        ],
        compiler_params=pltpu.CompilerParams(
            dimension_semantics=("arbitrary", "arbitrary"), vmem_limit_bytes=VMEM_LIMIT_BYTES),
        name="attention",
    )(rel_table, sink, bucket, p16, p16, p16, w)


def _outproj_kernel(yh_ref, ya_ref, x_ref, wh_ref, wa_ref, g_ref, o_ref):
    y_h = jnp.concatenate([yh_ref[c] for c in range(yh_ref.shape[0])], axis=1)
    y_a = jnp.concatenate([ya_ref[c] for c in range(ya_ref.shape[0])], axis=1)
    mixed = jnp.dot(y_h, wh_ref[...], preferred_element_type=F32)
    mixed = mixed + jnp.dot(y_a, wa_ref[...], preferred_element_type=F32)
    o_ref[...] = x_ref[...] + _rms(mixed, g_ref[...])


def _outproj(y_h, y_a, x, w_out, gain, *, tm=512):
    m, d = x.shape
    return pl.pallas_call(
        _outproj_kernel,
        grid=(m // tm,),
        in_specs=[
            pl.BlockSpec((HGRN_HEADS, tm, HEAD_DIM), lambda i: (0, i, 0)),
            pl.BlockSpec((ATTN_HEADS, tm, HEAD_DIM), lambda i: (0, i, 0)),
            pl.BlockSpec((tm, d), lambda i: (i, 0)),
            pl.BlockSpec((HGRN_WIDTH, d), lambda i: (0, 0)),
            pl.BlockSpec((ATTN_WIDTH, d), lambda i: (1, 0)),
            pl.BlockSpec((1, d), lambda i: (0, 0)),
        ],
        out_specs=pl.BlockSpec((tm, d), lambda i: (i, 0)),
        out_shape=jax.ShapeDtypeStruct((m, d), F32),
        compiler_params=pltpu.CompilerParams(
            dimension_semantics=("parallel",), vmem_limit_bytes=VMEM_LIMIT_BYTES),
        name="outproj",
    )(y_h, y_a, x, w_out, w_out, gain)


def kernel(x, pre_norm_ffn1, post_norm_ffn1, w_ffn1_gate_up, w_ffn1_down, pre_norm_mix, post_norm_mix,
           w_mix_in, hgrn_lower_bounds_fwd, hgrn_lower_bounds_bwd, hgrn_out_norm, attn_sink, w_mix_out,
           pre_norm_ffn2, post_norm_ffn2, w_ffn2_gate_up, w_ffn2_down, rel_bias_table):
    batch, seq, d = x.shape
    depth = pre_norm_ffn1.shape[0]
    assert depth == 1 and d == D_MODEL
    xf = x.reshape(batch * seq, d)
    layer = 0
    xf, xn = _ffn_f32_weights(xf, pre_norm_ffn1[layer:layer + 1], post_norm_ffn1[layer:layer + 1],
                              pre_norm_mix[layer:layer + 1], w_ffn1_gate_up[layer], w_ffn1_down[layer])
    w_in = w_mix_in[layer]
    p16 = _proj(xn, w_in, [0, 1, 2, 3, 8, 9, 10, 11, 12])
    y_h, w2_gate_up, w2_down = _hgrn(p16, xn, w_in, hgrn_lower_bounds_fwd, hgrn_lower_bounds_bwd,
                                     hgrn_out_norm[layer:layer + 1], w_ffn2_gate_up[layer], w_ffn2_down[layer],
                                     batch, seq)
    y_a, w_out = _attention(p16, attn_sink[layer:layer + 1], rel_bias_table, w_mix_out[layer], batch, seq)
    xf = _outproj(y_h, y_a, xf, w_out, post_norm_mix[layer:layer + 1])
    xf = _ffn(xf, pre_norm_ffn2[layer:layer + 1], post_norm_ffn2[layer:layer + 1], w2_gate_up, w2_down)
    return xf.reshape(batch, seq, d)
```

```python
import functools
import math

import jax
import jax.numpy as jnp
import numpy as np
from jax import lax
from jax.experimental import pallas as pl
from jax.experimental.pallas import tpu as pltpu

F32 = jnp.float32
BF16 = jnp.bfloat16

D_MODEL = 2048
HGRN_WIDTH = 1024
HEAD_DIM = 128
HGRN_HEADS = HGRN_WIDTH // HEAD_DIM
CHUNK = 64
ATTN_WIDTH = 1024
ATTN_HEADS = ATTN_WIDTH // HEAD_DIM
KV_HEADS = 2
GROUP = ATTN_HEADS // KV_HEADS
KV_WIDTH = KV_HEADS * HEAD_DIM
WINDOW = 128
ATTN_BLOCK = 128
KEY_SPAN = ATTN_BLOCK + 2 * WINDOW
REL_BUCKETS = 32
REL_MAX_DIST = 128
D_FF = 5632
EPS = 1e-6
NEG_INF = -1e30
LOG2_E = 1.0 / math.log(2.0)

MIB = 1024 * 1024
VMEM_LIMIT_BYTES = 60 * MIB
ROW_BLOCK = 32
EDGE_CHUNK = 256
STAGE_LAG = 10
LOGIT_SLAB = 256
ATTN_BLOCKS_PER_GROUP = 2


def _rms(x, gain):
    return x * lax.rsqrt(jnp.mean(x * x, axis=-1, keepdims=True) + EPS) * gain


def _bf16_weight(w_ref, axis):
    if len(w_ref.shape) == 3:
        return jnp.concatenate([w_ref[c] for c in range(w_ref.shape[0])], axis=axis)
    return w_ref[...].astype(BF16)


def _ffn_kernel(*refs, emit_next_norm, emit_weights, n_passthrough):
    x_ref, gpre_ref, gpost_ref = refs[:3]
    k = 3
    gnext_ref = refs[k] if emit_next_norm else None
    k += int(emit_next_norm)
    wg_ref, wu_ref, wd_ref = refs[k:k + 3]
    k += 3 + n_passthrough
    o_ref, h_ref = refs[k:k + 2]
    j = pl.program_id(1)
    tm = x_ref.shape[0]

    def pre_norm_rows(r0, r1):
        for r in range(r0, r1, ROW_BLOCK):
            rows = pl.ds(r, ROW_BLOCK)
            h_ref[rows, :] = _rms(x_ref[rows, :], gpre_ref[...]).astype(BF16)

    def finish_rows(r0, r1):
        for r in range(r0, r1, ROW_BLOCK):
            rows = pl.ds(r, ROW_BLOCK)
            new_x = x_ref[rows, :] + 0.5 * _rms(o_ref[rows, :], gpost_ref[...])
            o_ref[rows, :] = new_x
            if emit_next_norm:
                h_ref[rows, :] = _rms(new_x, gnext_ref[...]).astype(BF16)

    def swiglu(h, w_gate, w_up):
        gate = jnp.dot(h, w_gate, preferred_element_type=F32)
        up = jnp.dot(h, w_up, preferred_element_type=F32)
        return (0.5 * gate * (1.0 + jnp.tanh(0.5 * gate)) * up).astype(BF16)

    def step(position):
        w_gate, w_up, w_down = _bf16_weight(wg_ref, 1), _bf16_weight(wu_ref, 1), _bf16_weight(wd_ref, 0)
        if emit_weights:
            for w16_ref, w in zip(refs[k + 2:k + 5], (w_gate, w_up, w_down)):
                w16_ref[0] = w
        if position == "first":
            acts = []
            for c0 in range(0, tm, EDGE_CHUNK):
                pre_norm_rows(c0, c0 + EDGE_CHUNK)
                acts.append(swiglu(h_ref[pl.ds(c0, EDGE_CHUNK), :], w_gate, w_up))
            o_ref[...] = jnp.dot(jnp.concatenate(acts, axis=0), w_down, preferred_element_type=F32)
            return
        act = swiglu(h_ref[...], w_gate, w_up)
        if position == "middle":
            o_ref[...] += jnp.dot(act, w_down, preferred_element_type=F32)
            return
        for c0 in range(0, tm, EDGE_CHUNK):
            rows = pl.ds(c0, EDGE_CHUNK)
            o_ref[rows, :] += jnp.dot(act[c0:c0 + EDGE_CHUNK], w_down, preferred_element_type=F32)
            finish_rows(c0, c0 + EDGE_CHUNK)

    last = pl.num_programs(1) - 1
    pl.when(j == 0)(functools.partial(step, "first"))
    pl.when((j > 0) & (j < last))(functools.partial(step, "middle"))
    pl.when(j == last)(functools.partial(step, "last"))


def _ffn(x, gpre, gpost, w_gate_up, w_down, *, tm=1024, tf=512):
    m, d = x.shape
    nj = D_FF // tf
    row_tile = pl.BlockSpec((tm, d), lambda i, j: (i, 0))
    gain_spec = pl.BlockSpec((1, d), lambda i, j: (0, 0))
    return pl.pallas_call(
        functools.partial(_ffn_kernel, emit_next_norm=False, emit_weights=False, n_passthrough=0),
        grid=(m // tm, nj),
        in_specs=[row_tile, gain_spec, gain_spec,
                  pl.BlockSpec((d, tf), lambda i, j: (0, j)),
                  pl.BlockSpec((d, tf), lambda i, j: (0, j + nj)),
                  pl.BlockSpec((tf, d), lambda i, j: (j, 0))],
        out_specs=row_tile,
        out_shape=jax.ShapeDtypeStruct((m, d), F32),
        scratch_shapes=[pltpu.VMEM((tm, d), BF16)],
        compiler_params=pltpu.CompilerParams(
            dimension_semantics=("parallel", "arbitrary"), vmem_limit_bytes=VMEM_LIMIT_BYTES),
        name="ffn",
    )(x, gpre, gpost, w_gate_up, w_gate_up, w_down)


def _ffn_f32_weights(x, gpre, gpost, gnext, w_gate_up, w_down, *, tm=1024, tf=256, slabs_per_step=2):
    m, d = x.shape
    nj = D_FF // tf
    gains = (gpre, gpost, gnext)
    gain_spec = pl.BlockSpec((1, d), lambda i, j: (0, 0))
    params = pltpu.CompilerParams(dimension_semantics=("parallel", "arbitrary"), vmem_limit_bytes=VMEM_LIMIT_BYTES)
    results = (jax.ShapeDtypeStruct((m, d), F32), jax.ShapeDtypeStruct((m, d), BF16))

    first_tile = pl.BlockSpec((tm, d), lambda i, j: (0, 0))
    col_slab = pl.BlockSpec((1, d, tf), lambda i, j: (j, 0, 0))
    row_slab = pl.BlockSpec((1, tf, d), lambda i, j: (j, 0, 0))
    out, xn, wg16, wu16, wd16 = pl.pallas_call(
        functools.partial(_ffn_kernel, emit_next_norm=True, emit_weights=True, n_passthrough=0),
        grid=(1, nj),
        in_specs=[pl.BlockSpec((tm, d), lambda i, j: (0, 0), pipeline_mode=pl.Buffered(1))] + [gain_spec] * 3 + [
            pl.BlockSpec((d, tf), lambda i, j: (0, j)),
            pl.BlockSpec((d, tf), lambda i, j: (0, j + nj)),
            pl.BlockSpec((tf, d), lambda i, j: (j, 0)),
        ],
        out_specs=(first_tile, first_tile, col_slab, col_slab, row_slab),
        out_shape=results + (jax.ShapeDtypeStruct((nj, d, tf), BF16), jax.ShapeDtypeStruct((nj, d, tf), BF16),
                             jax.ShapeDtypeStruct((nj, tf, d), BF16)),
        compiler_params=params,
        name="ffn_first",
    )(x, *gains, w_gate_up, w_gate_up, w_down)

    per = slabs_per_step
    later_tile = pl.BlockSpec((tm, d), lambda i, j: (i + 1, 0))
    untouched = pl.BlockSpec(memory_space=pl.ANY)
    return pl.pallas_call(
        functools.partial(_ffn_kernel, emit_next_norm=True, emit_weights=False, n_passthrough=2),
        grid=(m // tm - 1, nj // per),
        in_specs=[later_tile] + [gain_spec] * 3 + [
            pl.BlockSpec((per, d, tf), lambda i, j: (j, 0, 0)),
            pl.BlockSpec((per, d, tf), lambda i, j: (j, 0, 0)),
            pl.BlockSpec((per, tf, d), lambda i, j: (j, 0, 0)),
            untouched, untouched,
        ],
        out_specs=(later_tile, later_tile),
        out_shape=results,
        input_output_aliases={7: 0, 8: 1},
        compiler_params=params,
        name="ffn_rest",
    )(x, *gains, wg16, wu16, wd16, out, xn)


def _proj_kernel(xn_ref, w_ref, o_ref):
    res = jnp.dot(xn_ref[...], w_ref[...].astype(BF16), preferred_element_type=F32).astype(BF16)
    for c in range(o_ref.shape[0]):
        o_ref[c] = res[:, c * HEAD_DIM:(c + 1) * HEAD_DIM]


def _proj(xn, w_in, col_blocks, *, tm=4096, tn=512):
    m, d = xn.shape
    per = tn // HEAD_DIM
    jumps = [(pos, col_blocks[pos] - col_blocks[pos - 1] - 1) for pos in range(1, len(col_blocks))
             if col_blocks[pos] != col_blocks[pos - 1] + 1]

    def w_block(i, j):
        blk = j + col_blocks[0]
        for pos, gap in jumps:
            blk = blk + jnp.where(j >= pos, gap, 0)
        return (0, blk)

    return pl.pallas_call(
        _proj_kernel,
        grid=(m // tm, len(col_blocks)),
        in_specs=[
            pl.BlockSpec((tm, d), lambda i, j: (i, 0)),
            pl.BlockSpec((d, tn), w_block),
        ],
        out_specs=pl.BlockSpec((per, tm, HEAD_DIM), lambda i, j: (j, i, 0)),
        out_shape=jax.ShapeDtypeStruct((len(col_blocks) * per, m, HEAD_DIM), BF16),
        compiler_params=pltpu.CompilerParams(
            dimension_semantics=("parallel", "arbitrary"), vmem_limit_bytes=VMEM_LIMIT_BYTES),
        name="inproj",
    )(xn, w_in)


def _chunk_cumsum(x, row_in_chunk, reverse):
    n_rows = x.shape[0]
    c = x
    s = 1
    while s < CHUNK:
        if reverse:
            shifted = pltpu.roll(c, n_rows - s, axis=0)
            keep = row_in_chunk < CHUNK - s
        else:
            shifted = pltpu.roll(c, s, axis=0)
            keep = row_in_chunk >= s
        c = c + jnp.where(keep, shifted, 0.0)
        s *= 2
    return c


def _hgrn_kernel(q_ref, v_ref, xn_ref, wff_ref, wfb_ref, g_ref, lbf_ref, lbb_ref, gain_ref, wa_ref, wb_ref,
                 o_ref, wa16_ref, wb16_ref, qd_ref, kv_ref, dec_ref, st_ref, oi_ref):
    wa16_ref[...] = wa_ref[...].astype(BF16)
    wb16_ref[...] = wb_ref[...].astype(BF16)

    seq = q_ref.shape[0]
    n_chunks = seq // CHUNK
    pair = 2 * CHUNK
    contract_last = (((1,), (1,)), ((), ()))

    def gate_consts(lb_param_ref):
        a = lb_param_ref[...]
        e = jnp.exp(a - jnp.max(a, axis=0, keepdims=True))
        lb = e[0:1] / jnp.sum(e, axis=0, keepdims=True)
        return 0.5 * (1.0 + lb), 0.5 * (1.0 - lb)

    consts = (gate_consts(lbf_ref), gate_consts(lbb_ref))
    w_logits = jnp.concatenate([wff_ref[...].astype(BF16), wfb_ref[...].astype(BF16)], axis=1)

    r = lax.broadcasted_iota(jnp.int32, (pair, pair), 0)
    s = lax.broadcasted_iota(jnp.int32, (pair, pair), 1)
    same_chunk = (r // CHUNK) == (s // CHUNK)
    tri = (same_chunk & (r >= s), same_chunk & (s >= r))
    row_in_chunk = lax.broadcasted_iota(jnp.int32, (CHUNK, HEAD_DIM), 0)
    zero_blk = jnp.zeros((CHUNK, HEAD_DIM), BF16)

    n_pairs = n_chunks // 2
    rows_of = [pl.ds(i * pair, pair) for i in range(n_pairs)]
    pairs_per_slab = LOGIT_SLAB // pair
    n_slabs = seq // LOGIT_SLAB

    def project_logits(slab):
        return jnp.dot(xn_ref[pl.ds(slab * LOGIT_SLAB, LOGIT_SLAB), :], w_logits, preferred_element_type=F32)

    def gate_stage(i):
        slab, offset = divmod(i, pairs_per_slab)
        per_dir = []
        for d in range(2):
            mid, half = consts[d]
            ks, cs = [], []
            for r0 in range(offset * pair, (offset + 1) * pair, CHUNK):
                z = logits[slab][r0:r0 + CHUNK, d * HEAD_DIM:(d + 1) * HEAD_DIM]
                t = jnp.tanh(0.5 * z)
                f = mid + half * t
                ks.append(half * (1.0 - t))
                cs.append(_chunk_cumsum(jnp.log(f) * LOG2_E, row_in_chunk, reverse=(d == 1)))
            per_dir.append((jnp.concatenate(ks, axis=0), jnp.concatenate(cs, axis=0)))
        if offset == pairs_per_slab - 1 and slab + 2 < n_slabs:
            logits.append(project_logits(slab + 2))
        return per_dir

    def mix_stage(i, per_dir):
        rows = rows_of[i]
        q = q_ref[rows, :].astype(F32)
        vb = v_ref[rows, :]
        v_t = vb.astype(F32).T.astype(BF16)
        probs = None
        for d, (k, c) in enumerate(per_dir):
            c3 = c.reshape(2, CHUNK, HEAD_DIM)
            tot = c3[:, 0:1, :] if d == 1 else c3[:, CHUNK - 1:CHUNK, :]
            k_tail = (k * jnp.exp2(tot - c3).reshape(pair, HEAD_DIM)).astype(BF16)
            q_dec = (q * jnp.exp2(c)).astype(BF16)
            k_dec = (k * jnp.exp2(-c)).astype(BF16)
            qd_ref[rows, d * HEAD_DIM:(d + 1) * HEAD_DIM] = q_dec
            dec_ref[d, pl.ds(i * 2, 2), :] = jnp.exp2(tot).reshape(2, HEAD_DIM)
            sc = lax.dot_general(q_dec, k_dec, contract_last, preferred_element_type=F32)
            sc = jnp.where(tri[d], sc, 0.0)
            probs = sc if probs is None else probs + sc
            rhs = jnp.concatenate(
                [jnp.concatenate([k_tail[:CHUNK], zero_blk], axis=1),
                 jnp.concatenate([zero_blk, k_tail[CHUNK:]], axis=1)], axis=0)
            kv = jnp.dot(v_t, rhs, preferred_element_type=F32)
            kv_ref[d, i * 2] = kv[:, :HEAD_DIM]
            kv_ref[d, i * 2 + 1] = kv[:, HEAD_DIM:]
        return probs.astype(BF16), vb

    logits = [project_logits(s) for s in range(min(2, n_slabs))]
    gates, mixes = {}, {}
    for t in range(n_pairs + 2 * STAGE_LAG):
        if t < n_pairs:
            gates[t] = gate_stage(t)
        i = t - STAGE_LAG
        if 0 <= i < n_pairs:
            mixes[i] = mix_stage(i, gates.pop(i))
        i = t - 2 * STAGE_LAG
        if 0 <= i < n_pairs:
            probs, vb = mixes.pop(i)
            oi_ref[rows_of[i], :] = jnp.dot(probs, vb, preferred_element_type=F32)

    sf = sb = jnp.zeros((HEAD_DIM, HEAD_DIM), F32)
    for nf in range(n_chunks):
        nb = n_chunks - 1 - nf
        st_ref[nf, :, :HEAD_DIM] = sf.astype(BF16)
        st_ref[nb, :, HEAD_DIM:] = sb.astype(BF16)
        sf = sf * dec_ref[0, pl.ds(nf, 1), :] + kv_ref[0, nf]
        sb = sb * dec_ref[1, pl.ds(nb, 1), :] + kv_ref[1, nb]

    chunk_rows = [pl.ds(n * CHUNK, CHUNK) for n in range(n_chunks)]
    inter = [lax.dot_general(qd_ref[rows, :], st_ref[n], contract_last, preferred_element_type=F32)
             for n, rows in enumerate(chunk_rows)]
    for rows, o_inter in zip(chunk_rows, inter):
        o = _rms(oi_ref[rows, :] + o_inter, gain_ref[...])
        g = g_ref[rows, :].astype(F32)
        o_ref[rows, :] = (o * (0.5 * g * (1.0 + jnp.tanh(0.5 * g)))).astype(o_ref.dtype)


def _hgrn(p16, xn, w_in, lbf, lbb, out_gain, w_a, w_b, batch, seq):
    hw = HGRN_HEADS
    steps = batch * hw
    d = xn.shape[1]

    def w_col(group):
        return lambda b, h: (0, group * hw + h)

    def slab(group):
        return lambda b, h: (group * hw + h, b, 0)

    def row_slab(w):
        assert w.shape[0] % (steps * 16) == 0
        return pl.BlockSpec((w.shape[0] // steps, w.shape[1]), lambda b, h: (b * hw + h, 0))

    blk = (None, seq, HEAD_DIM)
    return pl.pallas_call(
        _hgrn_kernel,
        grid=(batch, hw),
        in_specs=[
            pl.BlockSpec(blk, slab(0)),
            pl.BlockSpec(blk, slab(1)),
            pl.BlockSpec((seq, d), lambda b, h: (b, 0)),
            pl.BlockSpec((d, HEAD_DIM), w_col(2)),
            pl.BlockSpec((d, HEAD_DIM), w_col(3)),
            pl.BlockSpec(blk, slab(2)),
            pl.BlockSpec((lbf.shape[0], HEAD_DIM), lambda b, h: (0, h)),
            pl.BlockSpec((lbb.shape[0], HEAD_DIM), lambda b, h: (0, h)),
            pl.BlockSpec((1, HEAD_DIM), lambda b, h: (0, h)),
            row_slab(w_a),
            row_slab(w_b),
        ],
        out_specs=(pl.BlockSpec(blk, lambda b, h: (h, b, 0)), row_slab(w_a), row_slab(w_b)),
        out_shape=(jax.ShapeDtypeStruct((hw, batch * seq, HEAD_DIM), BF16),
                   jax.ShapeDtypeStruct(w_a.shape, BF16), jax.ShapeDtypeStruct(w_b.shape, BF16)),
        scratch_shapes=[
            pltpu.VMEM((seq, 2 * HEAD_DIM), BF16),
            pltpu.VMEM((2, seq // CHUNK, HEAD_DIM, HEAD_DIM), F32),
            pltpu.VMEM((2, seq // CHUNK, HEAD_DIM), F32),
            pltpu.VMEM((seq // CHUNK, HEAD_DIM, 2 * HEAD_DIM), BF16),
            pltpu.VMEM((seq, HEAD_DIM), F32),
        ],
        compiler_params=pltpu.CompilerParams(
            dimension_semantics=("parallel", "parallel"), vmem_limit_bytes=VMEM_LIMIT_BYTES),
        name="hgrn",
    )(p16, p16, xn, w_in, w_in, p16, lbf, lbb, out_gain, w_a, w_b)


def _t5_bucket_table():
    nb = REL_BUCKETS // 2
    max_exact = nb // 2
    c = np.arange(ATTN_BLOCK)[:, None]
    s = np.arange(KEY_SPAN)[None, :]
    rel = s - WINDOW - c
    bucket = (rel > 0).astype(np.int32) * nb
    n = np.abs(rel)
    large = max_exact + (np.log(np.maximum(n, 1) / max_exact) / np.log(REL_MAX_DIST / max_exact)
                         * (nb - max_exact)).astype(np.int32)
    large = np.minimum(large, nb - 1)
    bucket = bucket + np.where(n < max_exact, n, large).astype(np.int32)
    return np.where(np.abs(rel) <= WINDOW, bucket, -1).astype(np.int32)


def _attn_kernel(table_ref, sink_ref, bucket_ref, q_ref, k_ref, v_ref, w_ref, o_ref, w16_ref, bias_ref):
    w16_ref[...] = w_ref[...].astype(BF16)

    x = pl.program_id(1)
    seq = k_ref.shape[0]
    n_blocks = seq // ATTN_BLOCK
    scale = LOG2_E / math.sqrt(HEAD_DIM)
    rows = GROUP * ATTN_BLOCK

    @pl.when((pl.program_id(0) == 0) & (x == 0))
    def _():
        bucket = bucket_ref[...]
        for head in range(ATTN_HEADS):
            bias = jnp.full((ATTN_BLOCK, KEY_SPAN), NEG_INF, F32)
            for b in range(REL_BUCKETS):
                bias = jnp.where(bucket == b, table_ref[b, head] * LOG2_E, bias)
            bias_ref[pl.ds(head * ATTN_BLOCK, ATTN_BLOCK), :] = bias

    row_head = lax.broadcasted_iota(jnp.int32, (rows, 1), 0) // ATTN_BLOCK
    sink = jnp.zeros((rows, 1), F32)
    for g in range(GROUP):
        sink = jnp.where(row_head == g, sink_ref[0, x * GROUP + g] * LOG2_E, sink)
    bias_rows = pl.ds(pl.multiple_of(x * rows, rows), rows)

    def block_rows(i):
        return pl.ds(i * ATTN_BLOCK, ATTN_BLOCK)

    def band(ref, i):
        return jnp.concatenate([ref[block_rows(max(i - 1, 0)), :], ref[block_rows(i), :],
                                ref[block_rows(min(i + 1, n_blocks - 1)), :]], axis=0)

    def scores(i):
        qs = jnp.concatenate([q_ref[g, block_rows(i), :] for g in range(GROUP)], axis=0)
        sc = lax.dot_general(qs, band(k_ref, i), (((1,), (1,)), ((), ())), preferred_element_type=F32)
        sc = sc * scale + bias_ref[bias_rows, :]
        if i in (0, n_blocks - 1):
            key_pos = (i - 1) * ATTN_BLOCK + lax.broadcasted_iota(jnp.int32, (rows, KEY_SPAN), 1)
            sc = jnp.where((key_pos >= 0) & (key_pos < seq), sc, NEG_INF)
        return sc

    def finish(i, sc):
        m = jnp.maximum(jnp.max(sc, axis=-1, keepdims=True), sink)
        pr = jnp.exp2(sc - m)
        den = jnp.sum(pr, axis=-1, keepdims=True) + jnp.exp2(sink - m)
        o = jnp.dot(pr.astype(BF16), band(v_ref, i), preferred_element_type=F32) / den
        for g in range(GROUP):
            o_ref[g, block_rows(i), :] = o[g * ATTN_BLOCK:(g + 1) * ATTN_BLOCK].astype(o_ref.dtype)

    for first in range(0, n_blocks, ATTN_BLOCKS_PER_GROUP):
        group = range(first, first + ATTN_BLOCKS_PER_GROUP)
        logits = [scores(i) for i in group]
        for i, sc in zip(group, logits):
            finish(i, sc)


def _attention(p16, sink, rel_table, w, batch, seq):
    q_blk0 = 3 * HGRN_HEADS // GROUP
    k_slab0 = 3 * HGRN_HEADS + ATTN_HEADS
    v_slab0 = k_slab0 + KV_HEADS
    bucket = jnp.asarray(_t5_bucket_table())
    smem = pl.BlockSpec(memory_space=pltpu.SMEM)
    assert w.shape[0] % (batch * KV_HEADS * 16) == 0
    w_slab = pl.BlockSpec((w.shape[0] // (batch * KV_HEADS), w.shape[1]), lambda b, x: (b * KV_HEADS + x, 0))
    return pl.pallas_call(
        _attn_kernel,
        grid=(batch, KV_HEADS),
        in_specs=[
            smem,
            smem,
            pl.BlockSpec((ATTN_BLOCK, KEY_SPAN), lambda b, x: (0, 0)),
            pl.BlockSpec((GROUP, seq, HEAD_DIM), lambda b, x: (q_blk0 + x, b, 0)),
            pl.BlockSpec((None, seq, HEAD_DIM), lambda b, x: (k_slab0 + x, b, 0)),
            pl.BlockSpec((None, seq, HEAD_DIM), lambda b, x: (v_slab0 + x, b, 0)),
            w_slab,
        ],
        out_specs=(pl.BlockSpec((GROUP, seq, HEAD_DIM), lambda b, x: (x, b, 0)), w_slab),
        out_shape=(jax.ShapeDtypeStruct((ATTN_HEADS, batch * seq, HEAD_DIM), BF16),
                   jax.ShapeDtypeStruct(w.shape, BF16)),
        scratch_shapes=[
            pltpu.VMEM((ATTN_HEADS * ATTN_BLOCK, KEY_SPAN), F32),
        ],
        compiler_params=pltpu.CompilerParams(
            dimension_semantics=("arbitrary", "arbitrary"), vmem_limit_bytes=VMEM_LIMIT_BYTES),
        name="attention",
    )(rel_table, sink, bucket, p16, p16, p16, w)


def _outproj_kernel(yh_ref, ya_ref, x_ref, wh_ref, wa_ref, g_ref, o_ref):
    y_h = jnp.concatenate([yh_ref[c] for c in range(yh_ref.shape[0])], axis=1)
    y_a = jnp.concatenate([ya_ref[c] for c in range(ya_ref.shape[0])], axis=1)
    mixed = jnp.dot(y_h, wh_ref[...], preferred_element_type=F32)
    mixed = mixed + jnp.dot(y_a, wa_ref[...], preferred_element_type=F32)
    o_ref[...] = x_ref[...] + _rms(mixed, g_ref[...])


def _outproj(y_h, y_a, x, w_out, gain, *, tm=512):
    m, d = x.shape
    return pl.pallas_call(
        _outproj_kernel,
        grid=(m // tm,),
        in_specs=[
            pl.BlockSpec((HGRN_HEADS, tm, HEAD_DIM), lambda i: (0, i, 0)),
            pl.BlockSpec((ATTN_HEADS, tm, HEAD_DIM), lambda i: (0, i, 0)),
            pl.BlockSpec((tm, d), lambda i: (i, 0)),
            pl.BlockSpec((HGRN_WIDTH, d), lambda i: (0, 0)),
            pl.BlockSpec((ATTN_WIDTH, d), lambda i: (1, 0)),
            pl.BlockSpec((1, d), lambda i: (0, 0)),
        ],
        out_specs=pl.BlockSpec((tm, d), lambda i: (i, 0)),
        out_shape=jax.ShapeDtypeStruct((m, d), F32),
        compiler_params=pltpu.CompilerParams(
            dimension_semantics=("parallel",), vmem_limit_bytes=VMEM_LIMIT_BYTES),
        name="outproj",
    )(y_h, y_a, x, w_out, w_out, gain)


def kernel(x, pre_norm_ffn1, post_norm_ffn1, w_ffn1_gate_up, w_ffn1_down, pre_norm_mix, post_norm_mix,
           w_mix_in, hgrn_lower_bounds_fwd, hgrn_lower_bounds_bwd, hgrn_out_norm, attn_sink, w_mix_out,
           pre_norm_ffn2, post_norm_ffn2, w_ffn2_gate_up, w_ffn2_down, rel_bias_table):
    batch, seq, d = x.shape
    depth = pre_norm_ffn1.shape[0]
    assert depth == 1 and d == D_MODEL
    xf = x.reshape(batch * seq, d)
    layer = 0
    xf, xn = _ffn_f32_weights(xf, pre_norm_ffn1[layer:layer + 1], post_norm_ffn1[layer:layer + 1],
                              pre_norm_mix[layer:layer + 1], w_ffn1_gate_up[layer], w_ffn1_down[layer])
    w_in = w_mix_in[layer]
    p16 = _proj(xn, w_in, [0, 1, 2, 3, 8, 9, 10, 11, 12])
    y_h, w2_gate_up, w2_down = _hgrn(p16, xn, w_in, hgrn_lower_bounds_fwd, hgrn_lower_bounds_bwd,
                                     hgrn_out_norm[layer:layer + 1], w_ffn2_gate_up[layer], w_ffn2_down[layer],
                                     batch, seq)
    y_a, w_out = _attention(p16, attn_sink[layer:layer + 1], rel_bias_table, w_mix_out[layer], batch, seq)
    xf = _outproj(y_h, y_a, xf, w_out, post_norm_mix[layer:layer + 1])
    xf = _ffn(xf, pre_norm_ffn2[layer:layer + 1], post_norm_ffn2[layer:layer + 1], w2_gate_up, w2_down)
    return xf.reshape(batch, seq, d)
```

```python
import functools
import math

import jax
import jax.numpy as jnp
import numpy as np
from jax import lax
from jax.experimental import pallas as pl
from jax.experimental.pallas import tpu as pltpu

F32 = jnp.float32
BF16 = jnp.bfloat16

D_MODEL = 2048
HGRN_WIDTH = 1024
HEAD_DIM = 128
HGRN_HEADS = HGRN_WIDTH // HEAD_DIM
CHUNK = 64
ATTN_WIDTH = 1024
ATTN_HEADS = ATTN_WIDTH // HEAD_DIM
KV_HEADS = 2
GROUP = ATTN_HEADS // KV_HEADS
KV_WIDTH = KV_HEADS * HEAD_DIM
WINDOW = 128
ATTN_BLOCK = 128
KEY_SPAN = ATTN_BLOCK + 2 * WINDOW
REL_BUCKETS = 32
REL_MAX_DIST = 128
D_FF = 5632
EPS = 1e-6
NEG_INF = -1e30
LOG2_E = 1.0 / math.log(2.0)

MIB = 1024 * 1024
VMEM_LIMIT_BYTES = 60 * MIB
ROW_BLOCK = 32
EDGE_CHUNK = 256
STAGE_LAG = 8
LOGIT_SLAB = 256
ATTN_SCORE_LEAD = 2


def _rms(x, gain):
    return x * lax.rsqrt(jnp.mean(x * x, axis=-1, keepdims=True) + EPS) * gain


def _bf16_weight(w_ref, axis):
    if len(w_ref.shape) == 3:
        return jnp.concatenate([w_ref[c] for c in range(w_ref.shape[0])], axis=axis)
    return w_ref[...].astype(BF16)


def _ffn_kernel(*refs, emit_next_norm, emit_weights, n_passthrough):
    x_ref, gpre_ref, gpost_ref = refs[:3]
    k = 3
    gnext_ref = refs[k] if emit_next_norm else None
    k += int(emit_next_norm)
    wg_ref, wu_ref, wd_ref = refs[k:k + 3]
    k += 3 + n_passthrough
    o_ref, h_ref = refs[k:k + 2]
    j = pl.program_id(1)
    tm = x_ref.shape[0]

    def pre_norm_rows(r0, r1):
        for r in range(r0, r1, ROW_BLOCK):
            rows = pl.ds(r, ROW_BLOCK)
            h_ref[rows, :] = _rms(x_ref[rows, :], gpre_ref[...]).astype(BF16)

    def finish_rows(r0, r1):
        for r in range(r0, r1, ROW_BLOCK):
            rows = pl.ds(r, ROW_BLOCK)
            new_x = x_ref[rows, :] + 0.5 * _rms(o_ref[rows, :], gpost_ref[...])
            o_ref[rows, :] = new_x
            if emit_next_norm:
                h_ref[rows, :] = _rms(new_x, gnext_ref[...]).astype(BF16)

    def swiglu(h, w_gate, w_up):
        gate = jnp.dot(h, w_gate, preferred_element_type=F32)
        up = jnp.dot(h, w_up, preferred_element_type=F32)
        return (0.5 * gate * (1.0 + jnp.tanh(0.5 * gate)) * up).astype(BF16)

    def step(position):
        w_gate, w_up, w_down = _bf16_weight(wg_ref, 1), _bf16_weight(wu_ref, 1), _bf16_weight(wd_ref, 0)
        if emit_weights:
            for w16_ref, w in zip(refs[k + 2:k + 5], (w_gate, w_up, w_down)):
                w16_ref[0] = w
        if position == "first":
            acts = []
            for c0 in range(0, tm, EDGE_CHUNK):
                pre_norm_rows(c0, c0 + EDGE_CHUNK)
                acts.append(swiglu(h_ref[pl.ds(c0, EDGE_CHUNK), :], w_gate, w_up))
            o_ref[...] = jnp.dot(jnp.concatenate(acts, axis=0), w_down, preferred_element_type=F32)
            return
        act = swiglu(h_ref[...], w_gate, w_up)
        if position == "middle":
            o_ref[...] += jnp.dot(act, w_down, preferred_element_type=F32)
            return
        for c0 in range(0, tm, EDGE_CHUNK):
            rows = pl.ds(c0, EDGE_CHUNK)
            o_ref[rows, :] += jnp.dot(act[c0:c0 + EDGE_CHUNK], w_down, preferred_element_type=F32)
            finish_rows(c0, c0 + EDGE_CHUNK)

    last = pl.num_programs(1) - 1
    pl.when(j == 0)(functools.partial(step, "first"))
    pl.when((j > 0) & (j < last))(functools.partial(step, "middle"))
    pl.when(j == last)(functools.partial(step, "last"))


def _ffn(x, gpre, gpost, w_gate_up, w_down, *, tm=1024, tf=512):
    m, d = x.shape
    nj = D_FF // tf
    row_tile = pl.BlockSpec((tm, d), lambda i, j: (i, 0))
    gain_spec = pl.BlockSpec((1, d), lambda i, j: (0, 0))
    return pl.pallas_call(
        functools.partial(_ffn_kernel, emit_next_norm=False, emit_weights=False, n_passthrough=0),
        grid=(m // tm, nj),
        in_specs=[row_tile, gain_spec, gain_spec,
                  pl.BlockSpec((d, tf), lambda i, j: (0, j)),
                  pl.BlockSpec((d, tf), lambda i, j: (0, j + nj)),
                  pl.BlockSpec((tf, d), lambda i, j: (j, 0))],
        out_specs=row_tile,
        out_shape=jax.ShapeDtypeStruct((m, d), F32),
        scratch_shapes=[pltpu.VMEM((tm, d), BF16)],
        compiler_params=pltpu.CompilerParams(
            dimension_semantics=("parallel", "arbitrary"), vmem_limit_bytes=VMEM_LIMIT_BYTES),
        name="ffn",
    )(x, gpre, gpost, w_gate_up, w_gate_up, w_down)


def _ffn_f32_weights(x, gpre, gpost, gnext, w_gate_up, w_down, *, tm=1024, tf=256, slabs_per_step=2):
    m, d = x.shape
    nj = D_FF // tf
    gains = (gpre, gpost, gnext)
    gain_spec = pl.BlockSpec((1, d), lambda i, j: (0, 0))
    params = pltpu.CompilerParams(dimension_semantics=("parallel", "arbitrary"), vmem_limit_bytes=VMEM_LIMIT_BYTES)
    results = (jax.ShapeDtypeStruct((m, d), F32), jax.ShapeDtypeStruct((m, d), BF16))

    first_tile = pl.BlockSpec((tm, d), lambda i, j: (0, 0))
    col_slab = pl.BlockSpec((1, d, tf), lambda i, j: (j, 0, 0))
    row_slab = pl.BlockSpec((1, tf, d), lambda i, j: (j, 0, 0))
    out, xn, wg16, wu16, wd16 = pl.pallas_call(
        functools.partial(_ffn_kernel, emit_next_norm=True, emit_weights=True, n_passthrough=0),
        grid=(1, nj),
        in_specs=[pl.BlockSpec((tm, d), lambda i, j: (0, 0), pipeline_mode=pl.Buffered(1))] + [gain_spec] * 3 + [
            pl.BlockSpec((d, tf), lambda i, j: (0, j)),
            pl.BlockSpec((d, tf), lambda i, j: (0, j + nj)),
            pl.BlockSpec((tf, d), lambda i, j: (j, 0)),
        ],
        out_specs=(first_tile, first_tile, col_slab, col_slab, row_slab),
        out_shape=results + (jax.ShapeDtypeStruct((nj, d, tf), BF16), jax.ShapeDtypeStruct((nj, d, tf), BF16),
                             jax.ShapeDtypeStruct((nj, tf, d), BF16)),
        compiler_params=params,
        name="ffn_first",
    )(x, *gains, w_gate_up, w_gate_up, w_down)

    per = slabs_per_step
    later_tile = pl.BlockSpec((tm, d), lambda i, j: (i + 1, 0))
    untouched = pl.BlockSpec(memory_space=pl.ANY)
    return pl.pallas_call(
        functools.partial(_ffn_kernel, emit_next_norm=True, emit_weights=False, n_passthrough=2),
        grid=(m // tm - 1, nj // per),
        in_specs=[later_tile] + [gain_spec] * 3 + [
            pl.BlockSpec((per, d, tf), lambda i, j: (j, 0, 0)),
            pl.BlockSpec((per, d, tf), lambda i, j: (j, 0, 0)),
            pl.BlockSpec((per, tf, d), lambda i, j: (j, 0, 0)),
            untouched, untouched,
        ],
        out_specs=(later_tile, later_tile),
        out_shape=results,
        input_output_aliases={7: 0, 8: 1},
        compiler_params=params,
        name="ffn_rest",
    )(x, *gains, wg16, wu16, wd16, out, xn)


def _proj_kernel(xn_ref, w_ref, o_ref):
    res = jnp.dot(xn_ref[...], w_ref[...].astype(BF16), preferred_element_type=F32).astype(BF16)
    for c in range(o_ref.shape[0]):
        o_ref[c] = res[:, c * HEAD_DIM:(c + 1) * HEAD_DIM]


def _proj(xn, w_in, col_blocks, *, tm=4096, tn=512):
    m, d = xn.shape
    per = tn // HEAD_DIM
    jumps = [(pos, col_blocks[pos] - col_blocks[pos - 1] - 1) for pos in range(1, len(col_blocks))
             if col_blocks[pos] != col_blocks[pos - 1] + 1]

    def w_block(i, j):
        blk = j + col_blocks[0]
        for pos, gap in jumps:
            blk = blk + jnp.where(j >= pos, gap, 0)
        return (0, blk)

    return pl.pallas_call(
        _proj_kernel,
        grid=(m // tm, len(col_blocks)),
        in_specs=[
            pl.BlockSpec((tm, d), lambda i, j: (i, 0)),
            pl.BlockSpec((d, tn), w_block),
        ],
        out_specs=pl.BlockSpec((per, tm, HEAD_DIM), lambda i, j: (j, i, 0)),
        out_shape=jax.ShapeDtypeStruct((len(col_blocks) * per, m, HEAD_DIM), BF16),
        compiler_params=pltpu.CompilerParams(
            dimension_semantics=("parallel", "arbitrary"), vmem_limit_bytes=VMEM_LIMIT_BYTES),
        name="inproj",
    )(xn, w_in)


def _chunk_cumsum(x, row_in_chunk, reverse):
    n_rows = x.shape[0]
    c = x
    s = 1
    while s < CHUNK:
        if reverse:
            shifted = pltpu.roll(c, n_rows - s, axis=0)
            keep = row_in_chunk < CHUNK - s
        else:
            shifted = pltpu.roll(c, s, axis=0)
            keep = row_in_chunk >= s
        c = c + jnp.where(keep, shifted, 0.0)
        s *= 2
    return c


def _hgrn_kernel(q_ref, v_ref, xn_ref, wff_ref, wfb_ref, g_ref, lbf_ref, lbb_ref, gain_ref, wa_ref, wb_ref,
                 o_ref, wa16_ref, wb16_ref, qd_ref, kv_ref, dec_ref, st_ref, oi_ref):
    wa16_ref[...] = wa_ref[...].astype(BF16)
    wb16_ref[...] = wb_ref[...].astype(BF16)

    seq = q_ref.shape[0]
    n_chunks = seq // CHUNK
    pair = 2 * CHUNK
    contract_last = (((1,), (1,)), ((), ()))

    def gate_consts(lb_param_ref):
        a = lb_param_ref[...]
        e = jnp.exp(a - jnp.max(a, axis=0, keepdims=True))
        lb = e[0:1] / jnp.sum(e, axis=0, keepdims=True)
        return 0.5 * (1.0 + lb), 0.5 * (1.0 - lb)

    consts = (gate_consts(lbf_ref), gate_consts(lbb_ref))
    w_logits = jnp.concatenate([wff_ref[...].astype(BF16), wfb_ref[...].astype(BF16)], axis=1)

    r = lax.broadcasted_iota(jnp.int32, (pair, pair), 0)
    s = lax.broadcasted_iota(jnp.int32, (pair, pair), 1)
    same_chunk = (r // CHUNK) == (s // CHUNK)
    tri = (same_chunk & (r >= s), same_chunk & (s >= r))
    row_in_chunk = lax.broadcasted_iota(jnp.int32, (CHUNK, HEAD_DIM), 0)
    zero_blk = jnp.zeros((CHUNK, HEAD_DIM), BF16)

    n_pairs = n_chunks // 2
    rows_of = [pl.ds(i * pair, pair) for i in range(n_pairs)]
    pairs_per_slab = LOGIT_SLAB // pair
    n_slabs = seq // LOGIT_SLAB

    def project_logits(slab):
        return jnp.dot(xn_ref[pl.ds(slab * LOGIT_SLAB, LOGIT_SLAB), :], w_logits, preferred_element_type=F32)

    def gate_stage(i):
        slab, offset = divmod(i, pairs_per_slab)
        per_dir = []
        for d in range(2):
            mid, half = consts[d]
            ks, cs = [], []
            for r0 in range(offset * pair, (offset + 1) * pair, CHUNK):
                z = logits[slab][r0:r0 + CHUNK, d * HEAD_DIM:(d + 1) * HEAD_DIM]
                t = jnp.tanh(0.5 * z)
                f = mid + half * t
                ks.append(half * (1.0 - t))
                cs.append(_chunk_cumsum(jnp.log(f) * LOG2_E, row_in_chunk, reverse=(d == 1)))
            per_dir.append((jnp.concatenate(ks, axis=0), jnp.concatenate(cs, axis=0)))
        if offset == pairs_per_slab - 1 and slab + 2 < n_slabs:
            logits.append(project_logits(slab + 2))
        return per_dir

    def mix_stage(i, per_dir):
        rows = rows_of[i]
        q = q_ref[rows, :].astype(F32)
        vb = v_ref[rows, :]
        v_t = vb.astype(F32).T.astype(BF16)
        probs = None
        for d, (k, c) in enumerate(per_dir):
            c3 = c.reshape(2, CHUNK, HEAD_DIM)
            tot = c3[:, 0:1, :] if d == 1 else c3[:, CHUNK - 1:CHUNK, :]
            k_tail = (k * jnp.exp2(tot - c3).reshape(pair, HEAD_DIM)).astype(BF16)
            q_dec = (q * jnp.exp2(c)).astype(BF16)
            k_dec = (k * jnp.exp2(-c)).astype(BF16)
            qd_ref[rows, d * HEAD_DIM:(d + 1) * HEAD_DIM] = q_dec
            dec_ref[d, pl.ds(i * 2, 2), :] = jnp.exp2(tot).reshape(2, HEAD_DIM)
            sc = lax.dot_general(q_dec, k_dec, contract_last, preferred_element_type=F32)
            sc = jnp.where(tri[d], sc, 0.0)
            probs = sc if probs is None else probs + sc
            rhs = jnp.concatenate(
                [jnp.concatenate([k_tail[:CHUNK], zero_blk], axis=1),
                 jnp.concatenate([zero_blk, k_tail[CHUNK:]], axis=1)], axis=0)
            kv = jnp.dot(v_t, rhs, preferred_element_type=F32)
            kv_ref[d, i * 2] = kv[:, :HEAD_DIM]
            kv_ref[d, i * 2 + 1] = kv[:, HEAD_DIM:]
        return probs.astype(BF16), vb

    logits = [project_logits(s) for s in range(min(2, n_slabs))]
    gates, mixes = {}, {}
    for t in range(n_pairs + 2 * STAGE_LAG):
        if t < n_pairs:
            gates[t] = gate_stage(t)
        i = t - STAGE_LAG
        if 0 <= i < n_pairs:
            mixes[i] = mix_stage(i, gates.pop(i))
        i = t - 2 * STAGE_LAG
        if 0 <= i < n_pairs:
            probs, vb = mixes.pop(i)
            oi_ref[rows_of[i], :] = jnp.dot(probs, vb, preferred_element_type=F32)

    sf = sb = jnp.zeros((HEAD_DIM, HEAD_DIM), F32)
    for nf in range(n_chunks):
        nb = n_chunks - 1 - nf
        st_ref[nf, :, :HEAD_DIM] = sf.astype(BF16)
        st_ref[nb, :, HEAD_DIM:] = sb.astype(BF16)
        sf = sf * dec_ref[0, pl.ds(nf, 1), :] + kv_ref[0, nf]
        sb = sb * dec_ref[1, pl.ds(nb, 1), :] + kv_ref[1, nb]

    chunk_rows = [pl.ds(n * CHUNK, CHUNK) for n in range(n_chunks)]
    inter = [lax.dot_general(qd_ref[rows, :], st_ref[n], contract_last, preferred_element_type=F32)
             for n, rows in enumerate(chunk_rows)]
    for rows, o_inter in zip(chunk_rows, inter):
        o = _rms(oi_ref[rows, :] + o_inter, gain_ref[...])
        g = g_ref[rows, :].astype(F32)
        o_ref[rows, :] = (o * (0.5 * g * (1.0 + jnp.tanh(0.5 * g)))).astype(o_ref.dtype)


def _hgrn(p16, xn, w_in, lbf, lbb, out_gain, w_a, w_b, batch, seq):
    hw = HGRN_HEADS
    steps = batch * hw
    d = xn.shape[1]

    def w_col(group):
        return lambda b, h: (0, group * hw + h)

    def slab(group):
        return lambda b, h: (group * hw + h, b, 0)

    def row_slab(w):
        assert w.shape[0] % (steps * 16) == 0
        return pl.BlockSpec((w.shape[0] // steps, w.shape[1]), lambda b, h: (b * hw + h, 0))

    blk = (None, seq, HEAD_DIM)
    return pl.pallas_call(
        _hgrn_kernel,
        grid=(batch, hw),
        in_specs=[
            pl.BlockSpec(blk, slab(0)),
            pl.BlockSpec(blk, slab(1)),
            pl.BlockSpec((seq, d), lambda b, h: (b, 0)),
            pl.BlockSpec((d, HEAD_DIM), w_col(2)),
            pl.BlockSpec((d, HEAD_DIM), w_col(3)),
            pl.BlockSpec(blk, slab(2)),
            pl.BlockSpec((lbf.shape[0], HEAD_DIM), lambda b, h: (0, h)),
            pl.BlockSpec((lbb.shape[0], HEAD_DIM), lambda b, h: (0, h)),
            pl.BlockSpec((1, HEAD_DIM), lambda b, h: (0, h)),
            row_slab(w_a),
            row_slab(w_b),
        ],
        out_specs=(pl.BlockSpec(blk, lambda b, h: (h, b, 0)), row_slab(w_a), row_slab(w_b)),
        out_shape=(jax.ShapeDtypeStruct((hw, batch * seq, HEAD_DIM), BF16),
                   jax.ShapeDtypeStruct(w_a.shape, BF16), jax.ShapeDtypeStruct(w_b.shape, BF16)),
        scratch_shapes=[
            pltpu.VMEM((seq, 2 * HEAD_DIM), BF16),
            pltpu.VMEM((2, seq // CHUNK, HEAD_DIM, HEAD_DIM), F32),
            pltpu.VMEM((2, seq // CHUNK, HEAD_DIM), F32),
            pltpu.VMEM((seq // CHUNK, HEAD_DIM, 2 * HEAD_DIM), BF16),
            pltpu.VMEM((seq, HEAD_DIM), F32),
        ],
        compiler_params=pltpu.CompilerParams(
            dimension_semantics=("parallel", "parallel"), vmem_limit_bytes=VMEM_LIMIT_BYTES),
        name="hgrn",
    )(p16, p16, xn, w_in, w_in, p16, lbf, lbb, out_gain, w_a, w_b)


def _t5_bucket_table():
    nb = REL_BUCKETS // 2
    max_exact = nb // 2
    c = np.arange(ATTN_BLOCK)[:, None]
    s = np.arange(KEY_SPAN)[None, :]
    rel = s - WINDOW - c
    bucket = (rel > 0).astype(np.int32) * nb
    n = np.abs(rel)
    large = max_exact + (np.log(np.maximum(n, 1) / max_exact) / np.log(REL_MAX_DIST / max_exact)
                         * (nb - max_exact)).astype(np.int32)
    large = np.minimum(large, nb - 1)
    bucket = bucket + np.where(n < max_exact, n, large).astype(np.int32)
    return np.where(np.abs(rel) <= WINDOW, bucket, -1).astype(np.int32)


def _attn_kernel(table_ref, sink_ref, bucket_ref, q_ref, k_ref, v_ref, w_ref, o_ref, w16_ref, bias_ref):
    w16_ref[...] = w_ref[...].astype(BF16)

    x = pl.program_id(1)
    seq = k_ref.shape[0]
    n_blocks = seq // ATTN_BLOCK
    scale = LOG2_E / math.sqrt(HEAD_DIM)
    rows = GROUP * ATTN_BLOCK

    @pl.when((pl.program_id(0) == 0) & (x == 0))
    def _():
        bucket = bucket_ref[...]
        for head in range(ATTN_HEADS):
            bias = jnp.full((ATTN_BLOCK, KEY_SPAN), NEG_INF, F32)
            for b in range(REL_BUCKETS):
                bias = jnp.where(bucket == b, table_ref[b, head] * LOG2_E, bias)
            bias_ref[pl.ds(head * ATTN_BLOCK, ATTN_BLOCK), :] = bias

    row_head = lax.broadcasted_iota(jnp.int32, (rows, 1), 0) // ATTN_BLOCK
    sink = jnp.zeros((rows, 1), F32)
    for g in range(GROUP):
        sink = jnp.where(row_head == g, sink_ref[0, x * GROUP + g] * LOG2_E, sink)
    bias_rows = pl.ds(pl.multiple_of(x * rows, rows), rows)

    def block_rows(i):
        return pl.ds(i * ATTN_BLOCK, ATTN_BLOCK)

    def band(ref, i):
        return jnp.concatenate([ref[block_rows(max(i - 1, 0)), :], ref[block_rows(i), :],
                                ref[block_rows(min(i + 1, n_blocks - 1)), :]], axis=0)

    def scores(i):
        qs = jnp.concatenate([q_ref[g, block_rows(i), :] for g in range(GROUP)], axis=0)
        sc = lax.dot_general(qs, band(k_ref, i), (((1,), (1,)), ((), ())), preferred_element_type=F32)
        sc = sc * scale + bias_ref[bias_rows, :]
        if i in (0, n_blocks - 1):
            key_pos = (i - 1) * ATTN_BLOCK + lax.broadcasted_iota(jnp.int32, (rows, KEY_SPAN), 1)
            sc = jnp.where((key_pos >= 0) & (key_pos < seq), sc, NEG_INF)
        return sc

    def finish(i, sc):
        m = jnp.maximum(jnp.max(sc, axis=-1, keepdims=True), sink)
        pr = jnp.exp2(sc - m)
        den = jnp.sum(pr, axis=-1, keepdims=True) + jnp.exp2(sink - m)
        o = jnp.dot(pr.astype(BF16), band(v_ref, i), preferred_element_type=F32) / den
        for g in range(GROUP):
            o_ref[g, block_rows(i), :] = o[g * ATTN_BLOCK:(g + 1) * ATTN_BLOCK].astype(o_ref.dtype)

    logits = {i: scores(i) for i in range(ATTN_SCORE_LEAD)}
    for i in range(n_blocks):
        if i + ATTN_SCORE_LEAD < n_blocks:
            logits[i + ATTN_SCORE_LEAD] = scores(i + ATTN_SCORE_LEAD)
        finish(i, logits.pop(i))


def _attention(p16, sink, rel_table, w, batch, seq):
    q_blk0 = 3 * HGRN_HEADS // GROUP
    k_slab0 = 3 * HGRN_HEADS + ATTN_HEADS
    v_slab0 = k_slab0 + KV_HEADS
    bucket = jnp.asarray(_t5_bucket_table())
    smem = pl.BlockSpec(memory_space=pltpu.SMEM)
    assert w.shape[0] % (batch * KV_HEADS * 16) == 0
    w_slab = pl.BlockSpec((w.shape[0] // (batch * KV_HEADS), w.shape[1]), lambda b, x: (b * KV_HEADS + x, 0))
    return pl.pallas_call(
        _attn_kernel,
        grid=(batch, KV_HEADS),
        in_specs=[
            smem,
            smem,
            pl.BlockSpec((ATTN_BLOCK, KEY_SPAN), lambda b, x: (0, 0)),
            pl.BlockSpec((GROUP, seq, HEAD_DIM), lambda b, x: (q_blk0 + x, b, 0)),
            pl.BlockSpec((None, seq, HEAD_DIM), lambda b, x: (k_slab0 + x, b, 0)),
            pl.BlockSpec((None, seq, HEAD_DIM), lambda b, x: (v_slab0 + x, b, 0)),
            w_slab,
        ],
        out_specs=(pl.BlockSpec((GROUP, seq, HEAD_DIM), lambda b, x: (x, b, 0)), w_slab),
        out_shape=(jax.ShapeDtypeStruct((ATTN_HEADS, batch * seq, HEAD_DIM), BF16),
                   jax.ShapeDtypeStruct(w.shape, BF16)),
        scratch_shapes=[
            pltpu.VMEM((ATTN_HEADS * ATTN_BLOCK, KEY_SPAN), F32),
        ],
        compiler_params=pltpu.CompilerParams(
            dimension_semantics=("arbitrary", "arbitrary"), vmem_limit_bytes=VMEM_LIMIT_BYTES),
        name="attention",
    )(rel_table, sink, bucket, p16, p16, p16, w)


def _outproj_kernel(yh_ref, ya_ref, x_ref, wh_ref, wa_ref, g_ref, o_ref):
    y_h = jnp.concatenate([yh_ref[c] for c in range(yh_ref.shape[0])], axis=1)
    y_a = jnp.concatenate([ya_ref[c] for c in range(ya_ref.shape[0])], axis=1)
    mixed = jnp.dot(y_h, wh_ref[...], preferred_element_type=F32)
    mixed = mixed + jnp.dot(y_a, wa_ref[...], preferred_element_type=F32)
    o_ref[...] = x_ref[...] + _rms(mixed, g_ref[...])


def _outproj(y_h, y_a, x, w_out, gain, *, tm=512):
    m, d = x.shape
    return pl.pallas_call(
        _outproj_kernel,
        grid=(m // tm,),
        in_specs=[
            pl.BlockSpec((HGRN_HEADS, tm, HEAD_DIM), lambda i: (0, i, 0)),
            pl.BlockSpec((ATTN_HEADS, tm, HEAD_DIM), lambda i: (0, i, 0)),
            pl.BlockSpec((tm, d), lambda i: (i, 0)),
            pl.BlockSpec((HGRN_WIDTH, d), lambda i: (0, 0)),
            pl.BlockSpec((ATTN_WIDTH, d), lambda i: (1, 0)),
            pl.BlockSpec((1, d), lambda i: (0, 0)),
        ],
        out_specs=pl.BlockSpec((tm, d), lambda i: (i, 0)),
        out_shape=jax.ShapeDtypeStruct((m, d), F32),
        compiler_params=pltpu.CompilerParams(
            dimension_semantics=("parallel",), vmem_limit_bytes=VMEM_LIMIT_BYTES),
        name="outproj",
    )(y_h, y_a, x, w_out, w_out, gain)


def kernel(x, pre_norm_ffn1, post_norm_ffn1, w_ffn1_gate_up, w_ffn1_down, pre_norm_mix, post_norm_mix,
           w_mix_in, hgrn_lower_bounds_fwd, hgrn_lower_bounds_bwd, hgrn_out_norm, attn_sink, w_mix_out,
           pre_norm_ffn2, post_norm_ffn2, w_ffn2_gate_up, w_ffn2_down, rel_bias_table):
    batch, seq, d = x.shape
    depth = pre_norm_ffn1.shape[0]
    assert depth == 1 and d == D_MODEL
    xf = x.reshape(batch * seq, d)
    layer = 0
    xf, xn = _ffn_f32_weights(xf, pre_norm_ffn1[layer:layer + 1], post_norm_ffn1[layer:layer + 1],
                              pre_norm_mix[layer:layer + 1], w_ffn1_gate_up[layer], w_ffn1_down[layer])
    w_in = w_mix_in[layer]
    p16 = _proj(xn, w_in, [0, 1, 2, 3, 8, 9, 10, 11, 12])
    y_h, w2_gate_up, w2_down = _hgrn(p16, xn, w_in, hgrn_lower_bounds_fwd, hgrn_lower_bounds_bwd,
                                     hgrn_out_norm[layer:layer + 1], w_ffn2_gate_up[layer], w_ffn2_down[layer],
                                     batch, seq)
    y_a, w_out = _attention(p16, attn_sink[layer:layer + 1], rel_bias_table, w_mix_out[layer], batch, seq)
    xf = _outproj(y_h, y_a, xf, w_out, post_norm_mix[layer:layer + 1])
    xf = _ffn(xf, pre_norm_ffn2[layer:layer + 1], post_norm_ffn2[layer:layer + 1], w2_gate_up, w2_down)
    return xf.reshape(batch, seq, d)
```

```python
import functools
import math

import jax
import jax.numpy as jnp
import numpy as np
from jax import lax
from jax.experimental import pallas as pl
from jax.experimental.pallas import tpu as pltpu

F32 = jnp.float32
BF16 = jnp.bfloat16

D_MODEL = 2048
HGRN_WIDTH = 1024
HEAD_DIM = 128
HGRN_HEADS = HGRN_WIDTH // HEAD_DIM
CHUNK = 64
ATTN_WIDTH = 1024
ATTN_HEADS = ATTN_WIDTH // HEAD_DIM
KV_HEADS = 2
GROUP = ATTN_HEADS // KV_HEADS
KV_WIDTH = KV_HEADS * HEAD_DIM
WINDOW = 128
ATTN_BLOCK = 128
KEY_SPAN = ATTN_BLOCK + 2 * WINDOW
REL_BUCKETS = 32
REL_MAX_DIST = 128
D_FF = 5632
EPS = 1e-6
NEG_INF = -1e30
LOG2_E = 1.0 / math.log(2.0)

MIB = 1024 * 1024
VMEM_LIMIT_BYTES = 60 * MIB
ROW_BLOCK = 32
EDGE_CHUNK = 256
STAGE_LAG = 10
LOGIT_SLAB = 256
ATTN_SCORE_LEAD = 2


def _rms(x, gain):
    return x * lax.rsqrt(jnp.mean(x * x, axis=-1, keepdims=True) + EPS) * gain


def _bf16_weight(w_ref, axis):
    if len(w_ref.shape) == 3:
        return jnp.concatenate([w_ref[c] for c in range(w_ref.shape[0])], axis=axis)
    return w_ref[...].astype(BF16)


def _ffn_kernel(*refs, emit_next_norm, emit_weights, n_passthrough):
    x_ref, gpre_ref, gpost_ref = refs[:3]
    k = 3
    gnext_ref = refs[k] if emit_next_norm else None
    k += int(emit_next_norm)
    wg_ref, wu_ref, wd_ref = refs[k:k + 3]
    k += 3 + n_passthrough
    o_ref, h_ref = refs[k:k + 2]
    j = pl.program_id(1)
    tm = x_ref.shape[0]

    def pre_norm_rows(r0, r1):
        for r in range(r0, r1, ROW_BLOCK):
            rows = pl.ds(r, ROW_BLOCK)
            h_ref[rows, :] = _rms(x_ref[rows, :], gpre_ref[...]).astype(BF16)

    def finish_rows(r0, r1):
        for r in range(r0, r1, ROW_BLOCK):
            rows = pl.ds(r, ROW_BLOCK)
            new_x = x_ref[rows, :] + 0.5 * _rms(o_ref[rows, :], gpost_ref[...])
            o_ref[rows, :] = new_x
            if emit_next_norm:
                h_ref[rows, :] = _rms(new_x, gnext_ref[...]).astype(BF16)

    def swiglu(h, w_gate, w_up):
        gate = jnp.dot(h, w_gate, preferred_element_type=F32)
        up = jnp.dot(h, w_up, preferred_element_type=F32)
        return (0.5 * gate * (1.0 + jnp.tanh(0.5 * gate)) * up).astype(BF16)

    def step(position):
        w_gate, w_up, w_down = _bf16_weight(wg_ref, 1), _bf16_weight(wu_ref, 1), _bf16_weight(wd_ref, 0)
        if emit_weights:
            for w16_ref, w in zip(refs[k + 2:k + 5], (w_gate, w_up, w_down)):
                w16_ref[0] = w
        if position == "first":
            acts = []
            for c0 in range(0, tm, EDGE_CHUNK):
                pre_norm_rows(c0, c0 + EDGE_CHUNK)
                acts.append(swiglu(h_ref[pl.ds(c0, EDGE_CHUNK), :], w_gate, w_up))
            o_ref[...] = jnp.dot(jnp.concatenate(acts, axis=0), w_down, preferred_element_type=F32)
            return
        act = swiglu(h_ref[...], w_gate, w_up)
        if position == "middle":
            o_ref[...] += jnp.dot(act, w_down, preferred_element_type=F32)
            return
        for c0 in range(0, tm, EDGE_CHUNK):
            rows = pl.ds(c0, EDGE_CHUNK)
            o_ref[rows, :] += jnp.dot(act[c0:c0 + EDGE_CHUNK], w_down, preferred_element_type=F32)
            finish_rows(c0, c0 + EDGE_CHUNK)

    last = pl.num_programs(1) - 1
    pl.when(j == 0)(functools.partial(step, "first"))
    pl.when((j > 0) & (j < last))(functools.partial(step, "middle"))
    pl.when(j == last)(functools.partial(step, "last"))


def _ffn(x, gpre, gpost, w_gate_up, w_down, *, tm=1024, tf=512):
    m, d = x.shape
    nj = D_FF // tf
    row_tile = pl.BlockSpec((tm, d), lambda i, j: (i, 0))
    gain_spec = pl.BlockSpec((1, d), lambda i, j: (0, 0))
    return pl.pallas_call(
        functools.partial(_ffn_kernel, emit_next_norm=False, emit_weights=False, n_passthrough=0),
        grid=(m // tm, nj),
        in_specs=[row_tile, gain_spec, gain_spec,
                  pl.BlockSpec((d, tf), lambda i, j: (0, j)),
                  pl.BlockSpec((d, tf), lambda i, j: (0, j + nj)),
                  pl.BlockSpec((tf, d), lambda i, j: (j, 0))],
        out_specs=row_tile,
        out_shape=jax.ShapeDtypeStruct((m, d), F32),
        scratch_shapes=[pltpu.VMEM((tm, d), BF16)],
        compiler_params=pltpu.CompilerParams(
            dimension_semantics=("parallel", "arbitrary"), vmem_limit_bytes=VMEM_LIMIT_BYTES),
        name="ffn",
    )(x, gpre, gpost, w_gate_up, w_gate_up, w_down)


def _ffn_f32_weights(x, gpre, gpost, gnext, w_gate_up, w_down, *, tm=1024, tf=256, slabs_per_step=2):
    m, d = x.shape
    nj = D_FF // tf
    gains = (gpre, gpost, gnext)
    gain_spec = pl.BlockSpec((1, d), lambda i, j: (0, 0))
    params = pltpu.CompilerParams(dimension_semantics=("parallel", "arbitrary"), vmem_limit_bytes=VMEM_LIMIT_BYTES)
    results = (jax.ShapeDtypeStruct((m, d), F32), jax.ShapeDtypeStruct((m, d), BF16))

    first_tile = pl.BlockSpec((tm, d), lambda i, j: (0, 0))
    col_slab = pl.BlockSpec((1, d, tf), lambda i, j: (j, 0, 0))
    row_slab = pl.BlockSpec((1, tf, d), lambda i, j: (j, 0, 0))
    out, xn, wg16, wu16, wd16 = pl.pallas_call(
        functools.partial(_ffn_kernel, emit_next_norm=True, emit_weights=True, n_passthrough=0),
        grid=(1, nj),
        in_specs=[pl.BlockSpec((tm, d), lambda i, j: (0, 0), pipeline_mode=pl.Buffered(1))] + [gain_spec] * 3 + [
            pl.BlockSpec((d, tf), lambda i, j: (0, j)),
            pl.BlockSpec((d, tf), lambda i, j: (0, j + nj)),
            pl.BlockSpec((tf, d), lambda i, j: (j, 0)),
        ],
        out_specs=(first_tile, first_tile, col_slab, col_slab, row_slab),
        out_shape=results + (jax.ShapeDtypeStruct((nj, d, tf), BF16), jax.ShapeDtypeStruct((nj, d, tf), BF16),
                             jax.ShapeDtypeStruct((nj, tf, d), BF16)),
        compiler_params=params,
        name="ffn_first",
    )(x, *gains, w_gate_up, w_gate_up, w_down)

    per = slabs_per_step
    later_tile = pl.BlockSpec((tm, d), lambda i, j: (i + 1, 0))
    untouched = pl.BlockSpec(memory_space=pl.ANY)
    return pl.pallas_call(
        functools.partial(_ffn_kernel, emit_next_norm=True, emit_weights=False, n_passthrough=2),
        grid=(m // tm - 1, nj // per),
        in_specs=[later_tile] + [gain_spec] * 3 + [
            pl.BlockSpec((per, d, tf), lambda i, j: (j, 0, 0)),
            pl.BlockSpec((per, d, tf), lambda i, j: (j, 0, 0)),
            pl.BlockSpec((per, tf, d), lambda i, j: (j, 0, 0)),
            untouched, untouched,
        ],
        out_specs=(later_tile, later_tile),
        out_shape=results,
        input_output_aliases={7: 0, 8: 1},
        compiler_params=params,
        name="ffn_rest",
    )(x, *gains, wg16, wu16, wd16, out, xn)


def _proj_kernel(xn_ref, w_ref, o_ref):
    res = jnp.dot(xn_ref[...], w_ref[...].astype(BF16), preferred_element_type=F32).astype(BF16)
    for c in range(o_ref.shape[0]):
        o_ref[c] = res[:, c * HEAD_DIM:(c + 1) * HEAD_DIM]


def _proj(xn, w_in, col_blocks, *, tm=4096, tn=512):
    m, d = xn.shape
    per = tn // HEAD_DIM
    jumps = [(pos, col_blocks[pos] - col_blocks[pos - 1] - 1) for pos in range(1, len(col_blocks))
             if col_blocks[pos] != col_blocks[pos - 1] + 1]

    def w_block(i, j):
        blk = j + col_blocks[0]
        for pos, gap in jumps:
            blk = blk + jnp.where(j >= pos, gap, 0)
        return (0, blk)

    return pl.pallas_call(
        _proj_kernel,
        grid=(m // tm, len(col_blocks)),
        in_specs=[
            pl.BlockSpec((tm, d), lambda i, j: (i, 0)),
            pl.BlockSpec((d, tn), w_block),
        ],
        out_specs=pl.BlockSpec((per, tm, HEAD_DIM), lambda i, j: (j, i, 0)),
        out_shape=jax.ShapeDtypeStruct((len(col_blocks) * per, m, HEAD_DIM), BF16),
        compiler_params=pltpu.CompilerParams(
            dimension_semantics=("parallel", "arbitrary"), vmem_limit_bytes=VMEM_LIMIT_BYTES),
        name="inproj",
    )(xn, w_in)


def _chunk_cumsum(x, row_in_chunk, reverse):
    n_rows = x.shape[0]
    c = x
    s = 1
    while s < CHUNK:
        if reverse:
            shifted = pltpu.roll(c, n_rows - s, axis=0)
            keep = row_in_chunk < CHUNK - s
        else:
            shifted = pltpu.roll(c, s, axis=0)
            keep = row_in_chunk >= s
        c = c + jnp.where(keep, shifted, 0.0)
        s *= 2
    return c


def _hgrn_kernel(q_ref, v_ref, xn_ref, wff_ref, wfb_ref, g_ref, lbf_ref, lbb_ref, gain_ref, wa_ref, wb_ref,
                 o_ref, wa16_ref, wb16_ref, qd_ref, kv_ref, dec_ref, st_ref, oi_ref):
    wa16_ref[...] = wa_ref[...].astype(BF16)
    wb16_ref[...] = wb_ref[...].astype(BF16)

    seq = q_ref.shape[0]
    n_chunks = seq // CHUNK
    pair = 2 * CHUNK
    contract_last = (((1,), (1,)), ((), ()))

    def gate_consts(lb_param_ref):
        a = lb_param_ref[...]
        e = jnp.exp(a - jnp.max(a, axis=0, keepdims=True))
        lb = e[0:1] / jnp.sum(e, axis=0, keepdims=True)
        return 0.5 * (1.0 + lb), 0.5 * (1.0 - lb)

    consts = (gate_consts(lbf_ref), gate_consts(lbb_ref))
    w_logits = jnp.concatenate([wff_ref[...].astype(BF16), wfb_ref[...].astype(BF16)], axis=1)

    r = lax.broadcasted_iota(jnp.int32, (pair, pair), 0)
    s = lax.broadcasted_iota(jnp.int32, (pair, pair), 1)
    same_chunk = (r // CHUNK) == (s // CHUNK)
    tri = (same_chunk & (r >= s), same_chunk & (s >= r))
    row_in_chunk = lax.broadcasted_iota(jnp.int32, (CHUNK, HEAD_DIM), 0)
    zero_blk = jnp.zeros((CHUNK, HEAD_DIM), BF16)

    n_pairs = n_chunks // 2
    rows_of = [pl.ds(i * pair, pair) for i in range(n_pairs)]
    pairs_per_slab = LOGIT_SLAB // pair
    n_slabs = seq // LOGIT_SLAB

    def project_logits(slab):
        return jnp.dot(xn_ref[pl.ds(slab * LOGIT_SLAB, LOGIT_SLAB), :], w_logits, preferred_element_type=F32)

    def gate_stage(i):
        slab, offset = divmod(i, pairs_per_slab)
        per_dir = []
        for d in range(2):
            mid, half = consts[d]
            ks, cs = [], []
            for r0 in range(offset * pair, (offset + 1) * pair, CHUNK):
                z = logits[slab][r0:r0 + CHUNK, d * HEAD_DIM:(d + 1) * HEAD_DIM]
                t = jnp.tanh(0.5 * z)
                f = mid + half * t
                ks.append(half * (1.0 - t))
                cs.append(_chunk_cumsum(jnp.log(f) * LOG2_E, row_in_chunk, reverse=(d == 1)))
            per_dir.append((jnp.concatenate(ks, axis=0), jnp.concatenate(cs, axis=0)))
        if offset == pairs_per_slab - 1 and slab + 2 < n_slabs:
            logits.append(project_logits(slab + 2))
        return per_dir

    def mix_stage(i, per_dir):
        rows = rows_of[i]
        q = q_ref[rows, :].astype(F32)
        vb = v_ref[rows, :]
        v_t = vb.astype(F32).T.astype(BF16)
        probs = None
        for d, (k, c) in enumerate(per_dir):
            c3 = c.reshape(2, CHUNK, HEAD_DIM)
            tot = c3[:, 0:1, :] if d == 1 else c3[:, CHUNK - 1:CHUNK, :]
            k_tail = (k * jnp.exp2(tot - c3).reshape(pair, HEAD_DIM)).astype(BF16)
            q_dec = (q * jnp.exp2(c)).astype(BF16)
            k_dec = (k * jnp.exp2(-c)).astype(BF16)
            qd_ref[rows, d * HEAD_DIM:(d + 1) * HEAD_DIM] = q_dec
            dec_ref[d, pl.ds(i * 2, 2), :] = jnp.exp2(tot).reshape(2, HEAD_DIM)
            sc = lax.dot_general(q_dec, k_dec, contract_last, preferred_element_type=F32)
            sc = jnp.where(tri[d], sc, 0.0)
            probs = sc if probs is None else probs + sc
            rhs = jnp.concatenate(
                [jnp.concatenate([k_tail[:CHUNK], zero_blk], axis=1),
                 jnp.concatenate([zero_blk, k_tail[CHUNK:]], axis=1)], axis=0)
            kv = jnp.dot(v_t, rhs, preferred_element_type=F32)
            kv_ref[d, i * 2] = kv[:, :HEAD_DIM]
            kv_ref[d, i * 2 + 1] = kv[:, HEAD_DIM:]
        return probs.astype(BF16), vb

    logits = [project_logits(s) for s in range(min(2, n_slabs))]
    gates, mixes = {}, {}
    for t in range(n_pairs + 2 * STAGE_LAG):
        if t < n_pairs:
            gates[t] = gate_stage(t)
        i = t - STAGE_LAG
        if 0 <= i < n_pairs:
            mixes[i] = mix_stage(i, gates.pop(i))
        i = t - 2 * STAGE_LAG
        if 0 <= i < n_pairs:
            probs, vb = mixes.pop(i)
            oi_ref[rows_of[i], :] = jnp.dot(probs, vb, preferred_element_type=F32)

    sf = sb = jnp.zeros((HEAD_DIM, HEAD_DIM), F32)
    for nf in range(n_chunks):
        nb = n_chunks - 1 - nf
        st_ref[nf, :, :HEAD_DIM] = sf.astype(BF16)
        st_ref[nb, :, HEAD_DIM:] = sb.astype(BF16)
        sf = sf * dec_ref[0, pl.ds(nf, 1), :] + kv_ref[0, nf]
        sb = sb * dec_ref[1, pl.ds(nb, 1), :] + kv_ref[1, nb]

    chunk_rows = [pl.ds(n * CHUNK, CHUNK) for n in range(n_chunks)]
    inter = [lax.dot_general(qd_ref[rows, :], st_ref[n], contract_last, preferred_element_type=F32)
             for n, rows in enumerate(chunk_rows)]
    for rows, o_inter in zip(chunk_rows, inter):
        o = _rms(oi_ref[rows, :] + o_inter, gain_ref[...])
        g = g_ref[rows, :].astype(F32)
        o_ref[rows, :] = (o * (0.5 * g * (1.0 + jnp.tanh(0.5 * g)))).astype(o_ref.dtype)


def _hgrn(p16, xn, w_in, lbf, lbb, out_gain, w_a, w_b, batch, seq):
    hw = HGRN_HEADS
    steps = batch * hw
    d = xn.shape[1]

    def w_col(group):
        return lambda b, h: (0, group * hw + h)

    def slab(group):
        return lambda b, h: (group * hw + h, b, 0)

    def row_slab(w):
        assert w.shape[0] % (steps * 16) == 0
        return pl.BlockSpec((w.shape[0] // steps, w.shape[1]), lambda b, h: (b * hw + h, 0))

    blk = (None, seq, HEAD_DIM)
    return pl.pallas_call(
        _hgrn_kernel,
        grid=(batch, hw),
        in_specs=[
            pl.BlockSpec(blk, slab(0)),
            pl.BlockSpec(blk, slab(1)),
            pl.BlockSpec((seq, d), lambda b, h: (b, 0)),
            pl.BlockSpec((d, HEAD_DIM), w_col(2)),
            pl.BlockSpec((d, HEAD_DIM), w_col(3)),
            pl.BlockSpec(blk, slab(2)),
            pl.BlockSpec((lbf.shape[0], HEAD_DIM), lambda b, h: (0, h)),
            pl.BlockSpec((lbb.shape[0], HEAD_DIM), lambda b, h: (0, h)),
            pl.BlockSpec((1, HEAD_DIM), lambda b, h: (0, h)),
            row_slab(w_a),
            row_slab(w_b),
        ],
        out_specs=(pl.BlockSpec(blk, lambda b, h: (h, b, 0)), row_slab(w_a), row_slab(w_b)),
        out_shape=(jax.ShapeDtypeStruct((hw, batch * seq, HEAD_DIM), BF16),
                   jax.ShapeDtypeStruct(w_a.shape, BF16), jax.ShapeDtypeStruct(w_b.shape, BF16)),
        scratch_shapes=[
            pltpu.VMEM((seq, 2 * HEAD_DIM), BF16),
            pltpu.VMEM((2, seq // CHUNK, HEAD_DIM, HEAD_DIM), F32),
            pltpu.VMEM((2, seq // CHUNK, HEAD_DIM), F32),
            pltpu.VMEM((seq // CHUNK, HEAD_DIM, 2 * HEAD_DIM), BF16),
            pltpu.VMEM((seq, HEAD_DIM), F32),
        ],
        compiler_params=pltpu.CompilerParams(
            dimension_semantics=("parallel", "parallel"), vmem_limit_bytes=VMEM_LIMIT_BYTES),
        name="hgrn",
    )(p16, p16, xn, w_in, w_in, p16, lbf, lbb, out_gain, w_a, w_b)


def _t5_bucket_table():
    nb = REL_BUCKETS // 2
    max_exact = nb // 2
    c = np.arange(ATTN_BLOCK)[:, None]
    s = np.arange(KEY_SPAN)[None, :]
    rel = s - WINDOW - c
    bucket = (rel > 0).astype(np.int32) * nb
    n = np.abs(rel)
    large = max_exact + (np.log(np.maximum(n, 1) / max_exact) / np.log(REL_MAX_DIST / max_exact)
                         * (nb - max_exact)).astype(np.int32)
    large = np.minimum(large, nb - 1)
    bucket = bucket + np.where(n < max_exact, n, large).astype(np.int32)
    return np.where(np.abs(rel) <= WINDOW, bucket, -1).astype(np.int32)


def _attn_kernel(table_ref, sink_ref, bucket_ref, q_ref, k_ref, v_ref, w_ref, o_ref, w16_ref, bias_ref):
    w16_ref[...] = w_ref[...].astype(BF16)

    x = pl.program_id(1)
    seq = k_ref.shape[0]
    n_blocks = seq // ATTN_BLOCK
    scale = LOG2_E / math.sqrt(HEAD_DIM)
    rows = GROUP * ATTN_BLOCK

    @pl.when((pl.program_id(0) == 0) & (x == 0))
    def _():
        bucket = bucket_ref[...]
        for head in range(ATTN_HEADS):
            bias = jnp.full((ATTN_BLOCK, KEY_SPAN), NEG_INF, F32)
            for b in range(REL_BUCKETS):
                bias = jnp.where(bucket == b, table_ref[b, head] * LOG2_E, bias)
            bias_ref[pl.ds(head * ATTN_BLOCK, ATTN_BLOCK), :] = bias

    row_head = lax.broadcasted_iota(jnp.int32, (rows, 1), 0) // ATTN_BLOCK
    sink = jnp.zeros((rows, 1), F32)
    for g in range(GROUP):
        sink = jnp.where(row_head == g, sink_ref[0, x * GROUP + g] * LOG2_E, sink)
    bias_rows = pl.ds(pl.multiple_of(x * rows, rows), rows)

    def block_rows(i):
        return pl.ds(i * ATTN_BLOCK, ATTN_BLOCK)

    def band(ref, i):
        return jnp.concatenate([ref[block_rows(max(i - 1, 0)), :], ref[block_rows(i), :],
                                ref[block_rows(min(i + 1, n_blocks - 1)), :]], axis=0)

    def scores(i):
        qs = jnp.concatenate([q_ref[g, block_rows(i), :] for g in range(GROUP)], axis=0)
        sc = lax.dot_general(qs, band(k_ref, i), (((1,), (1,)), ((), ())), preferred_element_type=F32)
        sc = sc * scale + bias_ref[bias_rows, :]
        if i in (0, n_blocks - 1):
            key_pos = (i - 1) * ATTN_BLOCK + lax.broadcasted_iota(jnp.int32, (rows, KEY_SPAN), 1)
            sc = jnp.where((key_pos >= 0) & (key_pos < seq), sc, NEG_INF)
        return sc

    def finish(i, sc):
        m = jnp.maximum(jnp.max(sc, axis=-1, keepdims=True), sink)
        pr = jnp.exp2(sc - m)
        den = jnp.sum(pr, axis=-1, keepdims=True) + jnp.exp2(sink - m)
        o = jnp.dot(pr.astype(BF16), band(v_ref, i), preferred_element_type=F32) / den
        for g in range(GROUP):
            o_ref[g, block_rows(i), :] = o[g * ATTN_BLOCK:(g + 1) * ATTN_BLOCK].astype(o_ref.dtype)

    logits = {i: scores(i) for i in range(ATTN_SCORE_LEAD)}
    for i in range(n_blocks):
        if i + ATTN_SCORE_LEAD < n_blocks:
            logits[i + ATTN_SCORE_LEAD] = scores(i + ATTN_SCORE_LEAD)
        finish(i, logits.pop(i))


def _attention(p16, sink, rel_table, w, batch, seq):
    q_blk0 = 3 * HGRN_HEADS // GROUP
    k_slab0 = 3 * HGRN_HEADS + ATTN_HEADS
    v_slab0 = k_slab0 + KV_HEADS
    bucket = jnp.asarray(_t5_bucket_table())
    smem = pl.BlockSpec(memory_space=pltpu.SMEM)
    assert w.shape[0] % (batch * KV_HEADS * 16) == 0
    w_slab = pl.BlockSpec((w.shape[0] // (batch * KV_HEADS), w.shape[1]), lambda b, x: (b * KV_HEADS + x, 0))
    return pl.pallas_call(
        _attn_kernel,
        grid=(batch, KV_HEADS),
        in_specs=[
            smem,
            smem,
            pl.BlockSpec((ATTN_BLOCK, KEY_SPAN), lambda b, x: (0, 0)),
            pl.BlockSpec((GROUP, seq, HEAD_DIM), lambda b, x: (q_blk0 + x, b, 0)),
            pl.BlockSpec((None, seq, HEAD_DIM), lambda b, x: (k_slab0 + x, b, 0)),
            pl.BlockSpec((None, seq, HEAD_DIM), lambda b, x: (v_slab0 + x, b, 0)),
            w_slab,
        ],
        out_specs=(pl.BlockSpec((GROUP, seq, HEAD_DIM), lambda b, x: (x, b, 0)), w_slab),
        out_shape=(jax.ShapeDtypeStruct((ATTN_HEADS, batch * seq, HEAD_DIM), BF16),
                   jax.ShapeDtypeStruct(w.shape, BF16)),
        scratch_shapes=[
            pltpu.VMEM((ATTN_HEADS * ATTN_BLOCK, KEY_SPAN), F32),
        ],
        compiler_params=pltpu.CompilerParams(
            dimension_semantics=("arbitrary", "arbitrary"), vmem_limit_bytes=VMEM_LIMIT_BYTES),
        name="attention",
    )(rel_table, sink, bucket, p16, p16, p16, w)


def _outproj_kernel(yh_ref, ya_ref, x_ref, wh_ref, wa_ref, g_ref, o_ref):
    y_h = jnp.concatenate([yh_ref[c] for c in range(yh_ref.shape[0])], axis=1)
    y_a = jnp.concatenate([ya_ref[c] for c in range(ya_ref.shape[0])], axis=1)
    mixed = jnp.dot(y_h, wh_ref[...], preferred_element_type=F32)
    mixed = mixed + jnp.dot(y_a, wa_ref[...], preferred_element_type=F32)
    o_ref[...] = x_ref[...] + _rms(mixed, g_ref[...])


def _outproj(y_h, y_a, x, w_out, gain, *, tm=512):
    m, d = x.shape
    return pl.pallas_call(
        _outproj_kernel,
        grid=(m // tm,),
        in_specs=[
            pl.BlockSpec((HGRN_HEADS, tm, HEAD_DIM), lambda i: (0, i, 0)),
            pl.BlockSpec((ATTN_HEADS, tm, HEAD_DIM), lambda i: (0, i, 0)),
            pl.BlockSpec((tm, d), lambda i: (i, 0)),
            pl.BlockSpec((HGRN_WIDTH, d), lambda i: (0, 0)),
            pl.BlockSpec((ATTN_WIDTH, d), lambda i: (1, 0)),
            pl.BlockSpec((1, d), lambda i: (0, 0)),
        ],
        out_specs=pl.BlockSpec((tm, d), lambda i: (i, 0)),
        out_shape=jax.ShapeDtypeStruct((m, d), F32),
        compiler_params=pltpu.CompilerParams(
            dimension_semantics=("parallel",), vmem_limit_bytes=VMEM_LIMIT_BYTES),
        name="outproj",
    )(y_h, y_a, x, w_out, w_out, gain)


def kernel(x, pre_norm_ffn1, post_norm_ffn1, w_ffn1_gate_up, w_ffn1_down, pre_norm_mix, post_norm_mix,
           w_mix_in, hgrn_lower_bounds_fwd, hgrn_lower_bounds_bwd, hgrn_out_norm, attn_sink, w_mix_out,
           pre_norm_ffn2, post_norm_ffn2, w_ffn2_gate_up, w_ffn2_down, rel_bias_table):
    batch, seq, d = x.shape
    depth = pre_norm_ffn1.shape[0]
    assert depth == 1 and d == D_MODEL
    xf = x.reshape(batch * seq, d)
    layer = 0
    xf, xn = _ffn_f32_weights(xf, pre_norm_ffn1[layer:layer + 1], post_norm_ffn1[layer:layer + 1],
                              pre_norm_mix[layer:layer + 1], w_ffn1_gate_up[layer], w_ffn1_down[layer])
    w_in = w_mix_in[layer]
    p16 = _proj(xn, w_in, [0, 1, 2, 3, 8, 9, 10, 11, 12])
    y_h, w2_gate_up, w2_down = _hgrn(p16, xn, w_in, hgrn_lower_bounds_fwd, hgrn_lower_bounds_bwd,
                                     hgrn_out_norm[layer:layer + 1], w_ffn2_gate_up[layer], w_ffn2_down[layer],
                                     batch, seq)
    y_a, w_out = _attention(p16, attn_sink[layer:layer + 1], rel_bias_table, w_mix_out[layer], batch, seq)
    xf = _outproj(y_h, y_a, xf, w_out, post_norm_mix[layer:layer + 1])
    xf = _ffn(xf, pre_norm_ffn2[layer:layer + 1], post_norm_ffn2[layer:layer + 1], w2_gate_up, w2_down)
    return xf.reshape(batch, seq, d)
```

```python
import functools
import math

import jax
import jax.numpy as jnp
import numpy as np
from jax import lax
from jax.experimental import pallas as pl
from jax.experimental.pallas import tpu as pltpu

F32 = jnp.float32
BF16 = jnp.bfloat16

D_MODEL = 2048
HGRN_WIDTH = 1024
HEAD_DIM = 128
HGRN_HEADS = HGRN_WIDTH // HEAD_DIM
CHUNK = 64
ATTN_WIDTH = 1024
ATTN_HEADS = ATTN_WIDTH // HEAD_DIM
KV_HEADS = 2
GROUP = ATTN_HEADS // KV_HEADS
WINDOW = 128
ATTN_BLOCK = 128
KEY_SPAN = ATTN_BLOCK + 2 * WINDOW
REL_BUCKETS = 32
REL_MAX_DIST = 128
D_FF = 5632
EPS = 1e-6
NEG_INF = -1e30
LOG2_E = 1.0 / math.log(2.0)

MIB = 1024 * 1024
VMEM_LIMIT_BYTES = 60 * MIB
ROW_BLOCK = 32
EDGE_CHUNK = 256
STAGE_LAG = 10
LOGIT_SLAB = 256
ATTN_SCORE_LEAD = 2


def _rms(x, gain):
    return x * lax.rsqrt(jnp.mean(x * x, axis=-1, keepdims=True) + EPS) * gain


def _bf16_weight(w_ref, axis):
    if len(w_ref.shape) == 3:
        return jnp.concatenate([w_ref[c] for c in range(w_ref.shape[0])], axis=axis)
    return w_ref[...].astype(BF16)


def _ffn_kernel(*refs, emit_next_norm, emit_weights, n_passthrough):
    x_ref, gpre_ref, gpost_ref = refs[:3]
    k = 3
    gnext_ref = refs[k] if emit_next_norm else None
    k += int(emit_next_norm)
    wg_ref, wu_ref, wd_ref = refs[k:k + 3]
    k += 3 + n_passthrough
    o_ref, h_ref = refs[k:k + 2]
    j = pl.program_id(1)
    tm = x_ref.shape[0]

    def pre_norm_rows(r0, r1):
        for r in range(r0, r1, ROW_BLOCK):
            rows = pl.ds(r, ROW_BLOCK)
            h_ref[rows, :] = _rms(x_ref[rows, :], gpre_ref[...]).astype(BF16)

    def finish_rows(r0, r1):
        for r in range(r0, r1, ROW_BLOCK):
            rows = pl.ds(r, ROW_BLOCK)
            new_x = x_ref[rows, :] + 0.5 * _rms(o_ref[rows, :], gpost_ref[...])
            o_ref[rows, :] = new_x
            if emit_next_norm:
                h_ref[rows, :] = _rms(new_x, gnext_ref[...]).astype(BF16)

    def swiglu(h, w_gate, w_up):
        gate = jnp.dot(h, w_gate, preferred_element_type=F32)
        up = jnp.dot(h, w_up, preferred_element_type=F32)
        return (0.5 * gate * (1.0 + jnp.tanh(0.5 * gate)) * up).astype(BF16)

    def step(position):
        w_gate, w_up, w_down = _bf16_weight(wg_ref, 1), _bf16_weight(wu_ref, 1), _bf16_weight(wd_ref, 0)
        if emit_weights:
            for w16_ref, w in zip(refs[k + 2:k + 5], (w_gate, w_up, w_down)):
                w16_ref[0] = w
        if position == "first":
            acts = []
            for c0 in range(0, tm, EDGE_CHUNK):
                pre_norm_rows(c0, c0 + EDGE_CHUNK)
                acts.append(swiglu(h_ref[pl.ds(c0, EDGE_CHUNK), :], w_gate, w_up))
            o_ref[...] = jnp.dot(jnp.concatenate(acts, axis=0), w_down, preferred_element_type=F32)
            return
        act = swiglu(h_ref[...], w_gate, w_up)
        if position == "middle":
            o_ref[...] += jnp.dot(act, w_down, preferred_element_type=F32)
            return
        for c0 in range(0, tm, EDGE_CHUNK):
            rows = pl.ds(c0, EDGE_CHUNK)
            o_ref[rows, :] += jnp.dot(act[c0:c0 + EDGE_CHUNK], w_down, preferred_element_type=F32)
            finish_rows(c0, c0 + EDGE_CHUNK)

    last = pl.num_programs(1) - 1
    pl.when(j == 0)(functools.partial(step, "first"))
    pl.when((j > 0) & (j < last))(functools.partial(step, "middle"))
    pl.when(j == last)(functools.partial(step, "last"))


def _ffn(x, gpre, gpost, w_gate_up, w_down, *, tm=1024, tf=512):
    m, d = x.shape
    nj = D_FF // tf
    row_tile = pl.BlockSpec((tm, d), lambda i, j: (i, 0))
    gain_spec = pl.BlockSpec((1, d), lambda i, j: (0, 0))
    return pl.pallas_call(
        functools.partial(_ffn_kernel, emit_next_norm=False, emit_weights=False, n_passthrough=0),
        grid=(m // tm, nj),
        in_specs=[row_tile, gain_spec, gain_spec,
                  pl.BlockSpec((d, tf), lambda i, j: (0, j)),
                  pl.BlockSpec((d, tf), lambda i, j: (0, j + nj)),
                  pl.BlockSpec((tf, d), lambda i, j: (j, 0))],
        out_specs=row_tile,
        out_shape=jax.ShapeDtypeStruct((m, d), F32),
        scratch_shapes=[pltpu.VMEM((tm, d), BF16)],
        compiler_params=pltpu.CompilerParams(
            dimension_semantics=("parallel", "arbitrary"), vmem_limit_bytes=VMEM_LIMIT_BYTES),
        name="ffn",
    )(x, gpre, gpost, w_gate_up, w_gate_up, w_down)


def _ffn_f32_weights(x, gpre, gpost, gnext, w_gate_up, w_down, *, tm=1024, tf=256, slabs_per_step=2):
    m, d = x.shape
    nj = D_FF // tf
    gains = (gpre, gpost, gnext)
    gain_spec = pl.BlockSpec((1, d), lambda i, j: (0, 0))
    params = pltpu.CompilerParams(dimension_semantics=("parallel", "arbitrary"), vmem_limit_bytes=VMEM_LIMIT_BYTES)
    results = (jax.ShapeDtypeStruct((m, d), F32), jax.ShapeDtypeStruct((m, d), BF16))

    first_tile = pl.BlockSpec((tm, d), lambda i, j: (0, 0))
    col_slab = pl.BlockSpec((1, d, tf), lambda i, j: (j, 0, 0))
    row_slab = pl.BlockSpec((1, tf, d), lambda i, j: (j, 0, 0))
    out, xn, wg16, wu16, wd16 = pl.pallas_call(
        functools.partial(_ffn_kernel, emit_next_norm=True, emit_weights=True, n_passthrough=0),
        grid=(1, nj),
        in_specs=[pl.BlockSpec((tm, d), lambda i, j: (0, 0), pipeline_mode=pl.Buffered(1))] + [gain_spec] * 3 + [
            pl.BlockSpec((d, tf), lambda i, j: (0, j)),
            pl.BlockSpec((d, tf), lambda i, j: (0, j + nj)),
            pl.BlockSpec((tf, d), lambda i, j: (j, 0)),
        ],
        out_specs=(first_tile, first_tile, col_slab, col_slab, row_slab),
        out_shape=results + (jax.ShapeDtypeStruct((nj, d, tf), BF16), jax.ShapeDtypeStruct((nj, d, tf), BF16),
                             jax.ShapeDtypeStruct((nj, tf, d), BF16)),
        compiler_params=params,
        name="ffn_first",
    )(x, *gains, w_gate_up, w_gate_up, w_down)

    per = slabs_per_step
    later_tile = pl.BlockSpec((tm, d), lambda i, j: (i + 1, 0))
    untouched = pl.BlockSpec(memory_space=pl.ANY)
    return pl.pallas_call(
        functools.partial(_ffn_kernel, emit_next_norm=True, emit_weights=False, n_passthrough=2),
        grid=(m // tm - 1, nj // per),
        in_specs=[later_tile] + [gain_spec] * 3 + [
            pl.BlockSpec((per, d, tf), lambda i, j: (j, 0, 0)),
            pl.BlockSpec((per, d, tf), lambda i, j: (j, 0, 0)),
            pl.BlockSpec((per, tf, d), lambda i, j: (j, 0, 0)),
            untouched, untouched,
        ],
        out_specs=(later_tile, later_tile),
        out_shape=results,
        input_output_aliases={7: 0, 8: 1},
        compiler_params=params,
        name="ffn_rest",
    )(x, *gains, wg16, wu16, wd16, out, xn)


def _proj_kernel(xn_ref, w_ref, o_ref):
    res = jnp.dot(xn_ref[...], w_ref[...].astype(BF16), preferred_element_type=F32).astype(BF16)
    for c in range(o_ref.shape[0]):
        o_ref[c] = res[:, c * HEAD_DIM:(c + 1) * HEAD_DIM]


def _proj(xn, w_in, col_blocks, *, tm=4096, tn=512):
    m, d = xn.shape
    per = tn // HEAD_DIM
    jumps = [(pos, col_blocks[pos] - col_blocks[pos - 1] - 1) for pos in range(1, len(col_blocks))
             if col_blocks[pos] != col_blocks[pos - 1] + 1]

    def w_block(i, j):
        blk = j + col_blocks[0]
        for pos, gap in jumps:
            blk = blk + jnp.where(j >= pos, gap, 0)
        return (0, blk)

    return pl.pallas_call(
        _proj_kernel,
        grid=(m // tm, len(col_blocks)),
        in_specs=[
            pl.BlockSpec((tm, d), lambda i, j: (i, 0)),
            pl.BlockSpec((d, tn), w_block),
        ],
        out_specs=pl.BlockSpec((per, tm, HEAD_DIM), lambda i, j: (j, i, 0)),
        out_shape=jax.ShapeDtypeStruct((len(col_blocks) * per, m, HEAD_DIM), BF16),
        compiler_params=pltpu.CompilerParams(
            dimension_semantics=("parallel", "arbitrary"), vmem_limit_bytes=VMEM_LIMIT_BYTES),
        name="inproj",
    )(xn, w_in)


def _chunk_cumsum(x, row_in_chunk, reverse):
    n_rows = x.shape[0]
    c = x
    s = 1
    while s < CHUNK:
        if reverse:
            shifted = pltpu.roll(c, n_rows - s, axis=0)
            keep = row_in_chunk < CHUNK - s
        else:
            shifted = pltpu.roll(c, s, axis=0)
            keep = row_in_chunk >= s
        c = c + jnp.where(keep, shifted, 0.0)
        s *= 2
    return c


def _hgrn_kernel(q_ref, v_ref, xn_ref, wff_ref, wfb_ref, g_ref, lbf_ref, lbb_ref, gain_ref, wa_ref, wb_ref,
                 o_ref, wa16_ref, wb16_ref, qd_ref, kv_ref, dec_ref, st_ref, oi_ref):
    wa16_ref[...] = wa_ref[...].astype(BF16)
    wb16_ref[...] = wb_ref[...].astype(BF16)

    seq = q_ref.shape[0]
    n_chunks = seq // CHUNK
    pair = 2 * CHUNK
    contract_last = (((1,), (1,)), ((), ()))

    def gate_consts(lb_param_ref):
        a = lb_param_ref[...]
        e = jnp.exp(a - jnp.max(a, axis=0, keepdims=True))
        lb = e[0:1] / jnp.sum(e, axis=0, keepdims=True)
        return 0.5 * (1.0 + lb), 0.5 * (1.0 - lb)

    consts = (gate_consts(lbf_ref), gate_consts(lbb_ref))
    w_logits = jnp.concatenate([wff_ref[...].astype(BF16), wfb_ref[...].astype(BF16)], axis=1)

    r = lax.broadcasted_iota(jnp.int32, (pair, pair), 0)
    s = lax.broadcasted_iota(jnp.int32, (pair, pair), 1)
    same_chunk = (r // CHUNK) == (s // CHUNK)
    tri = (same_chunk & (r >= s), same_chunk & (s >= r))
    row_in_chunk = lax.broadcasted_iota(jnp.int32, (CHUNK, HEAD_DIM), 0)
    zero_blk = jnp.zeros((CHUNK, HEAD_DIM), BF16)

    n_pairs = n_chunks // 2
    rows_of = [pl.ds(i * pair, pair) for i in range(n_pairs)]
    pairs_per_slab = LOGIT_SLAB // pair
    n_slabs = seq // LOGIT_SLAB

    def project_logits(slab):
        return jnp.dot(xn_ref[pl.ds(slab * LOGIT_SLAB, LOGIT_SLAB), :], w_logits, preferred_element_type=F32)

    def gate_stage(i):
        slab, offset = divmod(i, pairs_per_slab)
        per_dir = []
        for d in range(2):
            mid, half = consts[d]
            ks, cs = [], []
            for r0 in range(offset * pair, (offset + 1) * pair, CHUNK):
                z = logits[slab][r0:r0 + CHUNK, d * HEAD_DIM:(d + 1) * HEAD_DIM]
                t = jnp.tanh(0.5 * z)
                f = mid + half * t
                ks.append(half * (1.0 - t))
                cs.append(_chunk_cumsum(jnp.log(f) * LOG2_E, row_in_chunk, reverse=(d == 1)))
            per_dir.append((jnp.concatenate(ks, axis=0), jnp.concatenate(cs, axis=0)))
        if offset == pairs_per_slab - 1 and slab + 2 < n_slabs:
            logits.append(project_logits(slab + 2))
        return per_dir

    def mix_stage(i, per_dir):
        rows = rows_of[i]
        q = q_ref[rows, :].astype(F32)
        vb = v_ref[rows, :]
        v_t = vb.astype(F32).T.astype(BF16)
        probs = None
        for d, (k, c) in enumerate(per_dir):
            c3 = c.reshape(2, CHUNK, HEAD_DIM)
            tot = c3[:, 0:1, :] if d == 1 else c3[:, CHUNK - 1:CHUNK, :]
            k_tail = (k * jnp.exp2(tot - c3).reshape(pair, HEAD_DIM)).astype(BF16)
            q_dec = (q * jnp.exp2(c)).astype(BF16)
            k_dec = (k * jnp.exp2(-c)).astype(BF16)
            qd_ref[rows, d * HEAD_DIM:(d + 1) * HEAD_DIM] = q_dec
            dec_ref[d, pl.ds(i * 2, 2), :] = jnp.exp2(tot).reshape(2, HEAD_DIM)
            sc = lax.dot_general(q_dec, k_dec, contract_last, preferred_element_type=F32)
            sc = jnp.where(tri[d], sc, 0.0)
            probs = sc if probs is None else probs + sc
            rhs = jnp.concatenate(
                [jnp.concatenate([k_tail[:CHUNK], zero_blk], axis=1),
                 jnp.concatenate([zero_blk, k_tail[CHUNK:]], axis=1)], axis=0)
            kv = jnp.dot(v_t, rhs, preferred_element_type=F32)
            kv_ref[d, i * 2] = kv[:, :HEAD_DIM]
            kv_ref[d, i * 2 + 1] = kv[:, HEAD_DIM:]
        return probs.astype(BF16), vb

    logits = [project_logits(s) for s in range(min(2, n_slabs))]
    gates, mixes = {}, {}
    for t in range(n_pairs + 2 * STAGE_LAG):
        if t < n_pairs:
            gates[t] = gate_stage(t)
        i = t - STAGE_LAG
        if 0 <= i < n_pairs:
            mixes[i] = mix_stage(i, gates.pop(i))
        i = t - 2 * STAGE_LAG
        if 0 <= i < n_pairs:
            probs, vb = mixes.pop(i)
            oi_ref[rows_of[i], :] = jnp.dot(probs, vb, preferred_element_type=F32)

    sf = sb = jnp.zeros((HEAD_DIM, HEAD_DIM), F32)
    for nf in range(n_chunks):
        nb = n_chunks - 1 - nf
        st_ref[nf, :, :HEAD_DIM] = sf.astype(BF16)
        st_ref[nb, :, HEAD_DIM:] = sb.astype(BF16)
        sf = sf * dec_ref[0, pl.ds(nf, 1), :] + kv_ref[0, nf]
        sb = sb * dec_ref[1, pl.ds(nb, 1), :] + kv_ref[1, nb]

    chunk_rows = [pl.ds(n * CHUNK, CHUNK) for n in range(n_chunks)]
    inter = [lax.dot_general(qd_ref[rows, :], st_ref[n], contract_last, preferred_element_type=F32)
             for n, rows in enumerate(chunk_rows)]
    for rows, o_inter in zip(chunk_rows, inter):
        o = _rms(oi_ref[rows, :] + o_inter, gain_ref[...])
        g = g_ref[rows, :].astype(F32)
        o_ref[rows, :] = (o * (0.5 * g * (1.0 + jnp.tanh(0.5 * g)))).astype(o_ref.dtype)


def _hgrn(p16, xn, w_in, lbf, lbb, out_gain, w_a, w_b, batch, seq):
    hw = HGRN_HEADS
    steps = batch * hw
    d = xn.shape[1]

    def w_col(group):
        return lambda b, h: (0, group * hw + h)

    def slab(group):
        return lambda b, h: (group * hw + h, b, 0)

    def row_slab(w):
        assert w.shape[0] % (steps * 16) == 0
        return pl.BlockSpec((w.shape[0] // steps, w.shape[1]), lambda b, h: (b * hw + h, 0))

    blk = (None, seq, HEAD_DIM)
    return pl.pallas_call(
        _hgrn_kernel,
        grid=(batch, hw),
        in_specs=[
            pl.BlockSpec(blk, slab(0)),
            pl.BlockSpec(blk, slab(1)),
            pl.BlockSpec((seq, d), lambda b, h: (b, 0)),
            pl.BlockSpec((d, HEAD_DIM), w_col(2)),
            pl.BlockSpec((d, HEAD_DIM), w_col(3)),
            pl.BlockSpec(blk, slab(2)),
            pl.BlockSpec((lbf.shape[0], HEAD_DIM), lambda b, h: (0, h)),
            pl.BlockSpec((lbb.shape[0], HEAD_DIM), lambda b, h: (0, h)),
            pl.BlockSpec((1, HEAD_DIM), lambda b, h: (0, h)),
            row_slab(w_a),
            row_slab(w_b),
        ],
        out_specs=(pl.BlockSpec(blk, lambda b, h: (h, b, 0)), row_slab(w_a), row_slab(w_b)),
        out_shape=(jax.ShapeDtypeStruct((hw, batch * seq, HEAD_DIM), BF16),
                   jax.ShapeDtypeStruct(w_a.shape, BF16), jax.ShapeDtypeStruct(w_b.shape, BF16)),
        scratch_shapes=[
            pltpu.VMEM((seq, 2 * HEAD_DIM), BF16),
            pltpu.VMEM((2, seq // CHUNK, HEAD_DIM, HEAD_DIM), F32),
            pltpu.VMEM((2, seq // CHUNK, HEAD_DIM), F32),
            pltpu.VMEM((seq // CHUNK, HEAD_DIM, 2 * HEAD_DIM), BF16),
            pltpu.VMEM((seq, HEAD_DIM), F32),
        ],
        compiler_params=pltpu.CompilerParams(
            dimension_semantics=("parallel", "parallel"), vmem_limit_bytes=VMEM_LIMIT_BYTES),
        name="hgrn",
    )(p16, p16, xn, w_in, w_in, p16, lbf, lbb, out_gain, w_a, w_b)


def _t5_bucket_table():
    nb = REL_BUCKETS // 2
    max_exact = nb // 2
    c = np.arange(ATTN_BLOCK)[:, None]
    s = np.arange(KEY_SPAN)[None, :]
    rel = s - WINDOW - c
    bucket = (rel > 0).astype(np.int32) * nb
    n = np.abs(rel)
    large = max_exact + (np.log(np.maximum(n, 1) / max_exact) / np.log(REL_MAX_DIST / max_exact)
                         * (nb - max_exact)).astype(np.int32)
    large = np.minimum(large, nb - 1)
    bucket = bucket + np.where(n < max_exact, n, large).astype(np.int32)
    return np.where(np.abs(rel) <= WINDOW, bucket, -1).astype(np.int32)


def _attn_kernel(table_ref, sink_ref, bucket_ref, q_ref, k_ref, v_ref, w_ref, o_ref, w16_ref, bias_ref):
    w16_ref[...] = w_ref[...].astype(BF16)

    x = pl.program_id(1)
    seq = k_ref.shape[0]
    n_blocks = seq // ATTN_BLOCK
    scale = LOG2_E / math.sqrt(HEAD_DIM)
    rows = GROUP * ATTN_BLOCK

    @pl.when((pl.program_id(0) == 0) & (x == 0))
    def _():
        bucket = bucket_ref[...]
        for head in range(ATTN_HEADS):
            bias = jnp.full((ATTN_BLOCK, KEY_SPAN), NEG_INF, F32)
            for b in range(REL_BUCKETS):
                bias = jnp.where(bucket == b, table_ref[b, head] * LOG2_E, bias)
            bias_ref[pl.ds(head * ATTN_BLOCK, ATTN_BLOCK), :] = bias

    row_head = lax.broadcasted_iota(jnp.int32, (rows, 1), 0) // ATTN_BLOCK
    sink = jnp.zeros((rows, 1), F32)
    for g in range(GROUP):
        sink = jnp.where(row_head == g, sink_ref[0, x * GROUP + g] * LOG2_E, sink)
    bias_rows = pl.ds(pl.multiple_of(x * rows, rows), rows)

    def block_rows(i):
        return pl.ds(i * ATTN_BLOCK, ATTN_BLOCK)

    def band(ref, i):
        return jnp.concatenate([ref[block_rows(max(i - 1, 0)), :], ref[block_rows(i), :],
                                ref[block_rows(min(i + 1, n_blocks - 1)), :]], axis=0)

    def scores(i):
        qs = jnp.concatenate([q_ref[g, block_rows(i), :] for g in range(GROUP)], axis=0)
        sc = lax.dot_general(qs, band(k_ref, i), (((1,), (1,)), ((), ())), preferred_element_type=F32)
        sc = sc * scale + bias_ref[bias_rows, :]
        if i in (0, n_blocks - 1):
            key_pos = (i - 1) * ATTN_BLOCK + lax.broadcasted_iota(jnp.int32, (rows, KEY_SPAN), 1)
            sc = jnp.where((key_pos >= 0) & (key_pos < seq), sc, NEG_INF)
        return sc

    def finish(i, sc):
        m = jnp.maximum(jnp.max(sc, axis=-1, keepdims=True), sink)
        pr = jnp.exp2(sc - m)
        den = jnp.sum(pr, axis=-1, keepdims=True) + jnp.exp2(sink - m)
        o = jnp.dot(pr.astype(BF16), band(v_ref, i), preferred_element_type=F32) / den
        for g in range(GROUP):
            o_ref[g, block_rows(i), :] = o[g * ATTN_BLOCK:(g + 1) * ATTN_BLOCK].astype(o_ref.dtype)

    logits = {i: scores(i) for i in range(ATTN_SCORE_LEAD)}
    for i in range(n_blocks):
        if i + ATTN_SCORE_LEAD < n_blocks:
            logits[i + ATTN_SCORE_LEAD] = scores(i + ATTN_SCORE_LEAD)
        finish(i, logits.pop(i))


def _attention(p16, sink, rel_table, w, batch, seq):
    q_blk0 = 3 * HGRN_HEADS // GROUP
    k_slab0 = 3 * HGRN_HEADS + ATTN_HEADS
    v_slab0 = k_slab0 + KV_HEADS
    bucket = jnp.asarray(_t5_bucket_table())
    smem = pl.BlockSpec(memory_space=pltpu.SMEM)
    assert w.shape[0] % (batch * KV_HEADS * 16) == 0
    w_slab = pl.BlockSpec((w.shape[0] // (batch * KV_HEADS), w.shape[1]), lambda b, x: (b * KV_HEADS + x, 0))
    return pl.pallas_call(
        _attn_kernel,
        grid=(batch, KV_HEADS),
        in_specs=[
            smem,
            smem,
            pl.BlockSpec((ATTN_BLOCK, KEY_SPAN), lambda b, x: (0, 0)),
            pl.BlockSpec((GROUP, seq, HEAD_DIM), lambda b, x: (q_blk0 + x, b, 0)),
            pl.BlockSpec((None, seq, HEAD_DIM), lambda b, x: (k_slab0 + x, b, 0)),
            pl.BlockSpec((None, seq, HEAD_DIM), lambda b, x: (v_slab0 + x, b, 0)),
            w_slab,
        ],
        out_specs=(pl.BlockSpec((GROUP, seq, HEAD_DIM), lambda b, x: (x, b, 0)), w_slab),
        out_shape=(jax.ShapeDtypeStruct((ATTN_HEADS, batch * seq, HEAD_DIM), BF16),
                   jax.ShapeDtypeStruct(w.shape, BF16)),
        scratch_shapes=[
            pltpu.VMEM((ATTN_HEADS * ATTN_BLOCK, KEY_SPAN), F32),
        ],
        compiler_params=pltpu.CompilerParams(
            dimension_semantics=("arbitrary", "arbitrary"), vmem_limit_bytes=VMEM_LIMIT_BYTES),
        name="attention",
    )(rel_table, sink, bucket, p16, p16, p16, w)


def _outproj_kernel(yh_ref, ya_ref, x_ref, wh_ref, wa_ref, g_ref, o_ref):
    y_h = jnp.concatenate([yh_ref[c] for c in range(yh_ref.shape[0])], axis=1)
    y_a = jnp.concatenate([ya_ref[c] for c in range(ya_ref.shape[0])], axis=1)
    mixed = jnp.dot(y_h, wh_ref[...], preferred_element_type=F32)
    mixed = mixed + jnp.dot(y_a, wa_ref[...], preferred_element_type=F32)
    o_ref[...] = x_ref[...] + _rms(mixed, g_ref[...])


def _outproj(y_h, y_a, x, w_out, gain, *, tm=512):
    m, d = x.shape
    return pl.pallas_call(
        _outproj_kernel,
        grid=(m // tm,),
        in_specs=[
            pl.BlockSpec((HGRN_HEADS, tm, HEAD_DIM), lambda i: (0, i, 0)),
            pl.BlockSpec((ATTN_HEADS, tm, HEAD_DIM), lambda i: (0, i, 0)),
            pl.BlockSpec((tm, d), lambda i: (i, 0)),
            pl.BlockSpec((HGRN_WIDTH, d), lambda i: (0, 0)),
            pl.BlockSpec((ATTN_WIDTH, d), lambda i: (1, 0)),
            pl.BlockSpec((1, d), lambda i: (0, 0)),
        ],
        out_specs=pl.BlockSpec((tm, d), lambda i: (i, 0)),
        out_shape=jax.ShapeDtypeStruct((m, d), F32),
        compiler_params=pltpu.CompilerParams(
            dimension_semantics=("parallel",), vmem_limit_bytes=VMEM_LIMIT_BYTES),
        name="outproj",
    )(y_h, y_a, x, w_out, w_out, gain)


def kernel(x, pre_norm_ffn1, post_norm_ffn1, w_ffn1_gate_up, w_ffn1_down, pre_norm_mix, post_norm_mix,
           w_mix_in, hgrn_lower_bounds_fwd, hgrn_lower_bounds_bwd, hgrn_out_norm, attn_sink, w_mix_out,
           pre_norm_ffn2, post_norm_ffn2, w_ffn2_gate_up, w_ffn2_down, rel_bias_table):
    batch, seq, d = x.shape
    depth = pre_norm_ffn1.shape[0]
    assert depth == 1 and d == D_MODEL
    xf = x.reshape(batch * seq, d)
    layer = 0
    xf, xn = _ffn_f32_weights(xf, pre_norm_ffn1[layer:layer + 1], post_norm_ffn1[layer:layer + 1],
                              pre_norm_mix[layer:layer + 1], w_ffn1_gate_up[layer], w_ffn1_down[layer])
    w_in = w_mix_in[layer]
    p16 = _proj(xn, w_in, [0, 1, 2, 3, 8, 9, 10, 11, 12])
    y_h, w2_gate_up, w2_down = _hgrn(p16, xn, w_in, hgrn_lower_bounds_fwd, hgrn_lower_bounds_bwd,
                                     hgrn_out_norm[layer:layer + 1], w_ffn2_gate_up[layer], w_ffn2_down[layer],
                                     batch, seq)
    y_a, w_out = _attention(p16, attn_sink[layer:layer + 1], rel_bias_table, w_mix_out[layer], batch, seq)
    xf = _outproj(y_h, y_a, xf, w_out, post_norm_mix[layer:layer + 1])
    xf = _ffn(xf, pre_norm_ffn2[layer:layer + 1], post_norm_ffn2[layer:layer + 1], w2_gate_up, w2_down)
    return xf.reshape(batch, seq, d)
```

```python
import functools
import math

import jax
import jax.numpy as jnp
import numpy as np
from jax import lax
from jax.experimental import pallas as pl
from jax.experimental.pallas import tpu as pltpu

F32 = jnp.float32
BF16 = jnp.bfloat16

D_MODEL = 2048
HGRN_WIDTH = 1024
HEAD_DIM = 128
HGRN_HEADS = HGRN_WIDTH // HEAD_DIM
CHUNK = 64
ATTN_WIDTH = 1024
ATTN_HEADS = ATTN_WIDTH // HEAD_DIM
KV_HEADS = 2
GROUP = ATTN_HEADS // KV_HEADS
WINDOW = 128
ATTN_BLOCK = 128
KEY_SPAN = ATTN_BLOCK + 2 * WINDOW
REL_BUCKETS = 32
REL_MAX_DIST = 128
D_FF = 5632
EPS = 1e-6
NEG_INF = -1e30
LOG2_E = 1.0 / math.log(2.0)

MIB = 1024 * 1024
VMEM_LIMIT_BYTES = 60 * MIB
ROW_BLOCK = 32
EDGE_CHUNK = 256
ACT_COLS = 256
STAGE_LAG = 10
LOGIT_SLAB = 256
ATTN_SCORE_LEAD = 2


def _rms(x, gain):
    return x * lax.rsqrt(jnp.mean(x * x, axis=-1, keepdims=True) + EPS) * gain


def _bf16_weight(w_ref, axis):
    if len(w_ref.shape) == 3:
        return jnp.concatenate([w_ref[c] for c in range(w_ref.shape[0])], axis=axis)
    return w_ref[...].astype(BF16)


def _ffn_kernel(*refs, emit_next_norm, emit_weights, n_passthrough):
    x_ref, gpre_ref, gpost_ref = refs[:3]
    k = 3
    gnext_ref = refs[k] if emit_next_norm else None
    k += int(emit_next_norm)
    wg_ref, wu_ref, wd_ref = refs[k:k + 3]
    k += 3 + n_passthrough
    o_ref, h_ref = refs[k:k + 2]
    j = pl.program_id(1)
    tm = x_ref.shape[0]

    def pre_norm_rows(r0, r1):
        for r in range(r0, r1, ROW_BLOCK):
            rows = pl.ds(r, ROW_BLOCK)
            h_ref[rows, :] = _rms(x_ref[rows, :], gpre_ref[...]).astype(BF16)

    def finish_rows(r0, r1):
        for r in range(r0, r1, ROW_BLOCK):
            rows = pl.ds(r, ROW_BLOCK)
            new_x = x_ref[rows, :] + 0.5 * _rms(o_ref[rows, :], gpost_ref[...])
            o_ref[rows, :] = new_x
            if emit_next_norm:
                h_ref[rows, :] = _rms(new_x, gnext_ref[...]).astype(BF16)

    def swiglu(h, w_gate, w_up):
        acts = []
        for c0 in range(0, w_gate.shape[1], ACT_COLS):
            gate = jnp.dot(h, w_gate[:, c0:c0 + ACT_COLS], preferred_element_type=F32)
            up = jnp.dot(h, w_up[:, c0:c0 + ACT_COLS], preferred_element_type=F32)
            acts.append((0.5 * gate * (1.0 + jnp.tanh(0.5 * gate)) * up).astype(BF16))
        return acts[0] if len(acts) == 1 else jnp.concatenate(acts, axis=1)

    def step(position):
        w_gate, w_up, w_down = _bf16_weight(wg_ref, 1), _bf16_weight(wu_ref, 1), _bf16_weight(wd_ref, 0)
        if emit_weights:
            for w16_ref, w in zip(refs[k + 2:k + 5], (w_gate, w_up, w_down)):
                w16_ref[0] = w
        if position == "first":
            acts = []
            for c0 in range(0, tm, EDGE_CHUNK):
                pre_norm_rows(c0, c0 + EDGE_CHUNK)
                acts.append(swiglu(h_ref[pl.ds(c0, EDGE_CHUNK), :], w_gate, w_up))
            o_ref[...] = jnp.dot(jnp.concatenate(acts, axis=0), w_down, preferred_element_type=F32)
            return
        act = swiglu(h_ref[...], w_gate, w_up)
        if position == "middle":
            o_ref[...] += jnp.dot(act, w_down, preferred_element_type=F32)
            return
        for c0 in range(0, tm, EDGE_CHUNK):
            rows = pl.ds(c0, EDGE_CHUNK)
            o_ref[rows, :] += jnp.dot(act[c0:c0 + EDGE_CHUNK], w_down, preferred_element_type=F32)
            finish_rows(c0, c0 + EDGE_CHUNK)

    last = pl.num_programs(1) - 1
    pl.when(j == 0)(functools.partial(step, "first"))
    pl.when((j > 0) & (j < last))(functools.partial(step, "middle"))
    pl.when(j == last)(functools.partial(step, "last"))


def _ffn(x, gpre, gpost, w_gate_up, w_down, *, tm=1024, tf=512):
    m, d = x.shape
    nj = D_FF // tf
    row_tile = pl.BlockSpec((tm, d), lambda i, j: (i, 0))
    gain_spec = pl.BlockSpec((1, d), lambda i, j: (0, 0))
    return pl.pallas_call(
        functools.partial(_ffn_kernel, emit_next_norm=False, emit_weights=False, n_passthrough=0),
        grid=(m // tm, nj),
        in_specs=[row_tile, gain_spec, gain_spec,
                  pl.BlockSpec((d, tf), lambda i, j: (0, j)),
                  pl.BlockSpec((d, tf), lambda i, j: (0, j + nj)),
                  pl.BlockSpec((tf, d), lambda i, j: (j, 0))],
        out_specs=row_tile,
        out_shape=jax.ShapeDtypeStruct((m, d), F32),
        scratch_shapes=[pltpu.VMEM((tm, d), BF16)],
        compiler_params=pltpu.CompilerParams(
            dimension_semantics=("parallel", "arbitrary"), vmem_limit_bytes=VMEM_LIMIT_BYTES),
        name="ffn",
    )(x, gpre, gpost, w_gate_up, w_gate_up, w_down)


def _ffn_f32_weights(x, gpre, gpost, gnext, w_gate_up, w_down, *, tm=1024, tf=256, slabs_per_step=2):
    m, d = x.shape
    nj = D_FF // tf
    gains = (gpre, gpost, gnext)
    gain_spec = pl.BlockSpec((1, d), lambda i, j: (0, 0))
    params = pltpu.CompilerParams(dimension_semantics=("parallel", "arbitrary"), vmem_limit_bytes=VMEM_LIMIT_BYTES)
    results = (jax.ShapeDtypeStruct((m, d), F32), jax.ShapeDtypeStruct((m, d), BF16))

    first_tile = pl.BlockSpec((tm, d), lambda i, j: (0, 0))
    col_slab = pl.BlockSpec((1, d, tf), lambda i, j: (j, 0, 0))
    row_slab = pl.BlockSpec((1, tf, d), lambda i, j: (j, 0, 0))
    out, xn, wg16, wu16, wd16 = pl.pallas_call(
        functools.partial(_ffn_kernel, emit_next_norm=True, emit_weights=True, n_passthrough=0),
        grid=(1, nj),
        in_specs=[pl.BlockSpec((tm, d), lambda i, j: (0, 0), pipeline_mode=pl.Buffered(1))] + [gain_spec] * 3 + [
            pl.BlockSpec((d, tf), lambda i, j: (0, j)),
            pl.BlockSpec((d, tf), lambda i, j: (0, j + nj)),
            pl.BlockSpec((tf, d), lambda i, j: (j, 0)),
        ],
        out_specs=(first_tile, first_tile, col_slab, col_slab, row_slab),
        out_shape=results + (jax.ShapeDtypeStruct((nj, d, tf), BF16), jax.ShapeDtypeStruct((nj, d, tf), BF16),
                             jax.ShapeDtypeStruct((nj, tf, d), BF16)),
        compiler_params=params,
        name="ffn_first",
    )(x, *gains, w_gate_up, w_gate_up, w_down)

    per = slabs_per_step
    later_tile = pl.BlockSpec((tm, d), lambda i, j: (i + 1, 0))
    untouched = pl.BlockSpec(memory_space=pl.ANY)
    return pl.pallas_call(
        functools.partial(_ffn_kernel, emit_next_norm=True, emit_weights=False, n_passthrough=2),
        grid=(m // tm - 1, nj // per),
        in_specs=[later_tile] + [gain_spec] * 3 + [
            pl.BlockSpec((per, d, tf), lambda i, j: (j, 0, 0)),
            pl.BlockSpec((per, d, tf), lambda i, j: (j, 0, 0)),
            pl.BlockSpec((per, tf, d), lambda i, j: (j, 0, 0)),
            untouched, untouched,
        ],
        out_specs=(later_tile, later_tile),
        out_shape=results,
        input_output_aliases={7: 0, 8: 1},
        compiler_params=params,
        name="ffn_rest",
    )(x, *gains, wg16, wu16, wd16, out, xn)


def _proj_kernel(xn_ref, w_ref, o_ref):
    res = jnp.dot(xn_ref[...], w_ref[...].astype(BF16), preferred_element_type=F32).astype(BF16)
    for c in range(o_ref.shape[0]):
        o_ref[c] = res[:, c * HEAD_DIM:(c + 1) * HEAD_DIM]


def _proj(xn, w_in, col_blocks, *, tm=4096, tn=512):
    m, d = xn.shape
    per = tn // HEAD_DIM
    jumps = [(pos, col_blocks[pos] - col_blocks[pos - 1] - 1) for pos in range(1, len(col_blocks))
             if col_blocks[pos] != col_blocks[pos - 1] + 1]

    def w_block(i, j):
        blk = j + col_blocks[0]
        for pos, gap in jumps:
            blk = blk + jnp.where(j >= pos, gap, 0)
        return (0, blk)

    return pl.pallas_call(
        _proj_kernel,
        grid=(m // tm, len(col_blocks)),
        in_specs=[
            pl.BlockSpec((tm, d), lambda i, j: (i, 0)),
            pl.BlockSpec((d, tn), w_block),
        ],
        out_specs=pl.BlockSpec((per, tm, HEAD_DIM), lambda i, j: (j, i, 0)),
        out_shape=jax.ShapeDtypeStruct((len(col_blocks) * per, m, HEAD_DIM), BF16),
        compiler_params=pltpu.CompilerParams(
            dimension_semantics=("parallel", "arbitrary"), vmem_limit_bytes=VMEM_LIMIT_BYTES),
        name="inproj",
    )(xn, w_in)


def _chunk_cumsum(x, row_in_chunk, reverse):
    n_rows = x.shape[0]
    c = x
    s = 1
    while s < CHUNK:
        if reverse:
            shifted = pltpu.roll(c, n_rows - s, axis=0)
            keep = row_in_chunk < CHUNK - s
        else:
            shifted = pltpu.roll(c, s, axis=0)
            keep = row_in_chunk >= s
        c = c + jnp.where(keep, shifted, 0.0)
        s *= 2
    return c


def _hgrn_kernel(q_ref, v_ref, xn_ref, wff_ref, wfb_ref, g_ref, lbf_ref, lbb_ref, gain_ref, wa_ref, wb_ref,
                 o_ref, wa16_ref, wb16_ref, qd_ref, kv_ref, dec_ref, st_ref, oi_ref):
    wa16_ref[...] = wa_ref[...].astype(BF16)
    wb16_ref[...] = wb_ref[...].astype(BF16)

    seq = q_ref.shape[0]
    n_chunks = seq // CHUNK
    pair = 2 * CHUNK
    contract_last = (((1,), (1,)), ((), ()))

    def gate_consts(lb_param_ref):
        a = lb_param_ref[...]
        e = jnp.exp(a - jnp.max(a, axis=0, keepdims=True))
        lb = e[0:1] / jnp.sum(e, axis=0, keepdims=True)
        return 0.5 * (1.0 + lb), 0.5 * (1.0 - lb)

    consts = (gate_consts(lbf_ref), gate_consts(lbb_ref))
    w_logits = jnp.concatenate([wff_ref[...].astype(BF16), wfb_ref[...].astype(BF16)], axis=1)

    r = lax.broadcasted_iota(jnp.int32, (pair, pair), 0)
    s = lax.broadcasted_iota(jnp.int32, (pair, pair), 1)
    same_chunk = (r // CHUNK) == (s // CHUNK)
    tri = (same_chunk & (r >= s), same_chunk & (s >= r))
    row_in_chunk = lax.broadcasted_iota(jnp.int32, (CHUNK, HEAD_DIM), 0)
    zero_blk = jnp.zeros((CHUNK, HEAD_DIM), BF16)

    n_pairs = n_chunks // 2
    rows_of = [pl.ds(i * pair, pair) for i in range(n_pairs)]
    pairs_per_slab = LOGIT_SLAB // pair
    n_slabs = seq // LOGIT_SLAB

    def project_logits(slab):
        return jnp.dot(xn_ref[pl.ds(slab * LOGIT_SLAB, LOGIT_SLAB), :], w_logits, preferred_element_type=F32)

    def gate_stage(i):
        slab, offset = divmod(i, pairs_per_slab)
        per_dir = []
        for d in range(2):
            mid, half = consts[d]
            ks, cs = [], []
            for r0 in range(offset * pair, (offset + 1) * pair, CHUNK):
                z = logits[slab][r0:r0 + CHUNK, d * HEAD_DIM:(d + 1) * HEAD_DIM]
                t = jnp.tanh(0.5 * z)
                f = mid + half * t
                ks.append(half * (1.0 - t))
                cs.append(_chunk_cumsum(jnp.log(f) * LOG2_E, row_in_chunk, reverse=(d == 1)))
            per_dir.append((jnp.concatenate(ks, axis=0), jnp.concatenate(cs, axis=0)))
        if offset == pairs_per_slab - 1 and slab + 2 < n_slabs:
            logits.append(project_logits(slab + 2))
        return per_dir

    def mix_stage(i, per_dir):
        rows = rows_of[i]
        q = q_ref[rows, :].astype(F32)
        vb = v_ref[rows, :]
        probs = None
        for d, (k, c) in enumerate(per_dir):
            c3 = c.reshape(2, CHUNK, HEAD_DIM)
            tot = c3[:, 0:1, :] if d == 1 else c3[:, CHUNK - 1:CHUNK, :]
            k_tail = (k * jnp.exp2(tot - c3).reshape(pair, HEAD_DIM)).astype(BF16)
            q_dec = (q * jnp.exp2(c)).astype(BF16)
            k_dec = (k * jnp.exp2(-c)).astype(BF16)
            qd_ref[rows, d * HEAD_DIM:(d + 1) * HEAD_DIM] = q_dec
            dec_ref[d, pl.ds(i * 2, 2), :] = jnp.exp2(tot).reshape(2, HEAD_DIM)
            sc = lax.dot_general(q_dec, k_dec, contract_last, preferred_element_type=F32)
            sc = jnp.where(tri[d], sc, 0.0)
            probs = sc if probs is None else probs + sc
            rhs = jnp.concatenate(
                [jnp.concatenate([k_tail[:CHUNK], zero_blk], axis=1),
                 jnp.concatenate([zero_blk, k_tail[CHUNK:]], axis=1)], axis=0)
            kv = lax.dot_general(vb, rhs, (((0,), (0,)), ((), ())), preferred_element_type=F32)
            kv_ref[d, i * 2] = kv[:, :HEAD_DIM]
            kv_ref[d, i * 2 + 1] = kv[:, HEAD_DIM:]
        return probs.astype(BF16), vb

    logits = [project_logits(s) for s in range(min(2, n_slabs))]
    gates, mixes = {}, {}
    for t in range(n_pairs + 2 * STAGE_LAG):
        if t < n_pairs:
            gates[t] = gate_stage(t)
        i = t - STAGE_LAG
        if 0 <= i < n_pairs:
            mixes[i] = mix_stage(i, gates.pop(i))
        i = t - 2 * STAGE_LAG
        if 0 <= i < n_pairs:
            probs, vb = mixes.pop(i)
            oi_ref[rows_of[i], :] = jnp.dot(probs, vb, preferred_element_type=F32)

    sf = sb = jnp.zeros((HEAD_DIM, HEAD_DIM), F32)
    for nf in range(n_chunks):
        nb = n_chunks - 1 - nf
        st_ref[nf, :, :HEAD_DIM] = sf.astype(BF16)
        st_ref[nb, :, HEAD_DIM:] = sb.astype(BF16)
        sf = sf * dec_ref[0, pl.ds(nf, 1), :] + kv_ref[0, nf]
        sb = sb * dec_ref[1, pl.ds(nb, 1), :] + kv_ref[1, nb]

    chunk_rows = [pl.ds(n * CHUNK, CHUNK) for n in range(n_chunks)]
    inter = [lax.dot_general(qd_ref[rows, :], st_ref[n], contract_last, preferred_element_type=F32)
             for n, rows in enumerate(chunk_rows)]
    for rows, o_inter in zip(chunk_rows, inter):
        o = _rms(oi_ref[rows, :] + o_inter, gain_ref[...])
        g = g_ref[rows, :].astype(F32)
        o_ref[rows, :] = (o * (0.5 * g * (1.0 + jnp.tanh(0.5 * g)))).astype(o_ref.dtype)


def _hgrn(p16, xn, w_in, lbf, lbb, out_gain, w_a, w_b, batch, seq):
    hw = HGRN_HEADS
    steps = batch * hw
    d = xn.shape[1]

    def w_col(group):
        return lambda b, h: (0, group * hw + h)

    def slab(group):
        return lambda b, h: (group * hw + h, b, 0)

    def row_slab(w):
        assert w.shape[0] % (steps * 16) == 0
        return pl.BlockSpec((w.shape[0] // steps, w.shape[1]), lambda b, h: (b * hw + h, 0))

    blk = (None, seq, HEAD_DIM)
    return pl.pallas_call(
        _hgrn_kernel,
        grid=(batch, hw),
        in_specs=[
            pl.BlockSpec(blk, slab(0)),
            pl.BlockSpec(blk, slab(1)),
            pl.BlockSpec((seq, d), lambda b, h: (b, 0)),
            pl.BlockSpec((d, HEAD_DIM), w_col(2)),
            pl.BlockSpec((d, HEAD_DIM), w_col(3)),
            pl.BlockSpec(blk, slab(2)),
            pl.BlockSpec((lbf.shape[0], HEAD_DIM), lambda b, h: (0, h)),
            pl.BlockSpec((lbb.shape[0], HEAD_DIM), lambda b, h: (0, h)),
            pl.BlockSpec((1, HEAD_DIM), lambda b, h: (0, h)),
            row_slab(w_a),
            row_slab(w_b),
        ],
        out_specs=(pl.BlockSpec(blk, lambda b, h: (h, b, 0)), row_slab(w_a), row_slab(w_b)),
        out_shape=(jax.ShapeDtypeStruct((hw, batch * seq, HEAD_DIM), BF16),
                   jax.ShapeDtypeStruct(w_a.shape, BF16), jax.ShapeDtypeStruct(w_b.shape, BF16)),
        scratch_shapes=[
            pltpu.VMEM((seq, 2 * HEAD_DIM), BF16),
            pltpu.VMEM((2, seq // CHUNK, HEAD_DIM, HEAD_DIM), F32),
            pltpu.VMEM((2, seq // CHUNK, HEAD_DIM), F32),
            pltpu.VMEM((seq // CHUNK, HEAD_DIM, 2 * HEAD_DIM), BF16),
            pltpu.VMEM((seq, HEAD_DIM), F32),
        ],
        compiler_params=pltpu.CompilerParams(
            dimension_semantics=("parallel", "parallel"), vmem_limit_bytes=VMEM_LIMIT_BYTES),
        name="hgrn",
    )(p16, p16, xn, w_in, w_in, p16, lbf, lbb, out_gain, w_a, w_b)


def _t5_bucket_table():
    nb = REL_BUCKETS // 2
    max_exact = nb // 2
    c = np.arange(ATTN_BLOCK)[:, None]
    s = np.arange(KEY_SPAN)[None, :]
    rel = s - WINDOW - c
    bucket = (rel > 0).astype(np.int32) * nb
    n = np.abs(rel)
    large = max_exact + (np.log(np.maximum(n, 1) / max_exact) / np.log(REL_MAX_DIST / max_exact)
                         * (nb - max_exact)).astype(np.int32)
    large = np.minimum(large, nb - 1)
    bucket = bucket + np.where(n < max_exact, n, large).astype(np.int32)
    return np.where(np.abs(rel) <= WINDOW, bucket, -1).astype(np.int32)


def _attn_kernel(table_ref, sink_ref, bucket_ref, q_ref, k_ref, v_ref, w_ref, o_ref, w16_ref, bias_ref):
    w16_ref[...] = w_ref[...].astype(BF16)

    x = pl.program_id(1)
    seq = k_ref.shape[0]
    n_blocks = seq // ATTN_BLOCK
    scale = LOG2_E / math.sqrt(HEAD_DIM)
    rows = GROUP * ATTN_BLOCK

    @pl.when((pl.program_id(0) == 0) & (x == 0))
    def _():
        bucket = bucket_ref[...]
        for head in range(ATTN_HEADS):
            bias = jnp.full((ATTN_BLOCK, KEY_SPAN), NEG_INF, F32)
            for b in range(REL_BUCKETS):
                bias = jnp.where(bucket == b, table_ref[b, head] * LOG2_E, bias)
            bias_ref[pl.ds(head * ATTN_BLOCK, ATTN_BLOCK), :] = bias

    row_head = lax.broadcasted_iota(jnp.int32, (rows, 1), 0) // ATTN_BLOCK
    sink = jnp.zeros((rows, 1), F32)
    for g in range(GROUP):
        sink = jnp.where(row_head == g, sink_ref[0, x * GROUP + g] * LOG2_E, sink)
    bias_rows = pl.ds(pl.multiple_of(x * rows, rows), rows)

    def block_rows(i):
        return pl.ds(i * ATTN_BLOCK, ATTN_BLOCK)

    def band(ref, i):
        return jnp.concatenate([ref[block_rows(max(i - 1, 0)), :], ref[block_rows(i), :],
                                ref[block_rows(min(i + 1, n_blocks - 1)), :]], axis=0)

    def scores(i):
        qs = jnp.concatenate([q_ref[g, block_rows(i), :] for g in range(GROUP)], axis=0)
        sc = lax.dot_general(qs, band(k_ref, i), (((1,), (1,)), ((), ())), preferred_element_type=F32)
        sc = sc * scale + bias_ref[bias_rows, :]
        if i in (0, n_blocks - 1):
            key_pos = (i - 1) * ATTN_BLOCK + lax.broadcasted_iota(jnp.int32, (rows, KEY_SPAN), 1)
            sc = jnp.where((key_pos >= 0) & (key_pos < seq), sc, NEG_INF)
        return sc

    def finish(i, sc):
        m = jnp.maximum(jnp.max(sc, axis=-1, keepdims=True), sink)
        pr = jnp.exp2(sc - m)
        den = jnp.sum(pr, axis=-1, keepdims=True) + jnp.exp2(sink - m)
        o = jnp.dot(pr.astype(BF16), band(v_ref, i), preferred_element_type=F32) / den
        for g in range(GROUP):
            o_ref[g, block_rows(i), :] = o[g * ATTN_BLOCK:(g + 1) * ATTN_BLOCK].astype(o_ref.dtype)

    logits = {i: scores(i) for i in range(ATTN_SCORE_LEAD)}
    for i in range(n_blocks):
        if i + ATTN_SCORE_LEAD < n_blocks:
            logits[i + ATTN_SCORE_LEAD] = scores(i + ATTN_SCORE_LEAD)
        finish(i, logits.pop(i))


def _attention(p16, sink, rel_table, w, batch, seq):
    q_blk0 = 3 * HGRN_HEADS // GROUP
    k_slab0 = 3 * HGRN_HEADS + ATTN_HEADS
    v_slab0 = k_slab0 + KV_HEADS
    bucket = jnp.asarray(_t5_bucket_table())
    smem = pl.BlockSpec(memory_space=pltpu.SMEM)
    assert w.shape[0] % (batch * KV_HEADS * 16) == 0
    w_slab = pl.BlockSpec((w.shape[0] // (batch * KV_HEADS), w.shape[1]), lambda b, x: (b * KV_HEADS + x, 0))
    return pl.pallas_call(
        _attn_kernel,
        grid=(batch, KV_HEADS),
        in_specs=[
            smem,
            smem,
            pl.BlockSpec((ATTN_BLOCK, KEY_SPAN), lambda b, x: (0, 0)),
            pl.BlockSpec((GROUP, seq, HEAD_DIM), lambda b, x: (q_blk0 + x, b, 0)),
            pl.BlockSpec((None, seq, HEAD_DIM), lambda b, x: (k_slab0 + x, b, 0)),
            pl.BlockSpec((None, seq, HEAD_DIM), lambda b, x: (v_slab0 + x, b, 0)),
            w_slab,
        ],
        out_specs=(pl.BlockSpec((GROUP, seq, HEAD_DIM), lambda b, x: (x, b, 0)), w_slab),
        out_shape=(jax.ShapeDtypeStruct((ATTN_HEADS, batch * seq, HEAD_DIM), BF16),
                   jax.ShapeDtypeStruct(w.shape, BF16)),
        scratch_shapes=[
            pltpu.VMEM((ATTN_HEADS * ATTN_BLOCK, KEY_SPAN), F32),
        ],
        compiler_params=pltpu.CompilerParams(
            dimension_semantics=("arbitrary", "arbitrary"), vmem_limit_bytes=VMEM_LIMIT_BYTES),
        name="attention",
    )(rel_table, sink, bucket, p16, p16, p16, w)


def _outproj_kernel(yh_ref, ya_ref, x_ref, wh_ref, wa_ref, g_ref, o_ref):
    y_h = jnp.concatenate([yh_ref[c] for c in range(yh_ref.shape[0])], axis=1)
    y_a = jnp.concatenate([ya_ref[c] for c in range(ya_ref.shape[0])], axis=1)
    mixed = jnp.dot(y_h, wh_ref[...], preferred_element_type=F32)
    mixed = mixed + jnp.dot(y_a, wa_ref[...], preferred_element_type=F32)
    o_ref[...] = x_ref[...] + _rms(mixed, g_ref[...])


def _outproj(y_h, y_a, x, w_out, gain, *, tm=512):
    m, d = x.shape
    return pl.pallas_call(
        _outproj_kernel,
        grid=(m // tm,),
        in_specs=[
            pl.BlockSpec((HGRN_HEADS, tm, HEAD_DIM), lambda i: (0, i, 0)),
            pl.BlockSpec((ATTN_HEADS, tm, HEAD_DIM), lambda i: (0, i, 0)),
            pl.BlockSpec((tm, d), lambda i: (i, 0)),
            pl.BlockSpec((HGRN_WIDTH, d), lambda i: (0, 0)),
            pl.BlockSpec((ATTN_WIDTH, d), lambda i: (1, 0)),
            pl.BlockSpec((1, d), lambda i: (0, 0)),
        ],
        out_specs=pl.BlockSpec((tm, d), lambda i: (i, 0)),
        out_shape=jax.ShapeDtypeStruct((m, d), F32),
        compiler_params=pltpu.CompilerParams(
            dimension_semantics=("parallel",), vmem_limit_bytes=VMEM_LIMIT_BYTES),
        name="outproj",
    )(y_h, y_a, x, w_out, w_out, gain)


def kernel(x, pre_norm_ffn1, post_norm_ffn1, w_ffn1_gate_up, w_ffn1_down, pre_norm_mix, post_norm_mix,
           w_mix_in, hgrn_lower_bounds_fwd, hgrn_lower_bounds_bwd, hgrn_out_norm, attn_sink, w_mix_out,
           pre_norm_ffn2, post_norm_ffn2, w_ffn2_gate_up, w_ffn2_down, rel_bias_table):
    batch, seq, d = x.shape
    depth = pre_norm_ffn1.shape[0]
    assert depth == 1 and d == D_MODEL
    xf = x.reshape(batch * seq, d)
    layer = 0
    xf, xn = _ffn_f32_weights(xf, pre_norm_ffn1[layer:layer + 1], post_norm_ffn1[layer:layer + 1],
                              pre_norm_mix[layer:layer + 1], w_ffn1_gate_up[layer], w_ffn1_down[layer])
    w_in = w_mix_in[layer]
    p16 = _proj(xn, w_in, [0, 1, 2, 3, 8, 9, 10, 11, 12])
    y_h, w2_gate_up, w2_down = _hgrn(p16, xn, w_in, hgrn_lower_bounds_fwd, hgrn_lower_bounds_bwd,
                                     hgrn_out_norm[layer:layer + 1], w_ffn2_gate_up[layer], w_ffn2_down[layer],
                                     batch, seq)
    y_a, w_out = _attention(p16, attn_sink[layer:layer + 1], rel_bias_table, w_mix_out[layer], batch, seq)
    xf = _outproj(y_h, y_a, xf, w_out, post_norm_mix[layer:layer + 1])
    xf = _ffn(xf, pre_norm_ffn2[layer:layer + 1], post_norm_ffn2[layer:layer + 1], w2_gate_up, w2_down)
    return xf.reshape(batch, seq, d)
```
